```python
import jax, jax.numpy as jnp
from jax import lax
import numpy as np

D_MODEL = 1024
BATCH = 8
SEQ = 2048
DEPTH = 2

CHUNK = 64
HEAD_DIM = 64
EPS = 1e-6
NEG_INF = -1e30

A_HEADS = 8
A_KV_HEADS = 2
A_WINDOW = 128
A_PREV_CHUNKS = A_WINDOW // CHUNK
A_BAND_CHUNKS = A_PREV_CHUNKS + 1

B_HEADS = 8
B_BLOCK = 128
FORGET_BIAS_INIT = 3.0

C_HEADS = 8
C_PREV_CHUNKS = 8
C_BAND_CHUNKS = C_PREV_CHUNKS + 1
REL_CLIP = 128
N_REL = 2 * REL_CLIP + 1

N_BRANCH = 3
BRANCH_WIDTH = 8 * HEAD_DIM
FFN_HIDDEN = ((-(-8 * D_MODEL // 3)) + 255) // 256 * 256

IN_SPLIT_SIZES = (
    A_HEADS * HEAD_DIM, A_KV_HEADS * HEAD_DIM, A_KV_HEADS * HEAD_DIM,
    B_HEADS * HEAD_DIM, B_HEADS * HEAD_DIM, B_HEADS * HEAD_DIM, B_HEADS,
    C_HEADS * HEAD_DIM, C_HEADS * HEAD_DIM, C_HEADS * HEAD_DIM,
    N_BRANCH * D_MODEL,
)
N_IN_COLS = sum(IN_SPLIT_SIZES)

kernel_name = "chunk_causal_hybrid_swa_fox_relpos_adaln"


def rms_norm(x, g):
    xf = x.astype(jnp.float32)
    y = xf * lax.rsqrt(jnp.mean(xf * xf, axis=-1, keepdims=True) + EPS)
    return (y * g.astype(jnp.float32)).astype(x.dtype)


def modulate(h, shift, scale):
    return h * (1.0 + scale[:, None, :]) + shift[:, None, :]


def alibi_slopes(n_heads):
    return jnp.exp2(-8.0 * jnp.arange(1, n_heads + 1, dtype=jnp.float32) / n_heads)


def sliding_window_sink_attention(q, k, v, sinks):
    b, s, _, dh = q.shape
    nc = s // CHUNK
    g = A_HEADS // A_KV_HEADS
    band_len = A_BAND_CHUNKS * CHUNK
    qc = q.reshape(b, nc, CHUNK, A_KV_HEADS, g, dh)

    def band(t):
        tp = jnp.pad(t, ((0, 0), (A_PREV_CHUNKS * CHUNK, 0), (0, 0), (0, 0)))
        tp = tp.reshape(b, nc + A_PREV_CHUNKS, CHUNK, A_KV_HEADS, dh)
        return jnp.concatenate([tp[:, j:j + nc] for j in range(A_BAND_CHUNKS)], axis=2)

    kb, vb = band(k), band(v)
    scores = jnp.einsum('bnqkgd,bnskd->bnkgqs', qc, kb).astype(jnp.float32) * (dh ** -0.5)
    qi = jnp.arange(CHUNK)
    si = jnp.arange(band_len)
    dist = A_PREV_CHUNKS * CHUNK + qi[:, None] - si[None, :]
    alibi = -alibi_slopes(A_HEADS).reshape(A_KV_HEADS, g, 1, 1) * jnp.abs(dist).astype(jnp.float32)
    key_chunk = jnp.arange(nc)[:, None] - A_PREV_CHUNKS + si[None, :] // CHUNK
    valid = (key_chunk >= 0)[None, :, None, None, None, :]
    scores = jnp.where(valid, scores + alibi, NEG_INF)
    sink_col = jnp.broadcast_to(sinks.astype(jnp.float32).reshape(1, 1, A_KV_HEADS, g, 1, 1),
                                scores.shape[:-1] + (1,))
    probs = jax.nn.softmax(jnp.concatenate([scores, sink_col], axis=-1), axis=-1)[..., :-1]
    out = jnp.einsum('bnkgqs,bnskd->bnqkgd', probs.astype(v.dtype), vb)
    return out.reshape(b, s, A_HEADS * dh)


def forgetting_attention(q, k, v, f_logit):
    b, s, h, dh = q.shape
    log_f = jax.nn.log_sigmoid(f_logit.astype(jnp.float32))
    cum = lax.cumsum(log_f, axis=1).transpose(0, 2, 1)
    kpos = jnp.arange(s)
    scale = dh ** -0.5

    def block(i):
        start = i * B_BLOCK
        qb = lax.dynamic_slice_in_dim(q, start, B_BLOCK, axis=1)
        cq = lax.dynamic_slice_in_dim(cum, start, B_BLOCK, axis=2)
        sc = jnp.einsum('bqhd,bshd->bhqs', qb, k).astype(jnp.float32) * scale
        sc = sc + cq[..., :, None] - cum[:, :, None, :]
        qpos = start + jnp.arange(B_BLOCK)
        sc = jnp.where(kpos[None, :] <= qpos[:, None], sc, NEG_INF)
        p = jax.nn.softmax(sc, axis=-1)
        return jnp.einsum('bhqs,bshd->bqhd', p.astype(v.dtype), v)

    out = lax.map(block, jnp.arange(s // B_BLOCK))
    return out.transpose(1, 0, 2, 3, 4).reshape(b, s, h * dh)


def chunked_relpos_attention(q, k, v, rel_table):
    b, s, h, dh = q.shape
    nc = s // CHUNK
    band_len = C_BAND_CHUNKS * CHUNK
    pad = C_PREV_CHUNKS * CHUNK
    kp = jnp.pad(k, ((0, 0), (pad, 0), (0, 0), (0, 0)))
    vp = jnp.pad(v, ((0, 0), (pad, 0), (0, 0), (0, 0)))
    qi = jnp.arange(CHUNK)
    si = jnp.arange(band_len)
    dist = pad + qi[:, None] - si[None, :]
    rel_idx = jnp.clip(dist, -REL_CLIP, REL_CLIP) + REL_CLIP
    bias = rel_table[:, rel_idx].astype(jnp.float32)
    scale = dh ** -0.5

    def chunk(n):
        qc = lax.dynamic_slice_in_dim(q, n * CHUNK, CHUNK, axis=1)
        kc = lax.dynamic_slice_in_dim(kp, n * CHUNK, band_len, axis=1)
        vc = lax.dynamic_slice_in_dim(vp, n * CHUNK, band_len, axis=1)
        sc = jnp.einsum('bqhd,bshd->bhqs', qc, kc).astype(jnp.float32) * scale + bias
        valid = (n * CHUNK - pad + si) >= 0
        sc = jnp.where(valid[None, None, None, :], sc, NEG_INF)
        p = jax.nn.softmax(sc, axis=-1)
        return jnp.einsum('bhqs,bshd->bqhd', p.astype(vc.dtype), vc)

    out = lax.map(chunk, jnp.arange(nc))
    return out.transpose(1, 0, 2, 3, 4).reshape(b, s, h * dh)


def hybrid_mixer(h, w_in, b_forget, sinks, rel_table, w_branch, w_out):
    b, s, _ = h.shape
    proj = jnp.einsum('bsd,de->bse', h, w_in)
    split_points = [int(p) for p in np.cumsum(IN_SPLIT_SIZES)[:-1]]
    qa, ka, va, qb, kb, vb, fb, qc, kc, vc, gates = jnp.split(proj, split_points, axis=-1)
    heads = lambda t, n: t.reshape(b, s, n, HEAD_DIM)
    o_a = sliding_window_sink_attention(heads(qa, A_HEADS), heads(ka, A_KV_HEADS),
                                        heads(va, A_KV_HEADS), sinks)
    o_b = forgetting_attention(heads(qb, B_HEADS), heads(kb, B_HEADS), heads(vb, B_HEADS),
                               fb + b_forget)
    o_c = chunked_relpos_attention(heads(qc, C_HEADS), heads(kc, C_HEADS), heads(vc, C_HEADS),
                                   rel_table)
    branches = jnp.stack([o_a, o_b, o_c], axis=2)
    y = jnp.einsum('bskw,kwd->bskd', branches, w_branch)
    g = jax.nn.sigmoid(gates.reshape(b, s, N_BRANCH, D_MODEL))
    merged = jnp.sum(g * y, axis=2)
    return jnp.einsum('bsd,de->bse', merged, w_out)


def swiglu(h, w_ffn_in, w_ffn_out):
    u = jnp.einsum('bsd,df->bsf', h, w_ffn_in)
    gate, up = jnp.split(u, 2, axis=-1)
    return jnp.einsum('bsf,fd->bsd', jax.nn.silu(gate) * up, w_ffn_out)


def setup_inputs(seed: int = 0) -> dict:
    key = jax.random.key(seed)
    ks = jax.random.split(key, 16)
    f32 = jnp.float32
    nrm = lambda k, shape, sd: jax.random.normal(k, shape, f32) * sd
    return {
        "x": nrm(ks[0], (BATCH, SEQ, D_MODEL), 1.0),
        "c": nrm(ks[1], (BATCH, D_MODEL), 1.0),
        "norm_mix_g": 1.0 + nrm(ks[2], (DEPTH, D_MODEL), 0.02),
        "norm_ffn_g": 1.0 + nrm(ks[3], (DEPTH, D_MODEL), 0.02),
        "w_ada": nrm(ks[4], (DEPTH, D_MODEL, 6 * D_MODEL), 0.5 * D_MODEL ** -0.5),
        "b_ada": nrm(ks[5], (DEPTH, 6 * D_MODEL), 0.02),
        "w_in": nrm(ks[6], (DEPTH, D_MODEL, N_IN_COLS), D_MODEL ** -0.5),
        "b_forget": FORGET_BIAS_INIT + nrm(ks[7], (DEPTH, B_HEADS), 0.5),
        "sinks": nrm(ks[8], (DEPTH, A_HEADS), 0.5),
        "rel_bias": nrm(ks[9], (DEPTH, C_HEADS, N_REL), 0.1),
        "w_branch": nrm(ks[10], (DEPTH, N_BRANCH, BRANCH_WIDTH, D_MODEL), BRANCH_WIDTH ** -0.5),
        "w_out": nrm(ks[11], (DEPTH, D_MODEL, D_MODEL), D_MODEL ** -0.5),
        "w_ffn_in": nrm(ks[12], (DEPTH, D_MODEL, 2 * FFN_HIDDEN), D_MODEL ** -0.5),
        "w_ffn_out": nrm(ks[13], (DEPTH, FFN_HIDDEN, D_MODEL), FFN_HIDDEN ** -0.5),
        "final_norm_g": 1.0 + nrm(ks[14], (D_MODEL,), 0.02),
    }


def reference(x, c, norm_mix_g, norm_ffn_g, w_ada, b_ada, w_in, b_forget, sinks, rel_bias,
              w_branch, w_out, w_ffn_in, w_ffn_out, final_norm_g):
    cond = jax.nn.silu(c)
    for l in range(DEPTH):
        mod = jnp.einsum('bd,de->be', cond, w_ada[l]) + b_ada[l]
        sh_m, sc_m, g_m, sh_f, sc_f, g_f = jnp.split(mod, 6, axis=-1)
        h = modulate(rms_norm(x, norm_mix_g[l]), sh_m, sc_m)
        x = x + g_m[:, None, :] * hybrid_mixer(h, w_in[l], b_forget[l], sinks[l], rel_bias[l],
                                               w_branch[l], w_out[l])
        h = modulate(rms_norm(x, norm_ffn_g[l]), sh_f, sc_f)
        x = x + g_f[:, None, :] * swiglu(h, w_ffn_in[l], w_ffn_out[l])
    return rms_norm(x, final_norm_g)
```

```python
import functools

import jax
import jax.numpy as jnp
from jax import lax
from jax.experimental import pallas as pl
from jax.experimental.pallas import tpu as pltpu

F32 = jnp.float32
BF16 = jnp.bfloat16

D_MODEL = 1024
BATCH = 8
SEQ = 2048
TOKENS = BATCH * SEQ
DEPTH = 2
CHUNK = 64
HEAD_DIM = 64
EPS = 1e-6
NEG_INF = -1e30

A_HEADS = 8
A_KV_HEADS = 2
A_GROUP = A_HEADS // A_KV_HEADS
A_PREV = 2
A_BAND = (A_PREV + 1) * CHUNK
B_HEADS = 8
C_HEADS = 8
C_PREV = 8
C_PAD = C_PREV * CHUNK
C_BAND = (C_PREV + 1) * CHUNK
REL_CLIP = 128
N_REL = 2 * REL_CLIP + 1
BRANCH_WIDTH = 512
FFN_HIDDEN = 2816
N_CHUNKS = SEQ // CHUNK

LANES = 128
N_PAIRS = 4

PROJ_COLS = 3840
QA_BLK, KA_BLK, VA_BLK = 0, 4, 5
QB_BLK, KB_BLK, VB_BLK = 6, 10, 14
QC_BLK, KC_BLK, VC_BLK = 18, 22, 26

TM = 512
FOX_T = 256
CUM_BLK = 256

VMEM_LIMIT = 56 * 1024 * 1024


def _params(n_axes):
    return pltpu.CompilerParams(dimension_semantics=("arbitrary",) * n_axes,
                                vmem_limit_bytes=VMEM_LIMIT)


def _rms_mod(x, g, shift, scale):
    ms = jnp.mean(x * x, axis=-1, keepdims=True)
    y = x * lax.rsqrt(ms + EPS) * g
    return y * (1.0 + scale) + shift


def _lane_half(shape):
    return lax.broadcasted_iota(jnp.int32, shape, len(shape) - 1) < HEAD_DIM


def _ada_kernel(c_ref, w_ref, b_ref, o_ref):
    c = c_ref[...]
    cond = c * jax.nn.sigmoid(c)
    o_ref[...] = jnp.dot(cond.astype(BF16), w_ref[...].astype(BF16),
                         preferred_element_type=F32) + b_ref[...]


def _ada_mod(c, w_ada, b_ada):
    n_blk = 6
    out = pl.pallas_call(
        _ada_kernel,
        grid=(DEPTH, n_blk),
        in_specs=[
            pl.BlockSpec((BATCH, D_MODEL), lambda l, j: (0, 0)),
            pl.BlockSpec((None, D_MODEL, D_MODEL), lambda l, j: (l, 0, j)),
            pl.BlockSpec((None, 1, D_MODEL), lambda l, j: (l, 0, j)),
        ],
        out_specs=pl.BlockSpec((None, BATCH, D_MODEL), lambda l, j: (l, 0, j)),
        out_shape=jax.ShapeDtypeStruct((DEPTH, BATCH, n_blk * D_MODEL), F32),
        compiler_params=_params(2),
        name="ada_mod",
    )(c, w_ada, b_ada.reshape(DEPTH, 1, n_blk * D_MODEL))
    return out.reshape(DEPTH, BATCH, n_blk, D_MODEL)


def _relbias_kernel(tab_ref, o_ref):
    l = pl.program_id(0)
    h = pl.program_id(1)
    qi = lax.broadcasted_iota(jnp.int32, (CHUNK, C_BAND), 0)
    si = lax.broadcasted_iota(jnp.int32, (CHUNK, C_BAND), 1)
    idx = jnp.clip(C_PAD + qi - si, -REL_CLIP, REL_CLIP) + REL_CLIP

    def body(r, acc):
        return jnp.where(idx == r, tab_ref[l, h, r], acc)

    o_ref[...] = lax.fori_loop(0, N_REL, body, jnp.zeros((CHUNK, C_BAND), F32))


def _rel_bias(rel_bias):
    out = pl.pallas_call(
        _relbias_kernel,
        grid=(DEPTH, C_HEADS),
        in_specs=[pl.BlockSpec(memory_space=pltpu.SMEM)],
        out_specs=pl.BlockSpec((None, None, CHUNK, C_BAND), lambda l, h: (l, h, 0, 0)),
        out_shape=jax.ShapeDtypeStruct((DEPTH, C_HEADS, CHUNK, C_BAND), F32),
        compiler_params=_params(2),
        name="rel_bias",
    )(rel_bias)
    return out.reshape(DEPTH, N_PAIRS, 2 * CHUNK, C_BAND)


def _inproj_kernel(x_ref, mod_ref, g_ref, w_ref, wfb_ref, proj_ref, fb_ref, *, tn):
    h = _rms_mod(x_ref[...], g_ref[...], mod_ref[0:1, :], mod_ref[1:2, :]).astype(BF16)
    for j in range(PROJ_COLS // tn):
        sl = slice(j * tn, (j + 1) * tn)
        proj_ref[:, sl] = jnp.dot(h, w_ref[:, sl], preferred_element_type=F32).astype(BF16)
    fb_ref[...] = jnp.dot(h, wfb_ref[...], preferred_element_type=F32)


def _in_proj(x, mod, gain, w_qkv, w_fb, layer):
    tn = 768
    steps_per_batch = SEQ // TM
    return pl.pallas_call(
        functools.partial(_inproj_kernel, tn=tn),
        grid=(TOKENS // TM,),
        in_specs=[
            pl.BlockSpec((TM, D_MODEL), lambda i: (i, 0)),
            pl.BlockSpec((None, None, 6, D_MODEL), lambda i: (layer, i // steps_per_batch, 0, 0)),
            pl.BlockSpec((None, 1, D_MODEL), lambda i: (layer, 0, 0)),
            pl.BlockSpec((None, D_MODEL, PROJ_COLS), lambda i: (layer, 0, 0)),
            pl.BlockSpec((None, D_MODEL, LANES), lambda i: (layer, 0, 0)),
        ],
        out_specs=[
            pl.BlockSpec((TM, PROJ_COLS), lambda i: (i, 0)),
            pl.BlockSpec((TM, LANES), lambda i: (i, 0)),
        ],
        out_shape=[
            jax.ShapeDtypeStruct((TOKENS, PROJ_COLS), BF16),
            jax.ShapeDtypeStruct((TOKENS, LANES), F32),
        ],
        compiler_params=_params(1),
        name="in_proj",
    )(x, mod, gain, w_qkv, w_fb)


def _cumsum_kernel(fb_ref, bias_ref, col_ref, row_ref):
    r = lax.broadcasted_iota(jnp.int32, (CUM_BLK, CUM_BLK), 0)
    c = lax.broadcasted_iota(jnp.int32, (CUM_BLK, CUM_BLK), 1)
    tri = (r >= c).astype(F32)
    carry = jnp.zeros((1, LANES), F32)
    for blk in range(SEQ // CUM_BLK):
        rows = slice(blk * CUM_BLK, (blk + 1) * CUM_BLK)
        z = fb_ref[0, rows, :] + bias_ref[...]
        log_f = jnp.minimum(z, 0.0) - jnp.log1p(jnp.exp(-jnp.abs(z)))
        cum = jnp.dot(tri, log_f, preferred_element_type=F32,
                      precision=lax.Precision.HIGHEST) + carry
        col_ref[0, rows, :] = cum
        row_ref[0, :, rows] = cum.T[:B_HEADS, :]
        carry = cum[CUM_BLK - 1:CUM_BLK, :]


def _forget_cumsum(fb, b_forget_row):
    return pl.pallas_call(
        _cumsum_kernel,
        grid=(BATCH,),
        in_specs=[
            pl.BlockSpec((1, SEQ, LANES), lambda b: (b, 0, 0)),
            pl.BlockSpec((1, LANES), lambda b: (0, 0)),
        ],
        out_specs=[
            pl.BlockSpec((1, SEQ, LANES), lambda b: (b, 0, 0)),
            pl.BlockSpec((1, B_HEADS, SEQ), lambda b: (b, 0, 0)),
        ],
        out_shape=[
            jax.ShapeDtypeStruct((BATCH, SEQ, LANES), F32),
            jax.ShapeDtypeStruct((BATCH, B_HEADS, SEQ), F32),
        ],
        compiler_params=_params(1),
        name="forget_cumsum",
    )(fb.reshape(BATCH, SEQ, LANES), b_forget_row)


def _fox_kernel(q_ref, k_ref, v_ref, ccol_ref, crow_ref, o_ref):
    t = FOX_T
    pair = pl.program_id(1)
    qi = pl.program_id(2)
    first = _lane_half((1, LANES))
    q2 = q_ref[...]
    ccol = ccol_ref[0]
    head_lane = lax.broadcasted_iota(jnp.int32, (1, LANES), 1)
    qpos = lax.broadcasted_iota(jnp.int32, (t, t), 0)
    kpos = lax.broadcasted_iota(jnp.int32, (t, t), 1)
    causal = kpos <= qpos

    outs = []
    for e in range(2):
        head = 2 * pair + e
        qm = jnp.where(first if e == 0 else ~first, q2, jnp.zeros_like(q2))
        cq = jnp.sum(jnp.where(head_lane == head, ccol, 0.0), axis=1, keepdims=True)

        def step(j, carry, masked):
            m, l, acc = carry
            row0 = pl.multiple_of(j * t, t)
            kj = k_ref[pl.ds(row0, t), :]
            vj = v_ref[pl.ds(row0, t), :]
            s = lax.dot_general(qm, kj, (((1,), (1,)), ((), ())), preferred_element_type=F32)
            ck = crow_ref[0, head, pl.ds(j, 1), :]
            s = s + cq - ck
            if masked:
                s = jnp.where(causal, s, NEG_INF)
            m_new = jnp.maximum(m, jnp.max(s, axis=1, keepdims=True))
            alpha = jnp.exp(m - m_new)
            p = jnp.exp(s - m_new)
            l = alpha * l + jnp.sum(p, axis=1, keepdims=True)
            acc = alpha * acc + jnp.dot(p.astype(BF16), vj, preferred_element_type=F32)
            return m_new, l, acc

        init = (jnp.full((t, 1), NEG_INF, F32), jnp.zeros((t, 1), F32), jnp.zeros((t, LANES), F32))
        carry = lax.fori_loop(0, qi, functools.partial(step, masked=False), init)
        _, l, acc = step(qi, carry, masked=True)
        outs.append(acc / l)
    o_ref[...] = jnp.where(first, outs[0], outs[1]).astype(BF16)


def _fox_attention(proj, ccol, crow):
    t = FOX_T
    nq = SEQ // t
    return pl.pallas_call(
        _fox_kernel,
        grid=(BATCH, N_PAIRS, nq),
        in_specs=[
            pl.BlockSpec((t, LANES), lambda b, p, i: (b * nq + i, QB_BLK + p)),
            pl.BlockSpec((SEQ, LANES), lambda b, p, i: (b, KB_BLK + p)),
            pl.BlockSpec((SEQ, LANES), lambda b, p, i: (b, VB_BLK + p)),
            pl.BlockSpec((1, t, LANES), lambda b, p, i: (b, i, 0)),
            pl.BlockSpec((1, B_HEADS, nq, t), lambda b, p, i: (b, 0, 0, 0)),
        ],
        out_specs=pl.BlockSpec((t, LANES), lambda b, p, i: (b * nq + i, p)),
        out_shape=jax.ShapeDtypeStruct((TOKENS, BRANCH_WIDTH), BF16),
        compiler_params=_params(3),
        name="fox_attention",
    )(proj, proj, proj, ccol, crow.reshape(BATCH, B_HEADS, nq, t))


def _swa_kernel(sink_ref, q_ref, k_ref, v_ref, o_ref, kd_ref, vd_ref, *, layer):
    rows = A_GROUP * CHUNK
    first = _lane_half((1, LANES))
    grp = lax.broadcasted_iota(jnp.int32, (rows, 1), 0) // CHUNK
    qi = lax.broadcasted_iota(jnp.int32, (rows, A_BAND), 0) % CHUNK
    si = lax.broadcasted_iota(jnp.int32, (rows, A_BAND), 1)
    sel_r = lax.broadcasted_iota(jnp.int32, (LANES, LANES), 0)
    sel_c = lax.broadcasted_iota(jnp.int32, (LANES, LANES), 1)

    for kvh in range(A_KV_HEADS):
        sel = (sel_r == kvh * HEAD_DIM + sel_c % HEAD_DIM).astype(BF16)
        kd_ref[...] = jnp.dot(k_ref[...], sel, preferred_element_type=F32).astype(BF16)
        vd_ref[...] = jnp.dot(v_ref[...], sel, preferred_element_type=F32).astype(BF16)

        slope = jnp.zeros((rows, 1), F32)
        sink = jnp.zeros((rows, 1), F32)
        for g in range(A_GROUP):
            head = kvh * A_GROUP + g
            slope = jnp.where(grp == g, 2.0 ** -(head + 1), slope)
            sink = jnp.where(grp == g, sink_ref[layer, head], sink)

        def chunk(n, start, alibi, valid):
            q0 = n * CHUNK if isinstance(n, int) else pl.multiple_of(n * CHUNK, CHUNK)
            parts = []
            for g in range(A_GROUP):
                head = kvh * A_GROUP + g
                blk = q_ref[pl.ds(q0, CHUNK), (head // 2) * LANES:(head // 2 + 1) * LANES]
                parts.append(jnp.where(first if head % 2 == 0 else ~first, blk, jnp.zeros_like(blk)))
            qs = jnp.concatenate(parts, axis=0)
            kb = kd_ref[pl.ds(start, A_BAND), :]
            vb = vd_ref[pl.ds(start, A_BAND), :]
            s = lax.dot_general(qs, kb, (((1,), (1,)), ((), ())), preferred_element_type=F32) + alibi
            if valid is not None:
                s = jnp.where(valid, s, NEG_INF)
            m = jnp.maximum(jnp.max(s, axis=1, keepdims=True), sink)
            p = jnp.exp(s - m)
            denom = jnp.sum(p, axis=1, keepdims=True) + jnp.exp(sink - m)
            r = jnp.dot(p.astype(BF16), vb, preferred_element_type=F32) / denom
            for pr in range(A_GROUP // 2):
                even = r[(2 * pr) * CHUNK:(2 * pr + 1) * CHUNK]
                odd = r[(2 * pr + 1) * CHUNK:(2 * pr + 2) * CHUNK]
                col = (kvh * (A_GROUP // 2) + pr) * LANES
                o_ref[pl.ds(q0, CHUNK), col:col + LANES] = jnp.where(first, even, odd).astype(BF16)

        for n in range(A_PREV):
            dist = n * CHUNK + qi - si
            chunk(n, 0, -slope * jnp.abs(dist).astype(F32), si // CHUNK <= n)

        alibi = -slope * jnp.abs(A_PREV * CHUNK + qi - si).astype(F32)

        def body(n, carry):
            chunk(n, pl.multiple_of((n - A_PREV) * CHUNK, CHUNK), alibi, None)
            return carry

        lax.fori_loop(A_PREV, N_CHUNKS, body, 0)


def _swa_attention(proj, sinks, layer):
    return pl.pallas_call(
        functools.partial(_swa_kernel, layer=layer),
        grid=(BATCH,),
        in_specs=[
            pl.BlockSpec(memory_space=pltpu.SMEM),
            pl.BlockSpec((SEQ, A_HEADS * HEAD_DIM), lambda b: (b, 0)),
            pl.BlockSpec((SEQ, LANES), lambda b: (b, KA_BLK)),
            pl.BlockSpec((SEQ, LANES), lambda b: (b, VA_BLK)),
        ],
        out_specs=pl.BlockSpec((SEQ, BRANCH_WIDTH), lambda b: (b, 0)),
        out_shape=jax.ShapeDtypeStruct((TOKENS, BRANCH_WIDTH), BF16),
        scratch_shapes=[pltpu.VMEM((SEQ, LANES), BF16), pltpu.VMEM((SEQ, LANES), BF16)],
        compiler_params=_params(1),
        name="swa_attention",
    )(sinks, proj, proj, proj)


def _chunked_kernel(q_ref, k_ref, v_ref, bias_ref, o_ref, kp_ref, vp_ref):
    first = _lane_half((1, LANES))
    kp_ref[0:C_PAD, :] = jnp.zeros((C_PAD, LANES), BF16)
    vp_ref[0:C_PAD, :] = jnp.zeros((C_PAD, LANES), BF16)
    kp_ref[C_PAD:, :] = k_ref[...]
    vp_ref[C_PAD:, :] = v_ref[...]
    si = lax.broadcasted_iota(jnp.int32, (2 * CHUNK, C_BAND), 1)

    def chunk(n, masked):
        q0 = pl.multiple_of(n * CHUNK, CHUNK)
        q2 = q_ref[pl.ds(q0, CHUNK), :]
        zero = jnp.zeros_like(q2)
        qs = jnp.concatenate([jnp.where(first, q2, zero), jnp.where(first, zero, q2)], axis=0)
        kb = kp_ref[pl.ds(q0, C_BAND), :]
        vb = vp_ref[pl.ds(q0, C_BAND), :]
        s = lax.dot_general(qs, kb, (((1,), (1,)), ((), ())), preferred_element_type=F32) + bias_ref[0]
        if masked:
            s = jnp.where(n * CHUNK + si >= C_PAD, s, NEG_INF)
        m = jnp.max(s, axis=1, keepdims=True)
        p = jnp.exp(s - m)
        denom = jnp.sum(p, axis=1, keepdims=True)
        r = jnp.dot(p.astype(BF16), vb, preferred_element_type=F32) / denom
        o_ref[pl.ds(q0, CHUNK), :] = jnp.where(first, r[:CHUNK], r[CHUNK:]).astype(BF16)

    def masked_body(n, carry):
        chunk(n, True)
        return carry

    def plain_body(n, carry):
        chunk(n, False)
        return carry

    lax.fori_loop(0, C_PREV, masked_body, 0)
    lax.fori_loop(C_PREV, N_CHUNKS, plain_body, 0)


def _chunked_attention(proj, bias, layer):
    return pl.pallas_call(
        _chunked_kernel,
        grid=(BATCH, N_PAIRS),
        in_specs=[
            pl.BlockSpec((SEQ, LANES), lambda b, p: (b, QC_BLK + p)),
            pl.BlockSpec((SEQ, LANES), lambda b, p: (b, KC_BLK + p)),
            pl.BlockSpec((SEQ, LANES), lambda b, p: (b, VC_BLK + p)),
            pl.BlockSpec((None, 1, 2 * CHUNK, C_BAND), lambda b, p: (layer, p, 0, 0)),
        ],
        out_specs=pl.BlockSpec((SEQ, LANES), lambda b, p: (b, p)),
        out_shape=jax.ShapeDtypeStruct((TOKENS, BRANCH_WIDTH), BF16),
        scratch_shapes=[pltpu.VMEM((C_PAD + SEQ, LANES), BF16), pltpu.VMEM((C_PAD + SEQ, LANES), BF16)],
        compiler_params=_params(2),
        name="chunked_attention",
    )(proj, proj, proj, bias)


def _merge_kernel(x_ref, mod_ref, g_ref, oa_ref, ob_ref, oc_ref, wb_ref, wg_ref, wo_ref, out_ref,
                  merged_ref, *, tn):
    h = _rms_mod(x_ref[...], g_ref[...], mod_ref[0:1, :], mod_ref[1:2, :]).astype(BF16)
    branches = (oa_ref[...], ob_ref[...], oc_ref[...])
    for n in range(D_MODEL // tn):
        acc = None
        for k, o in enumerate(branches):
            y = jnp.dot(o, wb_ref[k, :, n * tn:(n + 1) * tn], preferred_element_type=F32)
            gate = jnp.dot(h, wg_ref[:, k * D_MODEL + n * tn:k * D_MODEL + (n + 1) * tn],
                           preferred_element_type=F32)
            term = jax.nn.sigmoid(gate) * y
            acc = term if acc is None else acc + term
        merged_ref[:, n * tn:(n + 1) * tn] = acc.astype(BF16)
    merged = merged_ref[...]
    for n in range(D_MODEL // tn):
        sl = slice(n * tn, (n + 1) * tn)
        out = jnp.dot(merged, wo_ref[:, sl], preferred_element_type=F32)
        out_ref[:, sl] = x_ref[:, sl] + mod_ref[2:3, sl] * out


def _merge(x, mod, gain, o_a, o_b, o_c, w_branch, w_gate, w_out, layer):
    steps_per_batch = SEQ // TM
    row = lambda i: (i, 0)
    return pl.pallas_call(
        functools.partial(_merge_kernel, tn=256),
        grid=(TOKENS // TM,),
        in_specs=[
            pl.BlockSpec((TM, D_MODEL), row),
            pl.BlockSpec((None, None, 6, D_MODEL), lambda i: (layer, i // steps_per_batch, 0, 0)),
            pl.BlockSpec((None, 1, D_MODEL), lambda i: (layer, 0, 0)),
            pl.BlockSpec((TM, BRANCH_WIDTH), row),
            pl.BlockSpec((TM, BRANCH_WIDTH), row),
            pl.BlockSpec((TM, BRANCH_WIDTH), row),
            pl.BlockSpec((None, 3, BRANCH_WIDTH, D_MODEL), lambda i: (layer, 0, 0, 0)),
            pl.BlockSpec((None, D_MODEL, 3 * D_MODEL), lambda i: (layer, 0, 0)),
            pl.BlockSpec((None, D_MODEL, D_MODEL), lambda i: (layer, 0, 0)),
        ],
        out_specs=pl.BlockSpec((TM, D_MODEL), row),
        out_shape=jax.ShapeDtypeStruct((TOKENS, D_MODEL), F32),
        scratch_shapes=[pltpu.VMEM((TM, D_MODEL), BF16)],
        compiler_params=_params(1),
        name="merge_out",
    )(x, mod, gain, o_a, o_b, o_c, w_branch, w_gate, w_out)


def _ffn_kernel(x_ref, mod_ref, g_ref, gf_ref, wi_ref, wo_ref, out_ref, act_ref, *, tf, tn, final):
    h = _rms_mod(x_ref[...], g_ref[...], mod_ref[3:4, :], mod_ref[4:5, :]).astype(BF16)
    for c in range(FFN_HIDDEN // tf):
        gate = jnp.dot(h, wi_ref[:, c * tf:(c + 1) * tf], preferred_element_type=F32)
        up = jnp.dot(h, wi_ref[:, FFN_HIDDEN + c * tf:FFN_HIDDEN + (c + 1) * tf],
                     preferred_element_type=F32)
        act_ref[:, c * tf:(c + 1) * tf] = (gate * jax.nn.sigmoid(gate) * up).astype(BF16)
    act = act_ref[...]
    for n in range(D_MODEL // tn):
        sl = slice(n * tn, (n + 1) * tn)
        out = jnp.dot(act, wo_ref[:, sl], preferred_element_type=F32)
        out_ref[:, sl] = x_ref[:, sl] + mod_ref[5:6, sl] * out
    if final:
        y = out_ref[...]
        ms = jnp.mean(y * y, axis=-1, keepdims=True)
        out_ref[...] = y * lax.rsqrt(ms + EPS) * gf_ref[...]


def _ffn(x, mod, gain, final_gain, w_ffn_in, w_ffn_out, layer, final):
    steps_per_batch = SEQ // TM
    row = lambda i: (i, 0)
    return pl.pallas_call(
        functools.partial(_ffn_kernel, tf=256, tn=256, final=final),
        grid=(TOKENS // TM,),
        in_specs=[
            pl.BlockSpec((TM, D_MODEL), row),
            pl.BlockSpec((None, None, 6, D_MODEL), lambda i: (layer, i // steps_per_batch, 0, 0)),
            pl.BlockSpec((None, 1, D_MODEL), lambda i: (layer, 0, 0)),
            pl.BlockSpec((1, D_MODEL), lambda i: (0, 0)),
            pl.BlockSpec((None, D_MODEL, 2 * FFN_HIDDEN), lambda i: (layer, 0, 0)),
            pl.BlockSpec((None, FFN_HIDDEN, D_MODEL), lambda i: (layer, 0, 0)),
        ],
        out_specs=pl.BlockSpec((TM, D_MODEL), row),
        out_shape=jax.ShapeDtypeStruct((TOKENS, D_MODEL), F32),
        scratch_shapes=[pltpu.VMEM((TM, FFN_HIDDEN), BF16)],
        compiler_params=_params(1),
        name="ffn",
    )(x, mod, gain, final_gain, w_ffn_in, w_ffn_out)


def kernel(x, c, norm_mix_g, norm_ffn_g, w_ada, b_ada, w_in, b_forget, sinks, rel_bias,
           w_branch, w_out, w_ffn_in, w_ffn_out, final_norm_g):
    scale = HEAD_DIM ** -0.5
    q_cols = jnp.zeros((PROJ_COLS,), bool)
    for blk in (QA_BLK, QB_BLK, QC_BLK):
        q_cols = q_cols.at[blk * LANES:(blk + 4) * LANES].set(True)
    fb0 = 2304
    w_qkv = jnp.concatenate([w_in[:, :, :fb0], w_in[:, :, fb0 + B_HEADS:fb0 + B_HEADS + 1536]], axis=-1)
    w_qkv = (w_qkv * jnp.where(q_cols, scale, 1.0)).astype(BF16)
    w_fb = jnp.pad(w_in[:, :, fb0:fb0 + B_HEADS], ((0, 0), (0, 0), (0, LANES - B_HEADS))).astype(BF16)
    w_gate = w_in[:, :, fb0 + B_HEADS + 1536:].astype(BF16)
    w_branch_b = w_branch.astype(BF16)
    w_out_b = w_out.astype(BF16)
    w_ffn_in_b = w_ffn_in.astype(BF16)
    w_ffn_out_b = w_ffn_out.astype(BF16)
    b_forget_rows = jnp.pad(b_forget, ((0, 0), (0, LANES - B_HEADS))).reshape(DEPTH, 1, LANES)
    gain_mix = norm_mix_g.reshape(DEPTH, 1, D_MODEL)
    gain_ffn = norm_ffn_g.reshape(DEPTH, 1, D_MODEL)
    gain_final = final_norm_g.reshape(1, D_MODEL)

    mod = _ada_mod(c, w_ada, b_ada)
    bias_c = _rel_bias(rel_bias)

    xt = x.reshape(TOKENS, D_MODEL)
    for layer in range(DEPTH):
        proj, fb = _in_proj(xt, mod, gain_mix, w_qkv, w_fb, layer)
        ccol, crow = _forget_cumsum(fb, b_forget_rows[layer])
        o_a = _swa_attention(proj, sinks, layer)
        o_b = _fox_attention(proj, ccol, crow)
        o_c = _chunked_attention(proj, bias_c, layer)
        xt = _merge(xt, mod, gain_mix, o_a, o_b, o_c, w_branch_b, w_gate, w_out_b, layer)
        xt = _ffn(xt, mod, gain_ffn, gain_final, w_ffn_in_b, w_ffn_out_b, layer, layer == DEPTH - 1)
    return xt.reshape(BATCH, SEQ, D_MODEL)
```

```python
import functools

import jax
import jax.numpy as jnp
from jax import lax
from jax.experimental import pallas as pl
from jax.experimental.pallas import tpu as pltpu

F32 = jnp.float32
BF16 = jnp.bfloat16

D_MODEL = 1024
BATCH = 8
SEQ = 2048
TOKENS = BATCH * SEQ
DEPTH = 2
CHUNK = 64
HEAD_DIM = 64
EPS = 1e-6
NEG_INF = -1e30

A_HEADS = 8
A_KV_HEADS = 2
A_GROUP = A_HEADS // A_KV_HEADS
A_PREV = 2
A_BAND = (A_PREV + 1) * CHUNK
B_HEADS = 8
C_HEADS = 8
C_PREV = 8
C_PAD = C_PREV * CHUNK
C_BAND = (C_PREV + 1) * CHUNK
REL_CLIP = 128
N_REL = 2 * REL_CLIP + 1
BRANCH_WIDTH = 512
FFN_HIDDEN = 2816
N_CHUNKS = SEQ // CHUNK

LANES = 128
N_PAIRS = 4

PROJ_COLS = 3840
QB_BLK, KB_BLK, VB_BLK = 0, 4, 8
QC_BLK, KC_BLK, VC_BLK = 12, 16, 20
QA_BLK, KA_BLK, VA_BLK = 24, 28, 29

TM = 512
FOX_T = 256
FOX_PAIRS = 2
CUM_BLK = 256

VMEM_LIMIT = 56 * 1024 * 1024


def _params(n_axes):
    return pltpu.CompilerParams(dimension_semantics=("arbitrary",) * n_axes,
                                vmem_limit_bytes=VMEM_LIMIT)


def _rms_mod(x, g, shift, scale):
    ms = jnp.mean(x * x, axis=-1, keepdims=True)
    y = x * lax.rsqrt(ms + EPS) * g
    return y * (1.0 + scale) + shift


def _row_start(index, size):
    if isinstance(index, int):
        return index * size
    return pl.multiple_of(index * size, size)


def _lane_half(shape):
    return lax.broadcasted_iota(jnp.int32, shape, len(shape) - 1) < HEAD_DIM


def _ada_kernel(c_ref, w_ref, b_ref, o_ref):
    c = c_ref[...]
    cond = c * jax.nn.sigmoid(c)
    o_ref[...] = jnp.dot(cond.astype(BF16), w_ref[...].astype(BF16),
                         preferred_element_type=F32) + b_ref[...]


def _ada_mod(c, w_ada, b_ada):
    n_blk = 6
    out = pl.pallas_call(
        _ada_kernel,
        grid=(DEPTH, n_blk),
        in_specs=[
            pl.BlockSpec((BATCH, D_MODEL), lambda l, j: (0, 0)),
            pl.BlockSpec((None, D_MODEL, D_MODEL), lambda l, j: (l, 0, j)),
            pl.BlockSpec((None, 1, D_MODEL), lambda l, j: (l, 0, j)),
        ],
        out_specs=pl.BlockSpec((None, BATCH, D_MODEL), lambda l, j: (l, 0, j)),
        out_shape=jax.ShapeDtypeStruct((DEPTH, BATCH, n_blk * D_MODEL), F32),
        compiler_params=_params(2),
        name="ada_mod",
    )(c, w_ada, b_ada.reshape(DEPTH, 1, n_blk * D_MODEL))
    return out.reshape(DEPTH, BATCH, n_blk, D_MODEL)


def _relbias_kernel(tab_ref, o_ref):
    l = pl.program_id(0)
    h = pl.program_id(1)
    qi = lax.broadcasted_iota(jnp.int32, (CHUNK, C_BAND), 0)
    si = lax.broadcasted_iota(jnp.int32, (CHUNK, C_BAND), 1)
    idx = jnp.clip(C_PAD + qi - si, -REL_CLIP, REL_CLIP) + REL_CLIP

    def body(r, acc):
        return jnp.where(idx == r, tab_ref[l, h, r], acc)

    o_ref[...] = lax.fori_loop(0, N_REL, body, jnp.zeros((CHUNK, C_BAND), F32))


def _rel_bias(rel_bias):
    out = pl.pallas_call(
        _relbias_kernel,
        grid=(DEPTH, C_HEADS),
        in_specs=[pl.BlockSpec(memory_space=pltpu.SMEM)],
        out_specs=pl.BlockSpec((None, None, CHUNK, C_BAND), lambda l, h: (l, h, 0, 0)),
        out_shape=jax.ShapeDtypeStruct((DEPTH, C_HEADS, CHUNK, C_BAND), F32),
        compiler_params=_params(2),
        name="rel_bias",
    )(rel_bias)
    return out.reshape(DEPTH, N_PAIRS, 2 * CHUNK, C_BAND)


def _inproj_kernel(x_ref, mod_ref, g_ref, w_ref, wfb_ref, proj_ref, fb_ref, *, tn):
    h = _rms_mod(x_ref[...], g_ref[...], mod_ref[0:1, :], mod_ref[1:2, :]).astype(BF16)
    for j in range(PROJ_COLS // tn):
        sl = slice(j * tn, (j + 1) * tn)
        proj_ref[:, sl] = jnp.dot(h, w_ref[:, sl], preferred_element_type=F32).astype(BF16)
    fb_ref[...] = jnp.dot(h, wfb_ref[...], preferred_element_type=F32)


def _in_proj(x, mod, gain, w_qkv, w_fb, layer):
    tn = 768
    steps_per_batch = SEQ // TM
    return pl.pallas_call(
        functools.partial(_inproj_kernel, tn=tn),
        grid=(TOKENS // TM,),
        in_specs=[
            pl.BlockSpec((TM, D_MODEL), lambda i: (i, 0)),
            pl.BlockSpec((None, None, 6, D_MODEL), lambda i: (layer, i // steps_per_batch, 0, 0)),
            pl.BlockSpec((None, 1, D_MODEL), lambda i: (layer, 0, 0)),
            pl.BlockSpec((None, D_MODEL, PROJ_COLS), lambda i: (layer, 0, 0)),
            pl.BlockSpec((None, D_MODEL, LANES), lambda i: (layer, 0, 0)),
        ],
        out_specs=[
            pl.BlockSpec((TM, PROJ_COLS), lambda i: (i, 0)),
            pl.BlockSpec((TM, LANES), lambda i: (i, 0)),
        ],
        out_shape=[
            jax.ShapeDtypeStruct((TOKENS, PROJ_COLS), BF16),
            jax.ShapeDtypeStruct((TOKENS, LANES), F32),
        ],
        compiler_params=_params(1),
        name="in_proj",
    )(x, mod, gain, w_qkv, w_fb)


def _cumsum_kernel(fb_ref, bias_ref, col_ref, row_ref):
    r = lax.broadcasted_iota(jnp.int32, (CUM_BLK, CUM_BLK), 0)
    c = lax.broadcasted_iota(jnp.int32, (CUM_BLK, CUM_BLK), 1)
    tri = (r >= c).astype(F32)
    carry = jnp.zeros((1, LANES), F32)
    for blk in range(SEQ // CUM_BLK):
        rows = slice(blk * CUM_BLK, (blk + 1) * CUM_BLK)
        z = fb_ref[0, rows, :] + bias_ref[...]
        log_f = jnp.minimum(z, 0.0) - jnp.log1p(jnp.exp(-jnp.abs(z)))
        cum = jnp.dot(tri, log_f, preferred_element_type=F32,
                      precision=lax.Precision.HIGHEST) + carry
        col_ref[0, rows, :] = cum
        row_ref[0, :, rows] = cum.T[:B_HEADS, :]
        carry = cum[CUM_BLK - 1:CUM_BLK, :]


def _forget_cumsum(fb, b_forget_row):
    return pl.pallas_call(
        _cumsum_kernel,
        grid=(BATCH,),
        in_specs=[
            pl.BlockSpec((1, SEQ, LANES), lambda b: (b, 0, 0)),
            pl.BlockSpec((1, LANES), lambda b: (0, 0)),
        ],
        out_specs=[
            pl.BlockSpec((1, SEQ, LANES), lambda b: (b, 0, 0)),
            pl.BlockSpec((1, B_HEADS, SEQ), lambda b: (b, 0, 0)),
        ],
        out_shape=[
            jax.ShapeDtypeStruct((BATCH, SEQ, LANES), F32),
            jax.ShapeDtypeStruct((BATCH, B_HEADS, SEQ), F32),
        ],
        compiler_params=_params(1),
        name="forget_cumsum",
    )(fb.reshape(BATCH, SEQ, LANES), b_forget_row)


def _fox_kernel(q_ref, k_ref, v_ref, ccol_ref, crow_ref, o_ref, vt_ref, ckb_ref):
    t = FOX_T
    heads = 2 * FOX_PAIRS
    grp = pl.program_id(1)
    qi = pl.program_id(2)
    first = _lane_half((1, LANES))

    @pl.when(qi == 0)
    def _():
        head_lane = lax.broadcasted_iota(jnp.int32, (1, LANES), 1)
        for jt in range(SEQ // t):
            rows = slice(jt * t, (jt + 1) * t)
            vt_ref[jt] = v_ref[rows, :].astype(F32).T.astype(BF16)
            ccol = ccol_ref[0, rows, :]
            for h in range(heads):
                col = jnp.sum(jnp.where(head_lane == grp * heads + h, ccol, 0.0), axis=1, keepdims=True)
                ckb_ref[h, rows, :] = jnp.broadcast_to(col, (t, LANES))

    qms, cqs = [], []
    for h in range(heads):
        q2 = q_ref[:, (h // 2) * LANES:(h // 2 + 1) * LANES]
        qms.append(jnp.where(first if h % 2 == 0 else ~first, q2, jnp.zeros_like(q2)))
        cqs.append(crow_ref[0, grp * heads + h, pl.ds(qi, 1), :])
    kpos = lax.broadcasted_iota(jnp.int32, (t, t), 0)
    qpos = lax.broadcasted_iota(jnp.int32, (t, t), 1)
    causal = kpos <= qpos

    def scores(j):
        row0 = _row_start(j, t)
        out = []
        for h in range(heads):
            kj = k_ref[pl.ds(row0, t), (h // 2) * LANES:(h // 2 + 1) * LANES]
            s = lax.dot_general(kj, qms[h], (((1,), (1,)), ((), ())), preferred_element_type=F32)
            ck = ckb_ref[h, pl.ds(row0, t), :]
            out.append(s + cqs[h] - jnp.concatenate([ck] * (t // LANES), axis=1))
        return tuple(out)

    def update(j, stats, s_all, masked):
        probs, new = [], []
        for h in range(heads):
            m, l, _ = stats[h]
            s = jnp.where(causal, s_all[h], NEG_INF) if masked else s_all[h]
            m_new = jnp.maximum(m, jnp.max(s, axis=0, keepdims=True))
            alpha = jnp.exp(m - m_new)
            p = jnp.exp(s - m_new)
            probs.append(p.astype(BF16))
            new.append((m_new, alpha * l + jnp.sum(p, axis=0, keepdims=True), alpha))
        out = []
        for h in range(heads):
            m_new, l_new, alpha = new[h]
            vt = vt_ref[j, h * HEAD_DIM:(h + 1) * HEAD_DIM, :]
            acc = alpha * stats[h][2] + jnp.dot(vt, probs[h], preferred_element_type=F32)
            out.append((m_new, l_new, acc))
        return tuple(out)

    def body(j, carry):
        stats, s_cur = carry
        s_next = scores(j + 1)
        return update(j, stats, s_cur, masked=False), s_next

    init = tuple((jnp.full((1, t), NEG_INF, F32), jnp.zeros((1, t), F32), jnp.zeros((HEAD_DIM, t), F32))
                 for _ in range(heads))
    stats, s_last = lax.fori_loop(0, qi, body, (init, scores(0)))
    final = update(qi, stats, s_last, masked=True)
    out_t = jnp.concatenate([acc / l for _, l, acc in final], axis=0)
    o_ref[...] = out_t.T.astype(BF16)


def _fox_attention(proj, ccol, crow):
    t = FOX_T
    nq = SEQ // t
    width = FOX_PAIRS * LANES
    return pl.pallas_call(
        _fox_kernel,
        grid=(BATCH, N_PAIRS // FOX_PAIRS, nq),
        in_specs=[
            pl.BlockSpec((t, width), lambda b, g, i: (b * nq + i, QB_BLK // FOX_PAIRS + g)),
            pl.BlockSpec((SEQ, width), lambda b, g, i: (b, KB_BLK // FOX_PAIRS + g)),
            pl.BlockSpec((SEQ, width), lambda b, g, i: (b, VB_BLK // FOX_PAIRS + g)),
            pl.BlockSpec((1, SEQ, LANES), lambda b, g, i: (b, 0, 0)),
            pl.BlockSpec((1, B_HEADS, nq, t), lambda b, g, i: (b, 0, 0, 0)),
        ],
        out_specs=pl.BlockSpec((t, width), lambda b, g, i: (b * nq + i, g)),
        out_shape=jax.ShapeDtypeStruct((TOKENS, BRANCH_WIDTH), BF16),
        scratch_shapes=[pltpu.VMEM((nq, width, t), BF16),
                        pltpu.VMEM((2 * FOX_PAIRS, SEQ, LANES), F32)],
        compiler_params=_params(3),
        name="fox_attention",
    )(proj, proj, proj, ccol, crow.reshape(BATCH, B_HEADS, nq, t))


def _swa_kernel(sink_ref, q_ref, k_ref, v_ref, o_ref, kd_ref, vd_ref, *, layer):
    rows = A_GROUP * CHUNK
    first = _lane_half((1, LANES))
    grp = lax.broadcasted_iota(jnp.int32, (rows, 1), 0) // CHUNK
    qi = lax.broadcasted_iota(jnp.int32, (rows, A_BAND), 0) % CHUNK
    si = lax.broadcasted_iota(jnp.int32, (rows, A_BAND), 1)
    sel_r = lax.broadcasted_iota(jnp.int32, (LANES, LANES), 0)
    sel_c = lax.broadcasted_iota(jnp.int32, (LANES, LANES), 1)

    slopes, sinks = [], []
    for kvh in range(A_KV_HEADS):
        sel = (sel_r == kvh * HEAD_DIM + sel_c % HEAD_DIM).astype(BF16)
        kd_ref[kvh] = jnp.dot(k_ref[...], sel, preferred_element_type=F32).astype(BF16)
        vd_ref[kvh] = jnp.dot(v_ref[...], sel, preferred_element_type=F32).astype(BF16)
        slope = jnp.zeros((rows, 1), F32)
        sink = jnp.zeros((rows, 1), F32)
        for g in range(A_GROUP):
            head = kvh * A_GROUP + g
            slope = jnp.where(grp == g, 2.0 ** -(head + 1), slope)
            sink = jnp.where(grp == g, sink_ref[layer, head], sink)
        slopes.append(slope)
        sinks.append(sink)

    def band_start(n):
        return 0 if isinstance(n, int) and n < A_PREV else _row_start(n - A_PREV, CHUNK)

    def scores(n, alibi):
        q0 = _row_start(n, CHUNK)
        out = []
        for kvh in range(A_KV_HEADS):
            parts = []
            for g in range(A_GROUP):
                head = kvh * A_GROUP + g
                blk = q_ref[pl.ds(q0, CHUNK), (head // 2) * LANES:(head // 2 + 1) * LANES]
                parts.append(jnp.where(first if head % 2 == 0 else ~first, blk, jnp.zeros_like(blk)))
            qs = jnp.concatenate(parts, axis=0)
            kb = kd_ref[kvh, pl.ds(band_start(n), A_BAND), :]
            s = lax.dot_general(qs, kb, (((1,), (1,)), ((), ())), preferred_element_type=F32)
            out.append(s + alibi[kvh])
        return tuple(out)

    def finish(n, s_all, valid):
        q0 = _row_start(n, CHUNK)
        probs, denoms = [], []
        for kvh in range(A_KV_HEADS):
            s = s_all[kvh] if valid is None else jnp.where(valid, s_all[kvh], NEG_INF)
            m = jnp.maximum(jnp.max(s, axis=1, keepdims=True), sinks[kvh])
            p = jnp.exp(s - m)
            denoms.append(jnp.sum(p, axis=1, keepdims=True) + jnp.exp(sinks[kvh] - m))
            probs.append(p.astype(BF16))
        for kvh in range(A_KV_HEADS):
            vb = vd_ref[kvh, pl.ds(band_start(n), A_BAND), :]
            r = jnp.dot(probs[kvh], vb, preferred_element_type=F32) / denoms[kvh]
            for pr in range(A_GROUP // 2):
                even = r[(2 * pr) * CHUNK:(2 * pr + 1) * CHUNK]
                odd = r[(2 * pr + 1) * CHUNK:(2 * pr + 2) * CHUNK]
                col = (kvh * (A_GROUP // 2) + pr) * LANES
                o_ref[pl.ds(q0, CHUNK), col:col + LANES] = jnp.where(first, even, odd).astype(BF16)

    def alibi_for(dist):
        return tuple(-slope * jnp.abs(dist).astype(F32) for slope in slopes)

    alibi = alibi_for(A_PREV * CHUNK + qi - si)
    s = scores(0, alibi_for(qi - si))
    for n in range(A_PREV):
        nxt = n + 1
        s_next = scores(nxt, alibi if nxt >= A_PREV else alibi_for(nxt * CHUNK + qi - si))
        finish(n, s, si // CHUNK <= n)
        s = s_next

    def body(n, s_cur):
        s_next = scores(n + 1, alibi)
        finish(n, s_cur, None)
        return s_next

    s = lax.fori_loop(A_PREV, N_CHUNKS - 1, body, s)
    finish(N_CHUNKS - 1, s, None)


def _swa_attention(proj, sinks, layer):
    return pl.pallas_call(
        functools.partial(_swa_kernel, layer=layer),
        grid=(BATCH,),
        in_specs=[
            pl.BlockSpec(memory_space=pltpu.SMEM),
            pl.BlockSpec((SEQ, A_HEADS * HEAD_DIM), lambda b: (b, QA_BLK // N_PAIRS)),
            pl.BlockSpec((SEQ, LANES), lambda b: (b, KA_BLK)),
            pl.BlockSpec((SEQ, LANES), lambda b: (b, VA_BLK)),
        ],
        out_specs=pl.BlockSpec((SEQ, BRANCH_WIDTH), lambda b: (b, 0)),
        out_shape=jax.ShapeDtypeStruct((TOKENS, BRANCH_WIDTH), BF16),
        scratch_shapes=[pltpu.VMEM((A_KV_HEADS, SEQ, LANES), BF16),
                        pltpu.VMEM((A_KV_HEADS, SEQ, LANES), BF16)],
        compiler_params=_params(1),
        name="swa_attention",
    )(sinks, proj, proj, proj)


def _chunked_kernel(q_ref, k_ref, v_ref, bias_ref, o_ref, kp_ref, vp_ref):
    first = _lane_half((1, LANES))
    width = N_PAIRS * LANES
    kp_ref[0:C_PAD, :] = jnp.zeros((C_PAD, width), BF16)
    vp_ref[0:C_PAD, :] = jnp.zeros((C_PAD, width), BF16)
    kp_ref[C_PAD:, :] = k_ref[...]
    vp_ref[C_PAD:, :] = v_ref[...]
    si = lax.broadcasted_iota(jnp.int32, (2 * CHUNK, C_BAND), 1)

    def scores(n):
        q0 = _row_start(n, CHUNK)
        out = []
        for pair in range(N_PAIRS):
            cols = slice(pair * LANES, (pair + 1) * LANES)
            q2 = q_ref[pl.ds(q0, CHUNK), cols]
            zero = jnp.zeros_like(q2)
            qs = jnp.concatenate([jnp.where(first, q2, zero), jnp.where(first, zero, q2)], axis=0)
            kb = kp_ref[pl.ds(q0, C_BAND), cols]
            s = lax.dot_general(qs, kb, (((1,), (1,)), ((), ())), preferred_element_type=F32)
            out.append(s + bias_ref[pair])
        return tuple(out)

    def finish(n, s_all, masked):
        q0 = _row_start(n, CHUNK)
        probs, denoms = [], []
        for pair in range(N_PAIRS):
            s = jnp.where(n * CHUNK + si >= C_PAD, s_all[pair], NEG_INF) if masked else s_all[pair]
            p = jnp.exp(s - jnp.max(s, axis=1, keepdims=True))
            denoms.append(jnp.sum(p, axis=1, keepdims=True))
            probs.append(p.astype(BF16))
        for pair in range(N_PAIRS):
            cols = slice(pair * LANES, (pair + 1) * LANES)
            vb = vp_ref[pl.ds(q0, C_BAND), cols]
            r = jnp.dot(probs[pair], vb, preferred_element_type=F32) / denoms[pair]
            o_ref[pl.ds(q0, CHUNK), cols] = jnp.where(first, r[:CHUNK], r[CHUNK:]).astype(BF16)

    def body(n, s_cur, masked):
        s_next = scores(n + 1)
        finish(n, s_cur, masked)
        return s_next

    s = lax.fori_loop(0, C_PREV, functools.partial(body, masked=True), scores(0))
    s = lax.fori_loop(C_PREV, N_CHUNKS - 1, functools.partial(body, masked=False), s)
    finish(N_CHUNKS - 1, s, False)


def _chunked_attention(proj, bias, layer):
    width = N_PAIRS * LANES
    return pl.pallas_call(
        _chunked_kernel,
        grid=(BATCH,),
        in_specs=[
            pl.BlockSpec((SEQ, width), lambda b: (b, QC_BLK // N_PAIRS)),
            pl.BlockSpec((SEQ, width), lambda b: (b, KC_BLK // N_PAIRS)),
            pl.BlockSpec((SEQ, width), lambda b: (b, VC_BLK // N_PAIRS)),
            pl.BlockSpec((None, N_PAIRS, 2 * CHUNK, C_BAND), lambda b: (layer, 0, 0, 0)),
        ],
        out_specs=pl.BlockSpec((SEQ, width), lambda b: (b, 0)),
        out_shape=jax.ShapeDtypeStruct((TOKENS, BRANCH_WIDTH), BF16),
        scratch_shapes=[pltpu.VMEM((C_PAD + SEQ, width), BF16), pltpu.VMEM((C_PAD + SEQ, width), BF16)],
        compiler_params=_params(1),
        name="chunked_attention",
    )(proj, proj, proj, bias)


def _merge_kernel(x_ref, mod_ref, g_ref, oa_ref, ob_ref, oc_ref, wb_ref, wg_ref, wo_ref, out_ref,
                  merged_ref, *, tn):
    h = _rms_mod(x_ref[...], g_ref[...], mod_ref[0:1, :], mod_ref[1:2, :]).astype(BF16)
    branches = (oa_ref[...], ob_ref[...], oc_ref[...])
    for n in range(D_MODEL // tn):
        acc = None
        for k, o in enumerate(branches):
            y = jnp.dot(o, wb_ref[k, :, n * tn:(n + 1) * tn], preferred_element_type=F32)
            gate = jnp.dot(h, wg_ref[:, k * D_MODEL + n * tn:k * D_MODEL + (n + 1) * tn],
                           preferred_element_type=F32)
            term = jax.nn.sigmoid(gate) * y
            acc = term if acc is None else acc + term
        merged_ref[:, n * tn:(n + 1) * tn] = acc.astype(BF16)
    merged = merged_ref[...]
    for n in range(D_MODEL // tn):
        sl = slice(n * tn, (n + 1) * tn)
        out = jnp.dot(merged, wo_ref[:, sl], preferred_element_type=F32)
        out_ref[:, sl] = x_ref[:, sl] + mod_ref[2:3, sl] * out


def _merge(x, mod, gain, o_a, o_b, o_c, w_branch, w_gate, w_out, layer):
    steps_per_batch = SEQ // TM
    row = lambda i: (i, 0)
    return pl.pallas_call(
        functools.partial(_merge_kernel, tn=256),
        grid=(TOKENS // TM,),
        in_specs=[
            pl.BlockSpec((TM, D_MODEL), row),
            pl.BlockSpec((None, None, 6, D_MODEL), lambda i: (layer, i // steps_per_batch, 0, 0)),
            pl.BlockSpec((None, 1, D_MODEL), lambda i: (layer, 0, 0)),
            pl.BlockSpec((TM, BRANCH_WIDTH), row),
            pl.BlockSpec((TM, BRANCH_WIDTH), row),
            pl.BlockSpec((TM, BRANCH_WIDTH), row),
            pl.BlockSpec((None, 3, BRANCH_WIDTH, D_MODEL), lambda i: (layer, 0, 0, 0)),
            pl.BlockSpec((None, D_MODEL, 3 * D_MODEL), lambda i: (layer, 0, 0)),
            pl.BlockSpec((None, D_MODEL, D_MODEL), lambda i: (layer, 0, 0)),
        ],
        out_specs=pl.BlockSpec((TM, D_MODEL), row),
        out_shape=jax.ShapeDtypeStruct((TOKENS, D_MODEL), F32),
        scratch_shapes=[pltpu.VMEM((TM, D_MODEL), BF16)],
        compiler_params=_params(1),
        name="merge_out",
    )(x, mod, gain, o_a, o_b, o_c, w_branch, w_gate, w_out)


def _ffn_kernel(x_ref, mod_ref, g_ref, gf_ref, wi_ref, wo_ref, out_ref, act_ref, *, tf, tn, final):
    h = _rms_mod(x_ref[...], g_ref[...], mod_ref[3:4, :], mod_ref[4:5, :]).astype(BF16)
    for c in range(FFN_HIDDEN // tf):
        gate = jnp.dot(h, wi_ref[:, c * tf:(c + 1) * tf], preferred_element_type=F32)
        up = jnp.dot(h, wi_ref[:, FFN_HIDDEN + c * tf:FFN_HIDDEN + (c + 1) * tf],
                     preferred_element_type=F32)
        act_ref[:, c * tf:(c + 1) * tf] = (gate * jax.nn.sigmoid(gate) * up).astype(BF16)
    act = act_ref[...]
    for n in range(D_MODEL // tn):
        sl = slice(n * tn, (n + 1) * tn)
        out = jnp.dot(act, wo_ref[:, sl], preferred_element_type=F32)
        out_ref[:, sl] = x_ref[:, sl] + mod_ref[5:6, sl] * out
    if final:
        y = out_ref[...]
        ms = jnp.mean(y * y, axis=-1, keepdims=True)
        out_ref[...] = y * lax.rsqrt(ms + EPS) * gf_ref[...]


def _ffn(x, mod, gain, final_gain, w_ffn_in, w_ffn_out, layer, final):
    steps_per_batch = SEQ // TM
    row = lambda i: (i, 0)
    return pl.pallas_call(
        functools.partial(_ffn_kernel, tf=256, tn=256, final=final),
        grid=(TOKENS // TM,),
        in_specs=[
            pl.BlockSpec((TM, D_MODEL), row),
            pl.BlockSpec((None, None, 6, D_MODEL), lambda i: (layer, i // steps_per_batch, 0, 0)),
            pl.BlockSpec((None, 1, D_MODEL), lambda i: (layer, 0, 0)),
            pl.BlockSpec((1, D_MODEL), lambda i: (0, 0)),
            pl.BlockSpec((None, D_MODEL, 2 * FFN_HIDDEN), lambda i: (layer, 0, 0)),
            pl.BlockSpec((None, FFN_HIDDEN, D_MODEL), lambda i: (layer, 0, 0)),
        ],
        out_specs=pl.BlockSpec((TM, D_MODEL), row),
        out_shape=jax.ShapeDtypeStruct((TOKENS, D_MODEL), F32),
        scratch_shapes=[pltpu.VMEM((TM, FFN_HIDDEN), BF16)],
        compiler_params=_params(1),
        name="ffn",
    )(x, mod, gain, final_gain, w_ffn_in, w_ffn_out)


def kernel(x, c, norm_mix_g, norm_ffn_g, w_ada, b_ada, w_in, b_forget, sinks, rel_bias,
           w_branch, w_out, w_ffn_in, w_ffn_out, final_norm_g):
    scale = HEAD_DIM ** -0.5
    q_cols = jnp.zeros((PROJ_COLS,), bool)
    for blk in (QA_BLK, QB_BLK, QC_BLK):
        q_cols = q_cols.at[blk * LANES:(blk + 4) * LANES].set(True)
    fb0 = 2304
    a_cols = (A_HEADS + 2 * A_KV_HEADS) * HEAD_DIM
    w_qkv = jnp.concatenate([w_in[:, :, a_cols:fb0], w_in[:, :, fb0 + B_HEADS:fb0 + B_HEADS + 1536],
                             w_in[:, :, :a_cols]], axis=-1)
    w_qkv = (w_qkv * jnp.where(q_cols, scale, 1.0)).astype(BF16)
    w_fb = jnp.pad(w_in[:, :, fb0:fb0 + B_HEADS], ((0, 0), (0, 0), (0, LANES - B_HEADS))).astype(BF16)
    w_gate = w_in[:, :, fb0 + B_HEADS + 1536:].astype(BF16)
    w_branch_b = w_branch.astype(BF16)
    w_out_b = w_out.astype(BF16)
    w_ffn_in_b = w_ffn_in.astype(BF16)
    w_ffn_out_b = w_ffn_out.astype(BF16)
    b_forget_rows = jnp.pad(b_forget, ((0, 0), (0, LANES - B_HEADS))).reshape(DEPTH, 1, LANES)
    gain_mix = norm_mix_g.reshape(DEPTH, 1, D_MODEL)
    gain_ffn = norm_ffn_g.reshape(DEPTH, 1, D_MODEL)
    gain_final = final_norm_g.reshape(1, D_MODEL)

    mod = _ada_mod(c, w_ada, b_ada)
    bias_c = _rel_bias(rel_bias)

    xt = x.reshape(TOKENS, D_MODEL)
    for layer in range(DEPTH):
        proj, fb = _in_proj(xt, mod, gain_mix, w_qkv, w_fb, layer)
        ccol, crow = _forget_cumsum(fb, b_forget_rows[layer])
        o_a = _swa_attention(proj, sinks, layer)
        o_b = _fox_attention(proj, ccol, crow)
        o_c = _chunked_attention(proj, bias_c, layer)
        xt = _merge(xt, mod, gain_mix, o_a, o_b, o_c, w_branch_b, w_gate, w_out_b, layer)
        xt = _ffn(xt, mod, gain_ffn, gain_final, w_ffn_in_b, w_ffn_out_b, layer, layer == DEPTH - 1)
    return xt.reshape(BATCH, SEQ, D_MODEL)
```

```python
import functools

import jax
import jax.numpy as jnp
from jax import lax
from jax.experimental import pallas as pl
from jax.experimental.pallas import tpu as pltpu

F32 = jnp.float32
BF16 = jnp.bfloat16

D_MODEL = 1024
BATCH = 8
SEQ = 2048
TOKENS = BATCH * SEQ
DEPTH = 2
CHUNK = 64
HEAD_DIM = 64
EPS = 1e-6
NEG_INF = -1e30
LOG2E = 1.4426950408889634

A_HEADS = 8
A_KV_HEADS = 2
A_GROUP = A_HEADS // A_KV_HEADS
A_PREV = 2
A_BAND = (A_PREV + 1) * CHUNK
B_HEADS = 8
C_HEADS = 8
C_PREV = 8
C_PAD = C_PREV * CHUNK
C_BAND = (C_PREV + 1) * CHUNK
REL_CLIP = 128
N_REL = 2 * REL_CLIP + 1
BRANCH_WIDTH = 512
FFN_HIDDEN = 2816
N_CHUNKS = SEQ // CHUNK

LANES = 128
N_PAIRS = 4

PROJ_COLS = 3840
QB_BLK, KB_BLK, VB_BLK = 0, 4, 8
QC_BLK, KC_BLK, VC_BLK = 12, 16, 20
QA_BLK, KA_BLK, VA_BLK = 24, 28, 29

TM = 512
FOX_TQ = 512
FOX_TK = 256
FOX_PAIRS = 1
CUM_BLK = 256

VMEM_LIMIT = 56 * 1024 * 1024


def _params(n_axes):
    return pltpu.CompilerParams(dimension_semantics=("arbitrary",) * n_axes,
                                vmem_limit_bytes=VMEM_LIMIT)


def _rms_mod(x, g, shift, scale):
    ms = jnp.mean(x * x, axis=-1, keepdims=True)
    y = x * lax.rsqrt(ms + EPS) * g
    return y * (1.0 + scale) + shift


def _row_start(index, size):
    if isinstance(index, int):
        return index * size
    return pl.multiple_of(index * size, size)


def _lane_half(shape):
    return lax.broadcasted_iota(jnp.int32, shape, len(shape) - 1) < HEAD_DIM


def _ada_kernel(c_ref, w_ref, b_ref, o_ref):
    c = c_ref[...]
    cond = c * jax.nn.sigmoid(c)
    o_ref[...] = jnp.dot(cond.astype(BF16), w_ref[...].astype(BF16),
                         preferred_element_type=F32) + b_ref[...]


def _ada_mod(c, w_ada, b_ada):
    n_blk = 6
    out = pl.pallas_call(
        _ada_kernel,
        grid=(DEPTH, n_blk),
        in_specs=[
            pl.BlockSpec((BATCH, D_MODEL), lambda l, j: (0, 0)),
            pl.BlockSpec((None, D_MODEL, D_MODEL), lambda l, j: (l, 0, j)),
            pl.BlockSpec((None, 1, D_MODEL), lambda l, j: (l, 0, j)),
        ],
        out_specs=pl.BlockSpec((None, BATCH, D_MODEL), lambda l, j: (l, 0, j)),
        out_shape=jax.ShapeDtypeStruct((DEPTH, BATCH, n_blk * D_MODEL), F32),
        compiler_params=_params(2),
        name="ada_mod",
    )(c, w_ada, b_ada.reshape(DEPTH, 1, n_blk * D_MODEL))
    return out.reshape(DEPTH, BATCH, n_blk, D_MODEL)


def _relbias_kernel(tab_ref, o_ref):
    l = pl.program_id(0)
    h = pl.program_id(1)
    qi = lax.broadcasted_iota(jnp.int32, (CHUNK, C_BAND), 0)
    si = lax.broadcasted_iota(jnp.int32, (CHUNK, C_BAND), 1)
    idx = jnp.clip(C_PAD + qi - si, -REL_CLIP, REL_CLIP) + REL_CLIP

    def body(r, acc):
        return jnp.where(idx == r, tab_ref[l, h, r], acc)

    o_ref[...] = lax.fori_loop(0, N_REL, body, jnp.zeros((CHUNK, C_BAND), F32))


def _rel_bias(rel_bias):
    out = pl.pallas_call(
        _relbias_kernel,
        grid=(DEPTH, C_HEADS),
        in_specs=[pl.BlockSpec(memory_space=pltpu.SMEM)],
        out_specs=pl.BlockSpec((None, None, CHUNK, C_BAND), lambda l, h: (l, h, 0, 0)),
        out_shape=jax.ShapeDtypeStruct((DEPTH, C_HEADS, CHUNK, C_BAND), F32),
        compiler_params=_params(2),
        name="rel_bias",
    )(rel_bias)
    return out.reshape(DEPTH, N_PAIRS, 2 * CHUNK, C_BAND)


def _inproj_kernel(x_ref, mod_ref, g_ref, w_ref, wfb_ref, proj_ref, fb_ref, *, tn):
    h = _rms_mod(x_ref[...], g_ref[...], mod_ref[0:1, :], mod_ref[1:2, :]).astype(BF16)
    for j in range(PROJ_COLS // tn):
        sl = slice(j * tn, (j + 1) * tn)
        proj_ref[:, sl] = jnp.dot(h, w_ref[:, sl], preferred_element_type=F32).astype(BF16)
    fb_ref[...] = jnp.dot(h, wfb_ref[...], preferred_element_type=F32)


def _in_proj(x, mod, gain, w_qkv, w_fb, layer):
    tn = 768
    steps_per_batch = SEQ // TM
    return pl.pallas_call(
        functools.partial(_inproj_kernel, tn=tn),
        grid=(TOKENS // TM,),
        in_specs=[
            pl.BlockSpec((TM, D_MODEL), lambda i: (i, 0)),
            pl.BlockSpec((None, None, 6, D_MODEL), lambda i: (layer, i // steps_per_batch, 0, 0)),
            pl.BlockSpec((None, 1, D_MODEL), lambda i: (layer, 0, 0)),
            pl.BlockSpec((None, D_MODEL, PROJ_COLS), lambda i: (layer, 0, 0)),
            pl.BlockSpec((None, D_MODEL, LANES), lambda i: (layer, 0, 0)),
        ],
        out_specs=[
            pl.BlockSpec((TM, PROJ_COLS), lambda i: (i, 0)),
            pl.BlockSpec((TM, LANES), lambda i: (i, 0)),
        ],
        out_shape=[
            jax.ShapeDtypeStruct((TOKENS, PROJ_COLS), BF16),
            jax.ShapeDtypeStruct((TOKENS, LANES), F32),
        ],
        compiler_params=_params(1),
        name="in_proj",
    )(x, mod, gain, w_qkv, w_fb)


def _cumsum_kernel(fb_ref, bias_ref, col_ref, row_ref):
    r = lax.broadcasted_iota(jnp.int32, (CUM_BLK, CUM_BLK), 0)
    c = lax.broadcasted_iota(jnp.int32, (CUM_BLK, CUM_BLK), 1)
    tri = (r >= c).astype(F32)
    carry = jnp.zeros((1, LANES), F32)
    for blk in range(SEQ // CUM_BLK):
        rows = slice(blk * CUM_BLK, (blk + 1) * CUM_BLK)
        z = fb_ref[0, rows, :] + bias_ref[...]
        log_f = jnp.minimum(z, 0.0) - jnp.log1p(jnp.exp(-jnp.abs(z)))
        cum = jnp.dot(tri, log_f, preferred_element_type=F32,
                      precision=lax.Precision.HIGHEST) + carry
        col_ref[0, rows, :] = cum
        row_ref[0, :, rows] = cum.T[:B_HEADS, :]
        carry = cum[CUM_BLK - 1:CUM_BLK, :]


def _forget_cumsum(fb, b_forget_row):
    return pl.pallas_call(
        _cumsum_kernel,
        grid=(BATCH,),
        in_specs=[
            pl.BlockSpec((1, SEQ, LANES), lambda b: (b, 0, 0)),
            pl.BlockSpec((1, LANES), lambda b: (0, 0)),
        ],
        out_specs=[
            pl.BlockSpec((1, SEQ, LANES), lambda b: (b, 0, 0)),
            pl.BlockSpec((1, B_HEADS, SEQ), lambda b: (b, 0, 0)),
        ],
        out_shape=[
            jax.ShapeDtypeStruct((BATCH, SEQ, LANES), F32),
            jax.ShapeDtypeStruct((BATCH, B_HEADS, SEQ), F32),
        ],
        compiler_params=_params(1),
        name="forget_cumsum",
    )(fb.reshape(BATCH, SEQ, LANES), b_forget_row)


def _fox_kernel(q_ref, k_ref, v_ref, ccol_ref, crow_ref, o_ref, vt_ref, ckb_ref,
                s0_ref, s1_ref, p0_ref, p1_ref, acc_ref, m_ref, l_ref, a0_ref, a1_ref):
    tq, tk = FOX_TQ, FOX_TK
    heads = 2 * FOX_PAIRS
    grp = pl.program_id(1)
    qi = pl.program_id(2)
    first = _lane_half((1, LANES))

    @pl.when(qi == 0)
    def _():
        head_lane = lax.broadcasted_iota(jnp.int32, (1, LANES), 1)
        for jt in range(SEQ // tk):
            rows = slice(jt * tk, (jt + 1) * tk)
            vt_ref[jt] = v_ref[rows, :].astype(F32).T.astype(BF16)
            ccol = ccol_ref[0, rows, :]
            for h in range(heads):
                col = jnp.sum(jnp.where(head_lane == grp * heads + h, ccol, 0.0), axis=1, keepdims=True)
                ckb_ref[h, rows, :] = jnp.broadcast_to(col * LOG2E, (tk, LANES))

    qms, cqs = [], []
    for h in range(heads):
        q2 = q_ref[:, (h // 2) * LANES:(h // 2 + 1) * LANES]
        qms.append(jnp.where(first if h % 2 == 0 else ~first, q2, jnp.zeros_like(q2)))
        cqs.append(crow_ref[0, grp * heads + h, pl.ds(qi, 1), :] * LOG2E)

    def scores(j, s_ref):
        row0 = _row_start(j, tk)
        for h in range(heads):
            kj = k_ref[pl.ds(row0, tk), (h // 2) * LANES:(h // 2 + 1) * LANES]
            s = lax.dot_general(kj, qms[h], (((1,), (1,)), ((), ())), preferred_element_type=F32)
            ck = ckb_ref[h, pl.ds(row0, tk), :]
            s_ref[h] = s + cqs[h] - jnp.concatenate([ck] * (tq // LANES), axis=1)

    def softmax(s_ref, p_ref, a_ref, mask):
        for h in range(heads):
            s = s_ref[h]
            if mask is not None:
                s = jnp.where(mask, s, NEG_INF)
            m_old = m_ref[h]
            m_new = jnp.maximum(m_old, jnp.max(s, axis=0, keepdims=True))
            alpha = jnp.exp2(m_old - m_new)
            p = jnp.exp2(s - m_new)
            p_ref[h] = p.astype(BF16)
            m_ref[h] = m_new
            l_ref[h] = alpha * l_ref[h] + jnp.sum(p, axis=0, keepdims=True)
            a_ref[h] = alpha

    def accumulate(j, p_ref, a_ref):
        tile = jnp.maximum(j, 0)
        for h in range(heads):
            vt = vt_ref[tile, h * HEAD_DIM:(h + 1) * HEAD_DIM, :]
            pv = jnp.dot(vt, p_ref[h], preferred_element_type=F32)
            acc_ref[h] = a_ref[h] * acc_ref[h] + pv

    def body(i, carry):
        even = 2 * i
        scores(even + 1, s1_ref)
        accumulate(even - 1, p1_ref, a1_ref)
        softmax(s0_ref, p0_ref, a0_ref, None)
        scores(even + 2, s0_ref)
        accumulate(even, p0_ref, a0_ref)
        softmax(s1_ref, p1_ref, a1_ref, None)
        return carry

    acc_ref[...] = jnp.zeros(acc_ref.shape, F32)
    m_ref[...] = jnp.full(m_ref.shape, NEG_INF, F32)
    l_ref[...] = jnp.zeros(l_ref.shape, F32)
    p1_ref[...] = jnp.zeros(p1_ref.shape, BF16)
    a1_ref[...] = jnp.ones(a1_ref.shape, F32)
    scores(0, s0_ref)
    lax.fori_loop(0, qi, body, 0)
    diag = 2 * qi
    kpos = lax.broadcasted_iota(jnp.int32, (tk, tq), 0)
    qpos = lax.broadcasted_iota(jnp.int32, (tk, tq), 1)
    scores(diag + 1, s1_ref)
    accumulate(diag - 1, p1_ref, a1_ref)
    softmax(s0_ref, p0_ref, a0_ref, kpos <= qpos)
    accumulate(diag, p0_ref, a0_ref)
    softmax(s1_ref, p1_ref, a1_ref, kpos + tk <= qpos)
    accumulate(diag + 1, p1_ref, a1_ref)
    out_t = jnp.concatenate([acc_ref[h] / l_ref[h] for h in range(heads)], axis=0)
    o_ref[...] = out_t.T.astype(BF16)


def _fox_attention(proj, ccol, crow):
    tq, tk = FOX_TQ, FOX_TK
    nq = SEQ // tq
    heads = 2 * FOX_PAIRS
    width = FOX_PAIRS * LANES
    return pl.pallas_call(
        _fox_kernel,
        grid=(BATCH, N_PAIRS // FOX_PAIRS, nq),
        in_specs=[
            pl.BlockSpec((tq, width), lambda b, g, i: (b * nq + i, QB_BLK // FOX_PAIRS + g)),
            pl.BlockSpec((SEQ, width), lambda b, g, i: (b, KB_BLK // FOX_PAIRS + g)),
            pl.BlockSpec((SEQ, width), lambda b, g, i: (b, VB_BLK // FOX_PAIRS + g)),
            pl.BlockSpec((1, SEQ, LANES), lambda b, g, i: (b, 0, 0)),
            pl.BlockSpec((1, B_HEADS, nq, tq), lambda b, g, i: (b, 0, 0, 0)),
        ],
        out_specs=pl.BlockSpec((tq, width), lambda b, g, i: (b * nq + i, g)),
        out_shape=jax.ShapeDtypeStruct((TOKENS, BRANCH_WIDTH), BF16),
        scratch_shapes=[pltpu.VMEM((SEQ // tk, width, tk), BF16),
                        pltpu.VMEM((heads, SEQ, LANES), F32),
                        pltpu.VMEM((heads, tk, tq), F32), pltpu.VMEM((heads, tk, tq), F32),
                        pltpu.VMEM((heads, tk, tq), BF16), pltpu.VMEM((heads, tk, tq), BF16),
                        pltpu.VMEM((heads, HEAD_DIM, tq), F32)] + [pltpu.VMEM((heads, 1, tq), F32)] * 4,
        compiler_params=_params(3),
        name="fox_attention",
    )(proj, proj, proj, ccol, crow.reshape(BATCH, B_HEADS, nq, tq))


def _swa_kernel(sink_ref, q_ref, k_ref, v_ref, o_ref, kd_ref, vd_ref, s0_ref, s1_ref, p0_ref, p1_ref, *, layer):
    rows = A_GROUP * CHUNK
    first = _lane_half((1, LANES))
    grp = lax.broadcasted_iota(jnp.int32, (rows, 1), 0) // CHUNK
    qi = lax.broadcasted_iota(jnp.int32, (rows, A_BAND), 0) % CHUNK
    si = lax.broadcasted_iota(jnp.int32, (rows, A_BAND), 1)
    sel_r = lax.broadcasted_iota(jnp.int32, (LANES, LANES), 0)
    sel_c = lax.broadcasted_iota(jnp.int32, (LANES, LANES), 1)

    slopes, sinks = [], []
    for kvh in range(A_KV_HEADS):
        sel = (sel_r == kvh * HEAD_DIM + sel_c % HEAD_DIM).astype(BF16)
        kd_ref[kvh] = jnp.dot(k_ref[...], sel, preferred_element_type=F32).astype(BF16)
        vd_ref[kvh] = jnp.dot(v_ref[...], sel, preferred_element_type=F32).astype(BF16)
        slope = jnp.zeros((rows, 1), F32)
        sink = jnp.zeros((rows, 1), F32)
        for g in range(A_GROUP):
            head = kvh * A_GROUP + g
            slope = jnp.where(grp == g, 2.0 ** -(head + 1), slope)
            sink = jnp.where(grp == g, sink_ref[layer, head], sink)
        slopes.append(slope)
        sinks.append(sink)

    def band_start(n):
        if isinstance(n, int):
            return max(n - A_PREV, 0) * CHUNK
        return _row_start(jnp.maximum(n - A_PREV, 0), CHUNK)

    def scores(n, s_ref, alibi):
        q0 = _row_start(n, CHUNK)
        for kvh in range(A_KV_HEADS):
            parts = []
            for g in range(A_GROUP):
                head = kvh * A_GROUP + g
                blk = q_ref[pl.ds(q0, CHUNK), (head // 2) * LANES:(head // 2 + 1) * LANES]
                parts.append(jnp.where(first if head % 2 == 0 else ~first, blk, jnp.zeros_like(blk)))
            qs = jnp.concatenate(parts, axis=0)
            kb = kd_ref[kvh, pl.ds(band_start(n), A_BAND), :]
            s = lax.dot_general(qs, kb, (((1,), (1,)), ((), ())), preferred_element_type=F32)
            s_ref[kvh] = s + alibi[kvh]

    def softmax(s_ref, p_ref, valid):
        denoms = []
        for kvh in range(A_KV_HEADS):
            s = s_ref[kvh] if valid is None else jnp.where(valid, s_ref[kvh], NEG_INF)
            m = jnp.maximum(jnp.max(s, axis=1, keepdims=True), sinks[kvh])
            p = jnp.exp(s - m)
            denoms.append(jnp.sum(p, axis=1, keepdims=True) + jnp.exp(sinks[kvh] - m))
            p_ref[kvh] = p.astype(BF16)
        return tuple(denoms)

    def emit(n, p_ref, denoms):
        q0 = _row_start(n, CHUNK)
        for kvh in range(A_KV_HEADS):
            vb = vd_ref[kvh, pl.ds(band_start(n), A_BAND), :]
            r = jnp.dot(p_ref[kvh], vb, preferred_element_type=F32) / denoms[kvh]
            for pr in range(A_GROUP // 2):
                even = r[(2 * pr) * CHUNK:(2 * pr + 1) * CHUNK]
                odd = r[(2 * pr + 1) * CHUNK:(2 * pr + 2) * CHUNK]
                col = (kvh * (A_GROUP // 2) + pr) * LANES
                o_ref[pl.ds(q0, CHUNK), col:col + LANES] = jnp.where(first, even, odd).astype(BF16)

    def alibi_for(dist):
        return tuple(-slope * jnp.abs(dist).astype(F32) for slope in slopes)

    alibi = alibi_for(A_PREV * CHUNK + qi - si)
    def trip(i, denoms_prev, *, first_trip=False, last_trip=False):
        even = 2 * i
        if first_trip:
            scores(1, s1_ref, alibi_for(CHUNK + qi - si))
            denoms_even = softmax(s0_ref, p0_ref, si // CHUNK <= 0)
        else:
            scores(even + 1, s1_ref, alibi)
            emit(even - 1, p1_ref, denoms_prev)
            denoms_even = softmax(s0_ref, p0_ref, None)
        if not last_trip:
            scores(even + 2, s0_ref, alibi)
        emit(even, p0_ref, denoms_even)
        return softmax(s1_ref, p1_ref, si // CHUNK <= 1 if first_trip else None)

    scores(0, s0_ref, alibi_for(qi - si))
    denoms = trip(0, None, first_trip=True)
    denoms = lax.fori_loop(1, N_CHUNKS // 2 - 1, trip, denoms)
    denoms = trip(N_CHUNKS // 2 - 1, denoms, last_trip=True)
    emit(N_CHUNKS - 1, p1_ref, denoms)


def _swa_attention(proj, sinks, layer):
    return pl.pallas_call(
        functools.partial(_swa_kernel, layer=layer),
        grid=(BATCH,),
        in_specs=[
            pl.BlockSpec(memory_space=pltpu.SMEM),
            pl.BlockSpec((SEQ, A_HEADS * HEAD_DIM), lambda b: (b, QA_BLK // N_PAIRS)),
            pl.BlockSpec((SEQ, LANES), lambda b: (b, KA_BLK)),
            pl.BlockSpec((SEQ, LANES), lambda b: (b, VA_BLK)),
        ],
        out_specs=pl.BlockSpec((SEQ, BRANCH_WIDTH), lambda b: (b, 0)),
        out_shape=jax.ShapeDtypeStruct((TOKENS, BRANCH_WIDTH), BF16),
        scratch_shapes=[pltpu.VMEM((A_KV_HEADS, SEQ, LANES), BF16),
                        pltpu.VMEM((A_KV_HEADS, SEQ, LANES), BF16),
                        pltpu.VMEM((A_KV_HEADS, A_GROUP * CHUNK, A_BAND), F32),
                        pltpu.VMEM((A_KV_HEADS, A_GROUP * CHUNK, A_BAND), F32),
                        pltpu.VMEM((A_KV_HEADS, A_GROUP * CHUNK, A_BAND), BF16),
                        pltpu.VMEM((A_KV_HEADS, A_GROUP * CHUNK, A_BAND), BF16)],
        compiler_params=_params(1),
        name="swa_attention",
    )(sinks, proj, proj, proj)


def _chunked_kernel(q_ref, k_ref, v_ref, bias_ref, o_ref, kp_ref, vp_ref, s0_ref, s1_ref, p0_ref, p1_ref):
    first = _lane_half((1, LANES))
    width = N_PAIRS * LANES
    kp_ref[0:C_PAD, :] = jnp.zeros((C_PAD, width), BF16)
    vp_ref[0:C_PAD, :] = jnp.zeros((C_PAD, width), BF16)
    kp_ref[C_PAD:, :] = k_ref[...]
    vp_ref[C_PAD:, :] = v_ref[...]
    si = lax.broadcasted_iota(jnp.int32, (2 * CHUNK, C_BAND), 1)

    def scores(n, s_ref):
        q0 = _row_start(n, CHUNK)
        for pair in range(N_PAIRS):
            cols = slice(pair * LANES, (pair + 1) * LANES)
            q2 = q_ref[pl.ds(q0, CHUNK), cols]
            zero = jnp.zeros_like(q2)
            qs = jnp.concatenate([jnp.where(first, q2, zero), jnp.where(first, zero, q2)], axis=0)
            kb = kp_ref[pl.ds(q0, C_BAND), cols]
            s = lax.dot_general(qs, kb, (((1,), (1,)), ((), ())), preferred_element_type=F32)
            s_ref[pair] = s + bias_ref[pair]

    def softmax(n, s_ref, p_ref, masked):
        denoms = []
        for pair in range(N_PAIRS):
            s = s_ref[pair]
            if masked:
                s = jnp.where(n * CHUNK + si >= C_PAD, s, NEG_INF)
            p = jnp.exp(s - jnp.max(s, axis=1, keepdims=True))
            denoms.append(jnp.sum(p, axis=1, keepdims=True))
            p_ref[pair] = p.astype(BF16)
        return tuple(denoms)

    def emit(n, p_ref, denoms):
        q0 = _row_start(n, CHUNK)
        for pair in range(N_PAIRS):
            cols = slice(pair * LANES, (pair + 1) * LANES)
            vb = vp_ref[pl.ds(q0, C_BAND), cols]
            r = jnp.dot(p_ref[pair], vb, preferred_element_type=F32) / denoms[pair]
            o_ref[pl.ds(q0, CHUNK), cols] = jnp.where(first, r[:CHUNK], r[CHUNK:]).astype(BF16)

    def trip(i, denoms_prev, *, masked, first_trip=False, last_trip=False):
        even = 2 * i
        scores(even + 1, s1_ref)
        if not first_trip:
            emit(even - 1, p1_ref, denoms_prev)
        denoms_even = softmax(even, s0_ref, p0_ref, masked)
        if not last_trip:
            scores(even + 2, s0_ref)
        emit(even, p0_ref, denoms_even)
        return softmax(even + 1, s1_ref, p1_ref, masked)

    masked_trips = C_PREV // 2
    scores(0, s0_ref)
    denoms = trip(0, None, masked=True, first_trip=True)
    denoms = lax.fori_loop(1, masked_trips, functools.partial(trip, masked=True), denoms)
    denoms = lax.fori_loop(masked_trips, N_CHUNKS // 2 - 1, functools.partial(trip, masked=False), denoms)
    denoms = trip(N_CHUNKS // 2 - 1, denoms, masked=False, last_trip=True)
    emit(N_CHUNKS - 1, p1_ref, denoms)


def _chunked_attention(proj, bias, layer):
    width = N_PAIRS * LANES
    return pl.pallas_call(
        _chunked_kernel,
        grid=(BATCH,),
        in_specs=[
            pl.BlockSpec((SEQ, width), lambda b: (b, QC_BLK // N_PAIRS)),
            pl.BlockSpec((SEQ, width), lambda b: (b, KC_BLK // N_PAIRS)),
            pl.BlockSpec((SEQ, width), lambda b: (b, VC_BLK // N_PAIRS)),
            pl.BlockSpec((None, N_PAIRS, 2 * CHUNK, C_BAND), lambda b: (layer, 0, 0, 0)),
        ],
        out_specs=pl.BlockSpec((SEQ, width), lambda b: (b, 0)),
        out_shape=jax.ShapeDtypeStruct((TOKENS, BRANCH_WIDTH), BF16),
        scratch_shapes=[pltpu.VMEM((C_PAD + SEQ, width), BF16), pltpu.VMEM((C_PAD + SEQ, width), BF16),
                        pltpu.VMEM((N_PAIRS, 2 * CHUNK, C_BAND), F32), pltpu.VMEM((N_PAIRS, 2 * CHUNK, C_BAND), F32),
                        pltpu.VMEM((N_PAIRS, 2 * CHUNK, C_BAND), BF16), pltpu.VMEM((N_PAIRS, 2 * CHUNK, C_BAND), BF16)],
        compiler_params=_params(1),
        name="chunked_attention",
    )(proj, proj, proj, bias)


def _merge_kernel(x_ref, mod_ref, g_ref, oa_ref, ob_ref, oc_ref, wb_ref, wg_ref, wo_ref, out_ref,
                  merged_ref, *, tn):
    h = _rms_mod(x_ref[...], g_ref[...], mod_ref[0:1, :], mod_ref[1:2, :]).astype(BF16)
    branches = (oa_ref[...], ob_ref[...], oc_ref[...])
    for n in range(D_MODEL // tn):
        acc = None
        for k, o in enumerate(branches):
            y = jnp.dot(o, wb_ref[k, :, n * tn:(n + 1) * tn], preferred_element_type=F32)
            gate = jnp.dot(h, wg_ref[:, k * D_MODEL + n * tn:k * D_MODEL + (n + 1) * tn],
                           preferred_element_type=F32)
            term = jax.nn.sigmoid(gate) * y
            acc = term if acc is None else acc + term
        merged_ref[:, n * tn:(n + 1) * tn] = acc.astype(BF16)
    merged = merged_ref[...]
    for n in range(D_MODEL // tn):
        sl = slice(n * tn, (n + 1) * tn)
        out = jnp.dot(merged, wo_ref[:, sl], preferred_element_type=F32)
        out_ref[:, sl] = x_ref[:, sl] + mod_ref[2:3, sl] * out


def _merge(x, mod, gain, o_a, o_b, o_c, w_branch, w_gate, w_out, layer):
    steps_per_batch = SEQ // TM
    row = lambda i: (i, 0)
    return pl.pallas_call(
        functools.partial(_merge_kernel, tn=256),
        grid=(TOKENS // TM,),
        in_specs=[
            pl.BlockSpec((TM, D_MODEL), row),
            pl.BlockSpec((None, None, 6, D_MODEL), lambda i: (layer, i // steps_per_batch, 0, 0)),
            pl.BlockSpec((None, 1, D_MODEL), lambda i: (layer, 0, 0)),
            pl.BlockSpec((TM, BRANCH_WIDTH), row),
            pl.BlockSpec((TM, BRANCH_WIDTH), row),
            pl.BlockSpec((TM, BRANCH_WIDTH), row),
            pl.BlockSpec((None, 3, BRANCH_WIDTH, D_MODEL), lambda i: (layer, 0, 0, 0)),
            pl.BlockSpec((None, D_MODEL, 3 * D_MODEL), lambda i: (layer, 0, 0)),
            pl.BlockSpec((None, D_MODEL, D_MODEL), lambda i: (layer, 0, 0)),
        ],
        out_specs=pl.BlockSpec((TM, D_MODEL), row),
        out_shape=jax.ShapeDtypeStruct((TOKENS, D_MODEL), F32),
        scratch_shapes=[pltpu.VMEM((TM, D_MODEL), BF16)],
        compiler_params=_params(1),
        name="merge_out",
    )(x, mod, gain, o_a, o_b, o_c, w_branch, w_gate, w_out)


def _ffn_kernel(x_ref, mod_ref, g_ref, gf_ref, wi_ref, wo_ref, out_ref, act_ref, *, tf, tn, final):
    h = _rms_mod(x_ref[...], g_ref[...], mod_ref[3:4, :], mod_ref[4:5, :]).astype(BF16)
    for c in range(FFN_HIDDEN // tf):
        gate = jnp.dot(h, wi_ref[:, c * tf:(c + 1) * tf], preferred_element_type=F32)
        up = jnp.dot(h, wi_ref[:, FFN_HIDDEN + c * tf:FFN_HIDDEN + (c + 1) * tf],
                     preferred_element_type=F32)
        act_ref[:, c * tf:(c + 1) * tf] = (gate * jax.nn.sigmoid(gate) * up).astype(BF16)
    act = act_ref[...]
    for n in range(D_MODEL // tn):
        sl = slice(n * tn, (n + 1) * tn)
        out = jnp.dot(act, wo_ref[:, sl], preferred_element_type=F32)
        out_ref[:, sl] = x_ref[:, sl] + mod_ref[5:6, sl] * out
    if final:
        y = out_ref[...]
        ms = jnp.mean(y * y, axis=-1, keepdims=True)
        out_ref[...] = y * lax.rsqrt(ms + EPS) * gf_ref[...]


def _ffn(x, mod, gain, final_gain, w_ffn_in, w_ffn_out, layer, final):
    steps_per_batch = SEQ // TM
    row = lambda i: (i, 0)
    return pl.pallas_call(
        functools.partial(_ffn_kernel, tf=256, tn=256, final=final),
        grid=(TOKENS // TM,),
        in_specs=[
            pl.BlockSpec((TM, D_MODEL), row),
            pl.BlockSpec((None, None, 6, D_MODEL), lambda i: (layer, i // steps_per_batch, 0, 0)),
            pl.BlockSpec((None, 1, D_MODEL), lambda i: (layer, 0, 0)),
            pl.BlockSpec((1, D_MODEL), lambda i: (0, 0)),
            pl.BlockSpec((None, D_MODEL, 2 * FFN_HIDDEN), lambda i: (layer, 0, 0)),
            pl.BlockSpec((None, FFN_HIDDEN, D_MODEL), lambda i: (layer, 0, 0)),
        ],
        out_specs=pl.BlockSpec((TM, D_MODEL), row),
        out_shape=jax.ShapeDtypeStruct((TOKENS, D_MODEL), F32),
        scratch_shapes=[pltpu.VMEM((TM, FFN_HIDDEN), BF16)],
        compiler_params=_params(1),
        name="ffn",
    )(x, mod, gain, final_gain, w_ffn_in, w_ffn_out)


def kernel(x, c, norm_mix_g, norm_ffn_g, w_ada, b_ada, w_in, b_forget, sinks, rel_bias,
           w_branch, w_out, w_ffn_in, w_ffn_out, final_norm_g):
    scale = HEAD_DIM ** -0.5
    q_scale = jnp.ones((PROJ_COLS,), F32)
    for blk, mult in ((QA_BLK, scale), (QB_BLK, scale * LOG2E), (QC_BLK, scale)):
        q_scale = q_scale.at[blk * LANES:(blk + 4) * LANES].set(mult)
    fb0 = 2304
    a_cols = (A_HEADS + 2 * A_KV_HEADS) * HEAD_DIM
    w_qkv = jnp.concatenate([w_in[:, :, a_cols:fb0], w_in[:, :, fb0 + B_HEADS:fb0 + B_HEADS + 1536],
                             w_in[:, :, :a_cols]], axis=-1)
    w_qkv = (w_qkv * q_scale).astype(BF16)
    w_fb = jnp.pad(w_in[:, :, fb0:fb0 + B_HEADS], ((0, 0), (0, 0), (0, LANES - B_HEADS))).astype(BF16)
    w_gate = w_in[:, :, fb0 + B_HEADS + 1536:].astype(BF16)
    w_branch_b = w_branch.astype(BF16)
    w_out_b = w_out.astype(BF16)
    w_ffn_in_b = w_ffn_in.astype(BF16)
    w_ffn_out_b = w_ffn_out.astype(BF16)
    b_forget_rows = jnp.pad(b_forget, ((0, 0), (0, LANES - B_HEADS))).reshape(DEPTH, 1, LANES)
    gain_mix = norm_mix_g.reshape(DEPTH, 1, D_MODEL)
    gain_ffn = norm_ffn_g.reshape(DEPTH, 1, D_MODEL)
    gain_final = final_norm_g.reshape(1, D_MODEL)

    mod = _ada_mod(c, w_ada, b_ada)
    bias_c = _rel_bias(rel_bias)

    xt = x.reshape(TOKENS, D_MODEL)
    for layer in range(DEPTH):
        proj, fb = _in_proj(xt, mod, gain_mix, w_qkv, w_fb, layer)
        ccol, crow = _forget_cumsum(fb, b_forget_rows[layer])
        o_a = _swa_attention(proj, sinks, layer)
        o_b = _fox_attention(proj, ccol, crow)
        o_c = _chunked_attention(proj, bias_c, layer)
        xt = _merge(xt, mod, gain_mix, o_a, o_b, o_c, w_branch_b, w_gate, w_out_b, layer)
        xt = _ffn(xt, mod, gain_ffn, gain_final, w_ffn_in_b, w_ffn_out_b, layer, layer == DEPTH - 1)
    return xt.reshape(BATCH, SEQ, D_MODEL)
```

```python
import functools

import jax
import jax.numpy as jnp
from jax import lax
from jax.experimental import pallas as pl
from jax.experimental.pallas import tpu as pltpu

F32 = jnp.float32
BF16 = jnp.bfloat16

D_MODEL = 1024
BATCH = 8
SEQ = 2048
TOKENS = BATCH * SEQ
DEPTH = 2
CHUNK = 64
HEAD_DIM = 64
EPS = 1e-6
NEG_INF = -1e30
LOG2E = 1.4426950408889634

A_HEADS = 8
A_KV_HEADS = 2
A_GROUP = A_HEADS // A_KV_HEADS
A_PREV = 2
A_BAND = (A_PREV + 1) * CHUNK
B_HEADS = 8
C_HEADS = 8
C_PREV = 8
C_PAD = C_PREV * CHUNK
C_BAND = (C_PREV + 1) * CHUNK
REL_CLIP = 128
N_REL = 2 * REL_CLIP + 1
BRANCH_WIDTH = 512
FFN_HIDDEN = 2816
N_CHUNKS = SEQ // CHUNK

LANES = 128
N_PAIRS = 4

PROJ_COLS = 3840
QB_BLK, KB_BLK, VB_BLK = 0, 4, 8
QC_BLK, KC_BLK, VC_BLK = 12, 16, 20
QA_BLK, KA_BLK, VA_BLK = 24, 28, 29

TM = 512
FOX_TQ = 512
FOX_TK = 256
FOX_VROWS = HEAD_DIM + 16
FOX_PAIRS = 2
CUM_BLK = 256

VMEM_LIMIT = 56 * 1024 * 1024


def _params(n_axes):
    return pltpu.CompilerParams(dimension_semantics=("arbitrary",) * n_axes,
                                vmem_limit_bytes=VMEM_LIMIT)


def _rms_mod(x, g, shift, scale):
    ms = jnp.mean(x * x, axis=-1, keepdims=True)
    y = x * lax.rsqrt(ms + EPS) * g
    return y * (1.0 + scale) + shift


def _row_start(index, size):
    if isinstance(index, int):
        return index * size
    return pl.multiple_of(index * size, size)


def _lane_half(shape):
    return lax.broadcasted_iota(jnp.int32, shape, len(shape) - 1) < HEAD_DIM


def _ada_kernel(c_ref, w_ref, b_ref, o_ref):
    c = c_ref[...]
    cond = c * jax.nn.sigmoid(c)
    o_ref[...] = jnp.dot(cond.astype(BF16), w_ref[...].astype(BF16),
                         preferred_element_type=F32) + b_ref[...]


def _ada_mod(c, w_ada, b_ada):
    n_blk = 6
    out = pl.pallas_call(
        _ada_kernel,
        grid=(DEPTH, n_blk),
        in_specs=[
            pl.BlockSpec((BATCH, D_MODEL), lambda l, j: (0, 0)),
            pl.BlockSpec((None, D_MODEL, D_MODEL), lambda l, j: (l, 0, j)),
            pl.BlockSpec((None, 1, D_MODEL), lambda l, j: (l, 0, j)),
        ],
        out_specs=pl.BlockSpec((None, BATCH, D_MODEL), lambda l, j: (l, 0, j)),
        out_shape=jax.ShapeDtypeStruct((DEPTH, BATCH, n_blk * D_MODEL), F32),
        compiler_params=_params(2),
        name="ada_mod",
    )(c, w_ada, b_ada.reshape(DEPTH, 1, n_blk * D_MODEL))
    return out.reshape(DEPTH, BATCH, n_blk, D_MODEL)


def _relbias_kernel(tab_ref, o_ref):
    l = pl.program_id(0)
    h = pl.program_id(1)
    qi = lax.broadcasted_iota(jnp.int32, (CHUNK, C_BAND), 0)
    si = lax.broadcasted_iota(jnp.int32, (CHUNK, C_BAND), 1)
    idx = jnp.clip(C_PAD + qi - si, -REL_CLIP, REL_CLIP) + REL_CLIP

    def body(r, acc):
        return jnp.where(idx == r, tab_ref[l, h, r], acc)

    o_ref[...] = lax.fori_loop(0, N_REL, body, jnp.zeros((CHUNK, C_BAND), F32))


def _rel_bias(rel_bias):
    out = pl.pallas_call(
        _relbias_kernel,
        grid=(DEPTH, C_HEADS),
        in_specs=[pl.BlockSpec(memory_space=pltpu.SMEM)],
        out_specs=pl.BlockSpec((None, None, CHUNK, C_BAND), lambda l, h: (l, h, 0, 0)),
        out_shape=jax.ShapeDtypeStruct((DEPTH, C_HEADS, CHUNK, C_BAND), F32),
        compiler_params=_params(2),
        name="rel_bias",
    )(rel_bias)
    return out.reshape(DEPTH, N_PAIRS, 2 * CHUNK, C_BAND)


def _inproj_kernel(x_ref, mod_ref, g_ref, w_ref, wfb_ref, proj_ref, fb_ref, *, tn):
    h = _rms_mod(x_ref[...], g_ref[...], mod_ref[0:1, :], mod_ref[1:2, :]).astype(BF16)
    for j in range(PROJ_COLS // tn):
        sl = slice(j * tn, (j + 1) * tn)
        proj_ref[:, sl] = jnp.dot(h, w_ref[:, sl], preferred_element_type=F32).astype(BF16)
    fb_ref[...] = jnp.dot(h, wfb_ref[...], preferred_element_type=F32)


def _in_proj(x, mod, gain, w_qkv, w_fb, layer):
    tn = 768
    steps_per_batch = SEQ // TM
    return pl.pallas_call(
        functools.partial(_inproj_kernel, tn=tn),
        grid=(TOKENS // TM,),
        in_specs=[
            pl.BlockSpec((TM, D_MODEL), lambda i: (i, 0)),
            pl.BlockSpec((None, None, 6, D_MODEL), lambda i: (layer, i // steps_per_batch, 0, 0)),
            pl.BlockSpec((None, 1, D_MODEL), lambda i: (layer, 0, 0)),
            pl.BlockSpec((None, D_MODEL, PROJ_COLS), lambda i: (layer, 0, 0)),
            pl.BlockSpec((None, D_MODEL, LANES), lambda i: (layer, 0, 0)),
        ],
        out_specs=[
            pl.BlockSpec((TM, PROJ_COLS), lambda i: (i, 0)),
            pl.BlockSpec((TM, LANES), lambda i: (i, 0)),
        ],
        out_shape=[
            jax.ShapeDtypeStruct((TOKENS, PROJ_COLS), BF16),
            jax.ShapeDtypeStruct((TOKENS, LANES), F32),
        ],
        compiler_params=_params(1),
        name="in_proj",
    )(x, mod, gain, w_qkv, w_fb)


def _cumsum_kernel(fb_ref, bias_ref, col_ref, row_ref):
    r = lax.broadcasted_iota(jnp.int32, (CUM_BLK, CUM_BLK), 0)
    c = lax.broadcasted_iota(jnp.int32, (CUM_BLK, CUM_BLK), 1)
    tri = (r >= c).astype(F32)
    carry = jnp.zeros((1, LANES), F32)
    for blk in range(SEQ // CUM_BLK):
        rows = slice(blk * CUM_BLK, (blk + 1) * CUM_BLK)
        z = fb_ref[0, rows, :] + bias_ref[...]
        log_f = jnp.minimum(z, 0.0) - jnp.log1p(jnp.exp(-jnp.abs(z)))
        cum = jnp.dot(tri, log_f, preferred_element_type=F32,
                      precision=lax.Precision.HIGHEST) + carry
        col_ref[0, rows, :] = cum
        row_ref[0, :, rows] = cum.T[:B_HEADS, :]
        carry = cum[CUM_BLK - 1:CUM_BLK, :]


def _forget_cumsum(fb, b_forget_row):
    return pl.pallas_call(
        _cumsum_kernel,
        grid=(BATCH,),
        in_specs=[
            pl.BlockSpec((1, SEQ, LANES), lambda b: (b, 0, 0)),
            pl.BlockSpec((1, LANES), lambda b: (0, 0)),
        ],
        out_specs=[
            pl.BlockSpec((1, SEQ, LANES), lambda b: (b, 0, 0)),
            pl.BlockSpec((1, B_HEADS, SEQ), lambda b: (b, 0, 0)),
        ],
        out_shape=[
            jax.ShapeDtypeStruct((BATCH, SEQ, LANES), F32),
            jax.ShapeDtypeStruct((BATCH, B_HEADS, SEQ), F32),
        ],
        compiler_params=_params(1),
        name="forget_cumsum",
    )(fb.reshape(BATCH, SEQ, LANES), b_forget_row)


def _fox_kernel(q_ref, k_ref, v_ref, ccol_ref, crow_ref, o_ref, vt_ref, ckb_ref,
                s0_ref, s1_ref, p0_ref, p1_ref, acc_ref, m_ref, a0_ref, a1_ref, cq_ref, qm_ref):
    tq, tk = FOX_TQ, FOX_TK
    heads = 2 * FOX_PAIRS
    grp = pl.program_id(1)
    qi = pl.program_id(2)
    first = _lane_half((1, LANES))

    @pl.when(qi == 0)
    def _():
        head_lane = lax.broadcasted_iota(jnp.int32, (1, LANES), 1)
        ones_row = (lax.broadcasted_iota(jnp.int32, (FOX_VROWS - HEAD_DIM, tk), 0) == 0).astype(BF16)
        for jt in range(SEQ // tk):
            rows = slice(jt * tk, (jt + 1) * tk)
            v_t = v_ref[rows, :].astype(F32).T.astype(BF16)
            ccol = ccol_ref[0, rows, :]
            for h in range(heads):
                vt_ref[jt, h * FOX_VROWS:h * FOX_VROWS + HEAD_DIM, :] = v_t[h * HEAD_DIM:(h + 1) * HEAD_DIM]
                vt_ref[jt, h * FOX_VROWS + HEAD_DIM:(h + 1) * FOX_VROWS, :] = ones_row
                col = jnp.sum(jnp.where(head_lane == grp * heads + h, ccol, 0.0), axis=1, keepdims=True)
                ckb_ref[h, rows, :] = jnp.broadcast_to(col * LOG2E, (tk, LANES))

    for h in range(heads):
        q2 = q_ref[:, (h // 2) * LANES:(h // 2 + 1) * LANES]
        qm_ref[h] = jnp.where(first if h % 2 == 0 else ~first, q2, jnp.zeros_like(q2))
        cq_ref[h] = crow_ref[0, grp * heads + h, pl.ds(qi, 1), :] * LOG2E

    def scores(j, s_ref):
        row0 = _row_start(j, tk)
        for h in range(heads):
            kj = k_ref[pl.ds(row0, tk), (h // 2) * LANES:(h // 2 + 1) * LANES]
            s = lax.dot_general(kj, qm_ref[h], (((1,), (1,)), ((), ())), preferred_element_type=F32)
            ck = ckb_ref[h, pl.ds(row0, tk), :]
            s_ref[h] = s - jnp.concatenate([ck] * (tq // LANES), axis=1)

    def softmax(s_ref, p_ref, a_ref, mask):
        for h in range(heads):
            s = s_ref[h]
            if mask is not None:
                s = jnp.where(mask, s, NEG_INF)
            m_old = m_ref[h]
            cq = cq_ref[h]
            m_new = jnp.maximum(m_old, jnp.max(s, axis=0, keepdims=True) + cq)
            p_ref[h] = jnp.exp2(s + (cq - m_new)).astype(BF16)
            m_ref[h] = m_new
            a_ref[h] = jnp.exp2(m_old - m_new)

    def accumulate(j, p_ref, a_ref):
        tile = jnp.maximum(j, 0)
        for h in range(heads):
            vt = vt_ref[tile, h * FOX_VROWS:(h + 1) * FOX_VROWS, :]
            pv = jnp.dot(vt, p_ref[h], preferred_element_type=F32)
            acc_ref[h] = a_ref[h] * acc_ref[h] + pv

    def trip(i, mask_even, mask_odd, last):
        cur = i & 1
        even = 2 * i
        scores(even + 1, s1_ref)
        accumulate(even - 1, p1_ref.at[1 - cur], a1_ref.at[1 - cur])
        softmax(s0_ref.at[cur], p0_ref, a0_ref, mask_even)
        if not last:
            scores(even + 2, s0_ref.at[1 - cur])
        accumulate(even, p0_ref, a0_ref)
        softmax(s1_ref, p1_ref.at[cur], a1_ref.at[cur], mask_odd)
        if last:
            accumulate(even + 1, p1_ref.at[cur], a1_ref.at[cur])

    def body(i, carry):
        trip(i, None, None, last=False)
        return carry

    acc_ref[...] = jnp.zeros(acc_ref.shape, F32)
    m_ref[...] = jnp.full(m_ref.shape, NEG_INF, F32)
    p1_ref[1] = jnp.zeros(p1_ref.shape[1:], BF16)
    a1_ref[1] = jnp.ones(a1_ref.shape[1:], F32)
    scores(0, s0_ref.at[0])
    lax.fori_loop(0, qi, body, 0)
    kpos = lax.broadcasted_iota(jnp.int32, (tk, tq), 0)
    qpos = lax.broadcasted_iota(jnp.int32, (tk, tq), 1)
    trip(qi, kpos <= qpos, kpos + tk <= qpos, last=True)
    out_t = jnp.concatenate([acc_ref[h, :HEAD_DIM, :] / acc_ref[h, HEAD_DIM:HEAD_DIM + 1, :]
                             for h in range(heads)], axis=0)
    o_ref[...] = out_t.T.astype(BF16)


def _fox_attention(proj, ccol, crow):
    tq, tk = FOX_TQ, FOX_TK
    nq = SEQ // tq
    heads = 2 * FOX_PAIRS
    width = FOX_PAIRS * LANES
    return pl.pallas_call(
        _fox_kernel,
        grid=(BATCH, N_PAIRS // FOX_PAIRS, nq),
        in_specs=[
            pl.BlockSpec((tq, width), lambda b, g, i: (b * nq + i, QB_BLK // FOX_PAIRS + g)),
            pl.BlockSpec((SEQ, width), lambda b, g, i: (b, KB_BLK // FOX_PAIRS + g)),
            pl.BlockSpec((SEQ, width), lambda b, g, i: (b, VB_BLK // FOX_PAIRS + g)),
            pl.BlockSpec((1, SEQ, LANES), lambda b, g, i: (b, 0, 0)),
            pl.BlockSpec((1, B_HEADS, nq, tq), lambda b, g, i: (b, 0, 0, 0)),
        ],
        out_specs=pl.BlockSpec((tq, width), lambda b, g, i: (b * nq + i, g)),
        out_shape=jax.ShapeDtypeStruct((TOKENS, BRANCH_WIDTH), BF16),
        scratch_shapes=[pltpu.VMEM((SEQ // tk, heads * FOX_VROWS, tk), BF16),
                        pltpu.VMEM((heads, SEQ, LANES), F32),
                        pltpu.VMEM((2, heads, tk, tq), F32), pltpu.VMEM((heads, tk, tq), F32),
                        pltpu.VMEM((heads, tk, tq), BF16), pltpu.VMEM((2, heads, tk, tq), BF16),
                        pltpu.VMEM((heads, FOX_VROWS, tq), F32),
                        pltpu.VMEM((heads, 1, tq), F32),
                        pltpu.VMEM((heads, 1, tq), F32),
                        pltpu.VMEM((2, heads, 1, tq), F32),
                        pltpu.VMEM((heads, 1, tq), F32),
                        pltpu.VMEM((heads, tq, LANES), BF16)],
        compiler_params=_params(3),
        name="fox_attention",
    )(proj, proj, proj, ccol, crow.reshape(BATCH, B_HEADS, nq, tq))


def _swa_kernel(sink_ref, q_ref, k_ref, v_ref, o_ref, kd_ref, vd_ref,
                s0_ref, s1_ref, p0_ref, p1_ref, d0_ref, d1_ref, *, layer):
    rows = A_GROUP * CHUNK
    first = _lane_half((1, LANES))
    grp = lax.broadcasted_iota(jnp.int32, (rows, 1), 0) // CHUNK
    qi = lax.broadcasted_iota(jnp.int32, (rows, A_BAND), 0) % CHUNK
    si = lax.broadcasted_iota(jnp.int32, (rows, A_BAND), 1)
    sel_r = lax.broadcasted_iota(jnp.int32, (LANES, LANES), 0)
    sel_c = lax.broadcasted_iota(jnp.int32, (LANES, LANES), 1)

    slopes, sinks = [], []
    for kvh in range(A_KV_HEADS):
        sel = (sel_r == kvh * HEAD_DIM + sel_c % HEAD_DIM).astype(BF16)
        kd_ref[kvh] = jnp.dot(k_ref[...], sel, preferred_element_type=F32).astype(BF16)
        vd_ref[kvh] = jnp.dot(v_ref[...], sel, preferred_element_type=F32).astype(BF16)
        slope = jnp.zeros((rows, 1), F32)
        sink = jnp.zeros((rows, 1), F32)
        for g in range(A_GROUP):
            head = kvh * A_GROUP + g
            slope = jnp.where(grp == g, 2.0 ** -(head + 1), slope)
            sink = jnp.where(grp == g, sink_ref[layer, head], sink)
        slopes.append(slope)
        sinks.append(sink)

    def band_start(n):
        if isinstance(n, int):
            return max(n - A_PREV, 0) * CHUNK
        return _row_start(jnp.maximum(n - A_PREV, 0), CHUNK)

    def scores(n, s_ref, alibi):
        q0 = _row_start(n, CHUNK)
        for kvh in range(A_KV_HEADS):
            parts = []
            for g in range(A_GROUP):
                head = kvh * A_GROUP + g
                blk = q_ref[pl.ds(q0, CHUNK), (head // 2) * LANES:(head // 2 + 1) * LANES]
                parts.append(jnp.where(first if head % 2 == 0 else ~first, blk, jnp.zeros_like(blk)))
            qs = jnp.concatenate(parts, axis=0)
            kb = kd_ref[kvh, pl.ds(band_start(n), A_BAND), :]
            s = lax.dot_general(qs, kb, (((1,), (1,)), ((), ())), preferred_element_type=F32)
            s_ref[kvh] = s + alibi[kvh]

    def softmax(s_ref, p_ref, d_ref, valid):
        for kvh in range(A_KV_HEADS):
            s = s_ref[kvh] if valid is None else jnp.where(valid, s_ref[kvh], NEG_INF)
            m = jnp.maximum(jnp.max(s, axis=1, keepdims=True), sinks[kvh])
            p = jnp.exp(s - m)
            denom = jnp.sum(p, axis=1, keepdims=True) + jnp.exp(sinks[kvh] - m)
            d_ref[kvh] = jnp.broadcast_to(denom, (rows, LANES))
            p_ref[kvh] = p.astype(BF16)

    def emit(n, p_ref, d_ref):
        q0 = _row_start(n, CHUNK)
        for kvh in range(A_KV_HEADS):
            vb = vd_ref[kvh, pl.ds(band_start(n), A_BAND), :]
            r = jnp.dot(p_ref[kvh], vb, preferred_element_type=F32) / d_ref[kvh]
            for pr in range(A_GROUP // 2):
                even = r[(2 * pr) * CHUNK:(2 * pr + 1) * CHUNK]
                odd = r[(2 * pr + 1) * CHUNK:(2 * pr + 2) * CHUNK]
                col = (kvh * (A_GROUP // 2) + pr) * LANES
                o_ref[pl.ds(q0, CHUNK), col:col + LANES] = jnp.where(first, even, odd).astype(BF16)

    def alibi_for(dist):
        return tuple(-slope * jnp.abs(dist).astype(F32) for slope in slopes)

    alibi = alibi_for(A_PREV * CHUNK + qi - si)
    def trip(i, carry=0, *, first_trip=False, last_trip=False):
        cur = i & 1
        even = 2 * i
        if first_trip:
            scores(1, s1_ref, alibi_for(CHUNK + qi - si))
            softmax(s0_ref.at[cur], p0_ref, d0_ref, si // CHUNK <= 0)
        else:
            scores(even + 1, s1_ref, alibi)
            emit(even - 1, p1_ref.at[1 - cur], d1_ref.at[1 - cur])
            softmax(s0_ref.at[cur], p0_ref, d0_ref, None)
        if not last_trip:
            scores(even + 2, s0_ref.at[1 - cur], alibi)
        emit(even, p0_ref, d0_ref)
        softmax(s1_ref, p1_ref.at[cur], d1_ref.at[cur], si // CHUNK <= 1 if first_trip else None)
        if last_trip:
            emit(even + 1, p1_ref.at[cur], d1_ref.at[cur])
        return carry

    scores(0, s0_ref.at[0], alibi_for(qi - si))
    trip(0, first_trip=True)
    lax.fori_loop(1, N_CHUNKS // 2 - 1, trip, 0)
    trip(N_CHUNKS // 2 - 1, last_trip=True)


def _swa_attention(proj, sinks, layer):
    return pl.pallas_call(
        functools.partial(_swa_kernel, layer=layer),
        grid=(BATCH,),
        in_specs=[
            pl.BlockSpec(memory_space=pltpu.SMEM),
            pl.BlockSpec((SEQ, A_HEADS * HEAD_DIM), lambda b: (b, QA_BLK // N_PAIRS)),
            pl.BlockSpec((SEQ, LANES), lambda b: (b, KA_BLK)),
            pl.BlockSpec((SEQ, LANES), lambda b: (b, VA_BLK)),
        ],
        out_specs=pl.BlockSpec((SEQ, BRANCH_WIDTH), lambda b: (b, 0)),
        out_shape=jax.ShapeDtypeStruct((TOKENS, BRANCH_WIDTH), BF16),
        scratch_shapes=[pltpu.VMEM((A_KV_HEADS, SEQ, LANES), BF16),
                        pltpu.VMEM((A_KV_HEADS, SEQ, LANES), BF16),
                        pltpu.VMEM((2, A_KV_HEADS, A_GROUP * CHUNK, A_BAND), F32),
                        pltpu.VMEM((A_KV_HEADS, A_GROUP * CHUNK, A_BAND), F32),
                        pltpu.VMEM((A_KV_HEADS, A_GROUP * CHUNK, A_BAND), BF16),
                        pltpu.VMEM((2, A_KV_HEADS, A_GROUP * CHUNK, A_BAND), BF16),
                        pltpu.VMEM((A_KV_HEADS, A_GROUP * CHUNK, LANES), F32),
                        pltpu.VMEM((2, A_KV_HEADS, A_GROUP * CHUNK, LANES), F32)],
        compiler_params=_params(1),
        name="swa_attention",
    )(sinks, proj, proj, proj)


def _chunked_kernel(q_ref, k_ref, v_ref, bias_ref, o_ref, kp_ref, vp_ref,
                    s0_ref, s1_ref, p0_ref, p1_ref, d0_ref, d1_ref):
    first = _lane_half((1, LANES))
    width = N_PAIRS * LANES
    kp_ref[0:C_PAD, :] = jnp.zeros((C_PAD, width), BF16)
    vp_ref[0:C_PAD, :] = jnp.zeros((C_PAD, width), BF16)
    kp_ref[C_PAD:, :] = k_ref[...]
    vp_ref[C_PAD:, :] = v_ref[...]
    si = lax.broadcasted_iota(jnp.int32, (2 * CHUNK, C_BAND), 1)

    def scores(n, s_ref):
        q0 = _row_start(n, CHUNK)
        for pair in range(N_PAIRS):
            cols = slice(pair * LANES, (pair + 1) * LANES)
            q2 = q_ref[pl.ds(q0, CHUNK), cols]
            zero = jnp.zeros_like(q2)
            qs = jnp.concatenate([jnp.where(first, q2, zero), jnp.where(first, zero, q2)], axis=0)
            kb = kp_ref[pl.ds(q0, C_BAND), cols]
            s = lax.dot_general(qs, kb, (((1,), (1,)), ((), ())), preferred_element_type=F32)
            s_ref[pair] = s + bias_ref[pair]

    def softmax(n, s_ref, p_ref, d_ref, masked):
        for pair in range(N_PAIRS):
            s = s_ref[pair]
            if masked:
                s = jnp.where(n * CHUNK + si >= C_PAD, s, NEG_INF)
            p = jnp.exp(s - jnp.max(s, axis=1, keepdims=True))
            d_ref[pair] = jnp.sum(p, axis=1, keepdims=True)
            p_ref[pair] = p.astype(BF16)

    def emit(n, p_ref, d_ref):
        q0 = _row_start(n, CHUNK)
        for pair in range(N_PAIRS):
            cols = slice(pair * LANES, (pair + 1) * LANES)
            vb = vp_ref[pl.ds(q0, C_BAND), cols]
            r = jnp.dot(p_ref[pair], vb, preferred_element_type=F32) / d_ref[pair]
            o_ref[pl.ds(q0, CHUNK), cols] = jnp.where(first, r[:CHUNK], r[CHUNK:]).astype(BF16)

    def trip(i, carry=0, *, masked, first_trip=False, last_trip=False):
        cur = i & 1
        even = 2 * i
        scores(even + 1, s1_ref)
        if not first_trip:
            emit(even - 1, p1_ref.at[1 - cur], d1_ref.at[1 - cur])
        softmax(even, s0_ref.at[cur], p0_ref, d0_ref, masked)
        if not last_trip:
            scores(even + 2, s0_ref.at[1 - cur])
        emit(even, p0_ref, d0_ref)
        softmax(even + 1, s1_ref, p1_ref.at[cur], d1_ref.at[cur], masked)
        if last_trip:
            emit(even + 1, p1_ref.at[cur], d1_ref.at[cur])
        return carry

    masked_trips = C_PREV // 2
    scores(0, s0_ref.at[0])
    trip(0, masked=True, first_trip=True)
    lax.fori_loop(1, masked_trips, functools.partial(trip, masked=True), 0)
    lax.fori_loop(masked_trips, N_CHUNKS // 2 - 1, functools.partial(trip, masked=False), 0)
    trip(N_CHUNKS // 2 - 1, masked=False, last_trip=True)


def _chunked_attention(proj, bias, layer):
    width = N_PAIRS * LANES
    return pl.pallas_call(
        _chunked_kernel,
        grid=(BATCH,),
        in_specs=[
            pl.BlockSpec((SEQ, width), lambda b: (b, QC_BLK // N_PAIRS)),
            pl.BlockSpec((SEQ, width), lambda b: (b, KC_BLK // N_PAIRS)),
            pl.BlockSpec((SEQ, width), lambda b: (b, VC_BLK // N_PAIRS)),
            pl.BlockSpec((None, N_PAIRS, 2 * CHUNK, C_BAND), lambda b: (layer, 0, 0, 0)),
        ],
        out_specs=pl.BlockSpec((SEQ, width), lambda b: (b, 0)),
        out_shape=jax.ShapeDtypeStruct((TOKENS, BRANCH_WIDTH), BF16),
        scratch_shapes=[pltpu.VMEM((C_PAD + SEQ, width), BF16), pltpu.VMEM((C_PAD + SEQ, width), BF16),
                        pltpu.VMEM((2, N_PAIRS, 2 * CHUNK, C_BAND), F32),
                        pltpu.VMEM((N_PAIRS, 2 * CHUNK, C_BAND), F32),
                        pltpu.VMEM((N_PAIRS, 2 * CHUNK, C_BAND), BF16),
                        pltpu.VMEM((2, N_PAIRS, 2 * CHUNK, C_BAND), BF16),
                        pltpu.VMEM((N_PAIRS, 2 * CHUNK, 1), F32),
                        pltpu.VMEM((2, N_PAIRS, 2 * CHUNK, 1), F32)],
        compiler_params=_params(1),
        name="chunked_attention",
    )(proj, proj, proj, bias)


def _merge_kernel(x_ref, mod_ref, g_ref, oa_ref, ob_ref, oc_ref, wb_ref, wg_ref, wo_ref, out_ref,
                  merged_ref, *, tn):
    h = _rms_mod(x_ref[...], g_ref[...], mod_ref[0:1, :], mod_ref[1:2, :]).astype(BF16)
    branches = (oa_ref[...], ob_ref[...], oc_ref[...])
    for n in range(D_MODEL // tn):
        acc = None
        for k, o in enumerate(branches):
            y = jnp.dot(o, wb_ref[k, :, n * tn:(n + 1) * tn], preferred_element_type=F32)
            gate = jnp.dot(h, wg_ref[:, k * D_MODEL + n * tn:k * D_MODEL + (n + 1) * tn],
                           preferred_element_type=F32)
            term = jax.nn.sigmoid(gate) * y
            acc = term if acc is None else acc + term
        merged_ref[:, n * tn:(n + 1) * tn] = acc.astype(BF16)
    merged = merged_ref[...]
    for n in range(D_MODEL // tn):
        sl = slice(n * tn, (n + 1) * tn)
        out = jnp.dot(merged, wo_ref[:, sl], preferred_element_type=F32)
        out_ref[:, sl] = x_ref[:, sl] + mod_ref[2:3, sl] * out


def _merge(x, mod, gain, o_a, o_b, o_c, w_branch, w_gate, w_out, layer):
    steps_per_batch = SEQ // TM
    row = lambda i: (i, 0)
    return pl.pallas_call(
        functools.partial(_merge_kernel, tn=256),
        grid=(TOKENS // TM,),
        in_specs=[
            pl.BlockSpec((TM, D_MODEL), row),
            pl.BlockSpec((None, None, 6, D_MODEL), lambda i: (layer, i // steps_per_batch, 0, 0)),
            pl.BlockSpec((None, 1, D_MODEL), lambda i: (layer, 0, 0)),
            pl.BlockSpec((TM, BRANCH_WIDTH), row),
            pl.BlockSpec((TM, BRANCH_WIDTH), row),
            pl.BlockSpec((TM, BRANCH_WIDTH), row),
            pl.BlockSpec((None, 3, BRANCH_WIDTH, D_MODEL), lambda i: (layer, 0, 0, 0)),
            pl.BlockSpec((None, D_MODEL, 3 * D_MODEL), lambda i: (layer, 0, 0)),
            pl.BlockSpec((None, D_MODEL, D_MODEL), lambda i: (layer, 0, 0)),
        ],
        out_specs=pl.BlockSpec((TM, D_MODEL), row),
        out_shape=jax.ShapeDtypeStruct((TOKENS, D_MODEL), F32),
        scratch_shapes=[pltpu.VMEM((TM, D_MODEL), BF16)],
        compiler_params=_params(1),
        name="merge_out",
    )(x, mod, gain, o_a, o_b, o_c, w_branch, w_gate, w_out)


def _ffn_kernel(x_ref, mod_ref, g_ref, gf_ref, wi_ref, wo_ref, out_ref, act_ref, *, tf, tn, final):
    h = _rms_mod(x_ref[...], g_ref[...], mod_ref[3:4, :], mod_ref[4:5, :]).astype(BF16)
    for c in range(FFN_HIDDEN // tf):
        gate = jnp.dot(h, wi_ref[:, c * tf:(c + 1) * tf], preferred_element_type=F32)
        up = jnp.dot(h, wi_ref[:, FFN_HIDDEN + c * tf:FFN_HIDDEN + (c + 1) * tf],
                     preferred_element_type=F32)
        act_ref[:, c * tf:(c + 1) * tf] = (gate * jax.nn.sigmoid(gate) * up).astype(BF16)
    act = act_ref[...]
    for n in range(D_MODEL // tn):
        sl = slice(n * tn, (n + 1) * tn)
        out = jnp.dot(act, wo_ref[:, sl], preferred_element_type=F32)
        out_ref[:, sl] = x_ref[:, sl] + mod_ref[5:6, sl] * out
    if final:
        y = out_ref[...]
        ms = jnp.mean(y * y, axis=-1, keepdims=True)
        out_ref[...] = y * lax.rsqrt(ms + EPS) * gf_ref[...]


def _ffn(x, mod, gain, final_gain, w_ffn_in, w_ffn_out, layer, final):
    steps_per_batch = SEQ // TM
    row = lambda i: (i, 0)
    return pl.pallas_call(
        functools.partial(_ffn_kernel, tf=256, tn=256, final=final),
        grid=(TOKENS // TM,),
        in_specs=[
            pl.BlockSpec((TM, D_MODEL), row),
            pl.BlockSpec((None, None, 6, D_MODEL), lambda i: (layer, i // steps_per_batch, 0, 0)),
            pl.BlockSpec((None, 1, D_MODEL), lambda i: (layer, 0, 0)),
            pl.BlockSpec((1, D_MODEL), lambda i: (0, 0)),
            pl.BlockSpec((None, D_MODEL, 2 * FFN_HIDDEN), lambda i: (layer, 0, 0)),
            pl.BlockSpec((None, FFN_HIDDEN, D_MODEL), lambda i: (layer, 0, 0)),
        ],
        out_specs=pl.BlockSpec((TM, D_MODEL), row),
        out_shape=jax.ShapeDtypeStruct((TOKENS, D_MODEL), F32),
        scratch_shapes=[pltpu.VMEM((TM, FFN_HIDDEN), BF16)],
        compiler_params=_params(1),
        name="ffn",
    )(x, mod, gain, final_gain, w_ffn_in, w_ffn_out)


def kernel(x, c, norm_mix_g, norm_ffn_g, w_ada, b_ada, w_in, b_forget, sinks, rel_bias,
           w_branch, w_out, w_ffn_in, w_ffn_out, final_norm_g):
    scale = HEAD_DIM ** -0.5
    q_scale = jnp.ones((PROJ_COLS,), F32)
    for blk, mult in ((QA_BLK, scale), (QB_BLK, scale * LOG2E), (QC_BLK, scale)):
        q_scale = q_scale.at[blk * LANES:(blk + 4) * LANES].set(mult)
    fb0 = 2304
    a_cols = (A_HEADS + 2 * A_KV_HEADS) * HEAD_DIM
    w_qkv = jnp.concatenate([w_in[:, :, a_cols:fb0], w_in[:, :, fb0 + B_HEADS:fb0 + B_HEADS + 1536],
                             w_in[:, :, :a_cols]], axis=-1)
    w_qkv = (w_qkv * q_scale).astype(BF16)
    w_fb = jnp.pad(w_in[:, :, fb0:fb0 + B_HEADS], ((0, 0), (0, 0), (0, LANES - B_HEADS))).astype(BF16)
    w_gate = w_in[:, :, fb0 + B_HEADS + 1536:].astype(BF16)
    w_branch_b = w_branch.astype(BF16)
    w_out_b = w_out.astype(BF16)
    w_ffn_in_b = w_ffn_in.astype(BF16)
    w_ffn_out_b = w_ffn_out.astype(BF16)
    b_forget_rows = jnp.pad(b_forget, ((0, 0), (0, LANES - B_HEADS))).reshape(DEPTH, 1, LANES)
    gain_mix = norm_mix_g.reshape(DEPTH, 1, D_MODEL)
    gain_ffn = norm_ffn_g.reshape(DEPTH, 1, D_MODEL)
    gain_final = final_norm_g.reshape(1, D_MODEL)

    mod = _ada_mod(c, w_ada, b_ada)
    bias_c = _rel_bias(rel_bias)

    xt = x.reshape(TOKENS, D_MODEL)
    for layer in range(DEPTH):
        proj, fb = _in_proj(xt, mod, gain_mix, w_qkv, w_fb, layer)
        ccol, crow = _forget_cumsum(fb, b_forget_rows[layer])
        o_a = _swa_attention(proj, sinks, layer)
        o_b = _fox_attention(proj, ccol, crow)
        o_c = _chunked_attention(proj, bias_c, layer)
        xt = _merge(xt, mod, gain_mix, o_a, o_b, o_c, w_branch_b, w_gate, w_out_b, layer)
        xt = _ffn(xt, mod, gain_ffn, gain_final, w_ffn_in_b, w_ffn_out_b, layer, layer == DEPTH - 1)
    return xt.reshape(BATCH, SEQ, D_MODEL)
```

```python
import functools

import jax
import jax.numpy as jnp
from jax import lax
from jax.experimental import pallas as pl
from jax.experimental.pallas import tpu as pltpu

F32 = jnp.float32
BF16 = jnp.bfloat16

D_MODEL = 1024
BATCH = 8
SEQ = 2048
TOKENS = BATCH * SEQ
DEPTH = 2
CHUNK = 64
HEAD_DIM = 64
EPS = 1e-6
NEG_INF = -1e30
LOG2E = 1.4426950408889634

A_HEADS = 8
A_KV_HEADS = 2
A_GROUP = A_HEADS // A_KV_HEADS
A_PREV = 2
A_BAND = (A_PREV + 1) * CHUNK
B_HEADS = 8
C_HEADS = 8
C_PREV = 8
C_PAD = C_PREV * CHUNK
C_BAND = (C_PREV + 1) * CHUNK
REL_CLIP = 128
N_REL = 2 * REL_CLIP + 1
BRANCH_WIDTH = 512
FFN_HIDDEN = 2816
N_CHUNKS = SEQ // CHUNK
N_IN_COLS = 6920

LANES = 128
N_PAIRS = 4

PROJ_COLS = 3840
QB_BLK, KB_BLK, VB_BLK = 0, 4, 8
QC_BLK, KC_BLK, VC_BLK = 12, 16, 20
QA_BLK, KA_BLK, VA_BLK = 24, 28, 29

TM = 512
FOX_TQ = 512
FOX_TK = 256
FOX_VROWS = HEAD_DIM + 16
FOX_PAIRS = 2
CUM_BLK = 256

VMEM_LIMIT = 56 * 1024 * 1024


def _params(n_axes):
    return pltpu.CompilerParams(dimension_semantics=("arbitrary",) * n_axes,
                                vmem_limit_bytes=VMEM_LIMIT)


def _rms_mod(x, g, shift, scale):
    ms = jnp.mean(x * x, axis=-1, keepdims=True)
    y = x * lax.rsqrt(ms + EPS) * g
    return y * (1.0 + scale) + shift


def _row_start(index, size):
    if isinstance(index, int):
        return index * size
    return pl.multiple_of(index * size, size)


def _lane_half(shape):
    return lax.broadcasted_iota(jnp.int32, shape, len(shape) - 1) < HEAD_DIM


def _ada_kernel(c_ref, w_ref, b_ref, o_ref):
    c = c_ref[...]
    cond = c * jax.nn.sigmoid(c)
    o_ref[...] = jnp.dot(cond.astype(BF16), w_ref[...].astype(BF16),
                         preferred_element_type=F32) + b_ref[...]


def _ada_mod(c, w_ada, b_ada):
    n_blk = 6
    out = pl.pallas_call(
        _ada_kernel,
        grid=(DEPTH, n_blk),
        in_specs=[
            pl.BlockSpec((BATCH, D_MODEL), lambda l, j: (0, 0)),
            pl.BlockSpec((None, D_MODEL, D_MODEL), lambda l, j: (l, 0, j)),
            pl.BlockSpec((None, 1, D_MODEL), lambda l, j: (l, 0, j)),
        ],
        out_specs=pl.BlockSpec((None, BATCH, D_MODEL), lambda l, j: (l, 0, j)),
        out_shape=jax.ShapeDtypeStruct((DEPTH, BATCH, n_blk * D_MODEL), F32),
        compiler_params=_params(2),
        name="ada_mod",
    )(c, w_ada, b_ada.reshape(DEPTH, 1, n_blk * D_MODEL))
    return out.reshape(DEPTH, BATCH, n_blk, D_MODEL)


def _relbias_kernel(rev_ref, o_ref):
    near = C_PAD - REL_CLIP
    width = 2 * REL_CLIP
    q = lax.broadcasted_iota(jnp.int32, (CHUNK, width), 0)
    c = lax.broadcasted_iota(jnp.int32, (CHUNK, width), 1)
    for h in range(C_HEADS):
        rev = rev_ref[h:h + 1, :]
        far = rev[:, 0:1]
        rolled = pltpu.roll(jnp.broadcast_to(rev, (CHUNK, width)), 0, 1, stride=1, stride_axis=0)
        o_ref[h, :, 0:near] = jnp.broadcast_to(far, (CHUNK, near))
        o_ref[h, :, near:C_BAND] = jnp.where(c >= q, rolled, far)[:, :C_BAND - near]


def _rel_bias(rel_bias):
    rev = rel_bias[:, :, ::-1][:, :, :2 * REL_CLIP]
    out = pl.pallas_call(
        _relbias_kernel,
        grid=(DEPTH,),
        in_specs=[pl.BlockSpec((None, C_HEADS, 2 * REL_CLIP), lambda l: (l, 0, 0))],
        out_specs=pl.BlockSpec((None, C_HEADS, CHUNK, C_BAND), lambda l: (l, 0, 0, 0)),
        out_shape=jax.ShapeDtypeStruct((DEPTH, C_HEADS, CHUNK, C_BAND), F32),
        compiler_params=_params(1),
        name="rel_bias",
    )(rev)
    return out.reshape(DEPTH, N_PAIRS, 2 * CHUNK, C_BAND)


IN_A0, IN_B0, IN_FB0, IN_C0, IN_GATE0 = 0, 768, 2304, 2312, 3848
PACK_ROWS = 384


def _pack_kernel(wt_ref, o_ref, *, region_starts, region_steps, q_scales):
    j = pl.program_id(1)
    w = wt_ref[0]
    first_step = 0
    row = lax.broadcasted_iota(jnp.int32, (PACK_ROWS, 1), 0)
    scale = jnp.ones((PACK_ROWS, 1), F32)
    for steps, q_scale in zip(region_steps, q_scales):
        if q_scale is not None:
            in_region = (j >= first_step) & (j < first_step + steps)
            col_in_region = (j - first_step) * PACK_ROWS + row
            scale = jnp.where(in_region & (col_in_region < A_HEADS * HEAD_DIM), q_scale, scale)
        first_step += steps
    o_ref[...] = (w * scale).T.astype(BF16)


def _pack_w_in(w_in_t, region_starts, region_widths, q_scales):
    region_steps = tuple(w // PACK_ROWS for w in region_widths)

    def source_row(j):
        row = jnp.int32(0)
        first_step = 0
        for start, steps in zip(region_starts, region_steps):
            row = jnp.where(j >= first_step, start + (j - first_step) * PACK_ROWS, row)
            first_step += steps
        return pl.multiple_of(row, 8)

    return pl.pallas_call(
        functools.partial(_pack_kernel, region_starts=region_starts, region_steps=region_steps, q_scales=q_scales),
        grid=(DEPTH, sum(region_steps)),
        in_specs=[pl.BlockSpec((pl.Element(1), pl.Element(PACK_ROWS), pl.Element(D_MODEL)),
                               lambda l, j: (l, source_row(j), 0))],
        out_specs=pl.BlockSpec((None, D_MODEL, PACK_ROWS), lambda l, j: (l, 0, j)),
        out_shape=jax.ShapeDtypeStruct((DEPTH, D_MODEL, sum(region_widths)), BF16),
        compiler_params=_params(2),
        name="pack_w_in",
    )(w_in_t)


def _inproj_kernel(x_ref, mod_ref, g_ref, w_ref, wfb_ref, proj_ref, fb_ref, *, tn):
    h = _rms_mod(x_ref[...], g_ref[...], mod_ref[0:1, :], mod_ref[1:2, :]).astype(BF16)
    for j in range(PROJ_COLS // tn):
        sl = slice(j * tn, (j + 1) * tn)
        proj_ref[:, sl] = jnp.dot(h, w_ref[:, sl], preferred_element_type=F32).astype(BF16)
    fb_ref[...] = jnp.dot(h, wfb_ref[...], preferred_element_type=F32)


def _in_proj(x, mod, gain, w_qkv, w_fb, layer):
    tn = 768
    steps_per_batch = SEQ // TM
    return pl.pallas_call(
        functools.partial(_inproj_kernel, tn=tn),
        grid=(TOKENS // TM,),
        in_specs=[
            pl.BlockSpec((TM, D_MODEL), lambda i: (i, 0)),
            pl.BlockSpec((None, None, 6, D_MODEL), lambda i: (layer, i // steps_per_batch, 0, 0)),
            pl.BlockSpec((None, 1, D_MODEL), lambda i: (layer, 0, 0)),
            pl.BlockSpec((None, D_MODEL, PROJ_COLS), lambda i: (layer, 0, 0)),
            pl.BlockSpec((None, D_MODEL, LANES), lambda i: (layer, 0, 0)),
        ],
        out_specs=[
            pl.BlockSpec((TM, PROJ_COLS), lambda i: (i, 0)),
            pl.BlockSpec((TM, LANES), lambda i: (i, 0)),
        ],
        out_shape=[
            jax.ShapeDtypeStruct((TOKENS, PROJ_COLS), BF16),
            jax.ShapeDtypeStruct((TOKENS, LANES), F32),
        ],
        compiler_params=_params(1),
        name="in_proj",
    )(x, mod, gain, w_qkv, w_fb)


def _cumsum_kernel(fb_ref, bias_ref, col_ref, row_ref):
    r = lax.broadcasted_iota(jnp.int32, (CUM_BLK, CUM_BLK), 0)
    c = lax.broadcasted_iota(jnp.int32, (CUM_BLK, CUM_BLK), 1)
    tri = (r >= c).astype(F32)
    carry = jnp.zeros((1, LANES), F32)
    for blk in range(SEQ // CUM_BLK):
        rows = slice(blk * CUM_BLK, (blk + 1) * CUM_BLK)
        z = fb_ref[0, rows, :] + bias_ref[...]
        log_f = jnp.minimum(z, 0.0) - jnp.log1p(jnp.exp(-jnp.abs(z)))
        cum = jnp.dot(tri, log_f, preferred_element_type=F32,
                      precision=lax.Precision.HIGHEST) + carry
        col_ref[0, rows, :] = cum
        row_ref[0, :, rows] = cum.T[:B_HEADS, :]
        carry = cum[CUM_BLK - 1:CUM_BLK, :]


def _forget_cumsum(fb, b_forget_row):
    return pl.pallas_call(
        _cumsum_kernel,
        grid=(BATCH,),
        in_specs=[
            pl.BlockSpec((1, SEQ, LANES), lambda b: (b, 0, 0)),
            pl.BlockSpec((1, LANES), lambda b: (0, 0)),
        ],
        out_specs=[
            pl.BlockSpec((1, SEQ, LANES), lambda b: (b, 0, 0)),
            pl.BlockSpec((1, B_HEADS, SEQ), lambda b: (b, 0, 0)),
        ],
        out_shape=[
            jax.ShapeDtypeStruct((BATCH, SEQ, LANES), F32),
            jax.ShapeDtypeStruct((BATCH, B_HEADS, SEQ), F32),
        ],
        compiler_params=_params(1),
        name="forget_cumsum",
    )(fb.reshape(BATCH, SEQ, LANES), b_forget_row)


def _fox_kernel(q_ref, k_ref, v_ref, ccol_ref, crow_ref, o_ref, vt_ref, ckb_ref,
                s0_ref, s1_ref, p0_ref, p1_ref, acc_ref, m_ref, a0_ref, a1_ref, cq_ref, qm_ref):
    tq, tk = FOX_TQ, FOX_TK
    heads = 2 * FOX_PAIRS
    grp = pl.program_id(1)
    qi = pl.program_id(2)
    first = _lane_half((1, LANES))

    @pl.when(qi == 0)
    def _():
        head_lane = lax.broadcasted_iota(jnp.int32, (1, LANES), 1)
        ones_row = (lax.broadcasted_iota(jnp.int32, (FOX_VROWS - HEAD_DIM, tk), 0) == 0).astype(BF16)
        for jt in range(SEQ // tk):
            rows = slice(jt * tk, (jt + 1) * tk)
            v_t = v_ref[rows, :].astype(F32).T.astype(BF16)
            ccol = ccol_ref[0, rows, :]
            for h in range(heads):
                vt_ref[jt, h * FOX_VROWS:h * FOX_VROWS + HEAD_DIM, :] = v_t[h * HEAD_DIM:(h + 1) * HEAD_DIM]
                vt_ref[jt, h * FOX_VROWS + HEAD_DIM:(h + 1) * FOX_VROWS, :] = ones_row
                col = jnp.sum(jnp.where(head_lane == grp * heads + h, ccol, 0.0), axis=1, keepdims=True)
                ckb_ref[h, rows, :] = jnp.broadcast_to(col * LOG2E, (tk, LANES))

    for h in range(heads):
        q2 = q_ref[:, (h // 2) * LANES:(h // 2 + 1) * LANES]
        qm_ref[h] = jnp.where(first if h % 2 == 0 else ~first, q2, jnp.zeros_like(q2))
        cq_ref[h] = crow_ref[0, grp * heads + h, pl.ds(qi, 1), :] * LOG2E

    def scores(j, s_ref):
        row0 = _row_start(j, tk)
        for h in range(heads):
            kj = k_ref[pl.ds(row0, tk), (h // 2) * LANES:(h // 2 + 1) * LANES]
            s = lax.dot_general(kj, qm_ref[h], (((1,), (1,)), ((), ())), preferred_element_type=F32)
            ck = ckb_ref[h, pl.ds(row0, tk), :]
            s_ref[h] = s - jnp.concatenate([ck] * (tq // LANES), axis=1)

    def softmax(s_ref, p_ref, a_ref, mask):
        for h in range(heads):
            s = s_ref[h]
            if mask is not None:
                s = jnp.where(mask, s, NEG_INF)
            m_old = m_ref[h]
            cq = cq_ref[h]
            m_new = jnp.maximum(m_old, jnp.max(s, axis=0, keepdims=True) + cq)
            p_ref[h] = jnp.exp2(s + (cq - m_new)).astype(BF16)
            m_ref[h] = m_new
            a_ref[h] = jnp.exp2(m_old - m_new)

    def accumulate(j, p_ref, a_ref):
        tile = jnp.maximum(j, 0)
        for h in range(heads):
            vt = vt_ref[tile, h * FOX_VROWS:(h + 1) * FOX_VROWS, :]
            pv = jnp.dot(vt, p_ref[h], preferred_element_type=F32)
            acc_ref[h] = a_ref[h] * acc_ref[h] + pv

    def trip(i, mask_even, mask_odd, last):
        cur = i & 1
        even = 2 * i
        scores(even + 1, s1_ref)
        accumulate(even - 1, p1_ref.at[1 - cur], a1_ref.at[1 - cur])
        softmax(s0_ref.at[cur], p0_ref, a0_ref, mask_even)
        if not last:
            scores(even + 2, s0_ref.at[1 - cur])
        accumulate(even, p0_ref, a0_ref)
        softmax(s1_ref, p1_ref.at[cur], a1_ref.at[cur], mask_odd)
        if last:
            accumulate(even + 1, p1_ref.at[cur], a1_ref.at[cur])

    def body(i, carry):
        trip(i, None, None, last=False)
        return carry

    acc_ref[...] = jnp.zeros(acc_ref.shape, F32)
    m_ref[...] = jnp.full(m_ref.shape, NEG_INF, F32)
    p1_ref[1] = jnp.zeros(p1_ref.shape[1:], BF16)
    a1_ref[1] = jnp.ones(a1_ref.shape[1:], F32)
    scores(0, s0_ref.at[0])
    lax.fori_loop(0, qi, body, 0)
    kpos = lax.broadcasted_iota(jnp.int32, (tk, tq), 0)
    qpos = lax.broadcasted_iota(jnp.int32, (tk, tq), 1)
    trip(qi, kpos <= qpos, kpos + tk <= qpos, last=True)
    out_t = jnp.concatenate([acc_ref[h, :HEAD_DIM, :] / acc_ref[h, HEAD_DIM:HEAD_DIM + 1, :]
                             for h in range(heads)], axis=0)
    o_ref[...] = out_t.T.astype(BF16)


def _fox_attention(proj, ccol, crow):
    tq, tk = FOX_TQ, FOX_TK
    nq = SEQ // tq
    heads = 2 * FOX_PAIRS
    width = FOX_PAIRS * LANES
    return pl.pallas_call(
        _fox_kernel,
        grid=(BATCH, N_PAIRS // FOX_PAIRS, nq),
        in_specs=[
            pl.BlockSpec((tq, width), lambda b, g, i: (b * nq + i, QB_BLK // FOX_PAIRS + g)),
            pl.BlockSpec((SEQ, width), lambda b, g, i: (b, KB_BLK // FOX_PAIRS + g)),
            pl.BlockSpec((SEQ, width), lambda b, g, i: (b, VB_BLK // FOX_PAIRS + g)),
            pl.BlockSpec((1, SEQ, LANES), lambda b, g, i: (b, 0, 0)),
            pl.BlockSpec((1, B_HEADS, nq, tq), lambda b, g, i: (b, 0, 0, 0)),
        ],
        out_specs=pl.BlockSpec((tq, width), lambda b, g, i: (b * nq + i, g)),
        out_shape=jax.ShapeDtypeStruct((TOKENS, BRANCH_WIDTH), BF16),
        scratch_shapes=[pltpu.VMEM((SEQ // tk, heads * FOX_VROWS, tk), BF16),
                        pltpu.VMEM((heads, SEQ, LANES), F32),
                        pltpu.VMEM((2, heads, tk, tq), F32), pltpu.VMEM((heads, tk, tq), F32),
                        pltpu.VMEM((heads, tk, tq), BF16), pltpu.VMEM((2, heads, tk, tq), BF16),
                        pltpu.VMEM((heads, FOX_VROWS, tq), F32),
                        pltpu.VMEM((heads, 1, tq), F32),
                        pltpu.VMEM((heads, 1, tq), F32),
                        pltpu.VMEM((2, heads, 1, tq), F32),
                        pltpu.VMEM((heads, 1, tq), F32),
                        pltpu.VMEM((heads, tq, LANES), BF16)],
        compiler_params=_params(3),
        name="fox_attention",
    )(proj, proj, proj, ccol, crow.reshape(BATCH, B_HEADS, nq, tq))


def _swa_kernel(sink_ref, q_ref, k_ref, v_ref, o_ref, kd_ref, vd_ref,
                s0_ref, s1_ref, p0_ref, p1_ref, d0_ref, d1_ref, *, layer):
    rows = A_GROUP * CHUNK
    first = _lane_half((1, LANES))
    grp = lax.broadcasted_iota(jnp.int32, (rows, 1), 0) // CHUNK
    qi = lax.broadcasted_iota(jnp.int32, (rows, A_BAND), 0) % CHUNK
    si = lax.broadcasted_iota(jnp.int32, (rows, A_BAND), 1)
    sel_r = lax.broadcasted_iota(jnp.int32, (LANES, LANES), 0)
    sel_c = lax.broadcasted_iota(jnp.int32, (LANES, LANES), 1)

    slopes, sinks = [], []
    for kvh in range(A_KV_HEADS):
        sel = (sel_r == kvh * HEAD_DIM + sel_c % HEAD_DIM).astype(BF16)
        kd_ref[kvh] = jnp.dot(k_ref[...], sel, preferred_element_type=F32).astype(BF16)
        vd_ref[kvh] = jnp.dot(v_ref[...], sel, preferred_element_type=F32).astype(BF16)
        slope = jnp.zeros((rows, 1), F32)
        sink = jnp.zeros((rows, 1), F32)
        for g in range(A_GROUP):
            head = kvh * A_GROUP + g
            slope = jnp.where(grp == g, 2.0 ** -(head + 1), slope)
            sink = jnp.where(grp == g, sink_ref[layer, head], sink)
        slopes.append(slope)
        sinks.append(sink)

    def band_start(n):
        if isinstance(n, int):
            return max(n - A_PREV, 0) * CHUNK
        return _row_start(jnp.maximum(n - A_PREV, 0), CHUNK)

    def scores(n, s_ref, alibi):
        q0 = _row_start(n, CHUNK)
        for kvh in range(A_KV_HEADS):
            parts = []
            for g in range(A_GROUP):
                head = kvh * A_GROUP + g
                blk = q_ref[pl.ds(q0, CHUNK), (head // 2) * LANES:(head // 2 + 1) * LANES]
                parts.append(jnp.where(first if head % 2 == 0 else ~first, blk, jnp.zeros_like(blk)))
            qs = jnp.concatenate(parts, axis=0)
            kb = kd_ref[kvh, pl.ds(band_start(n), A_BAND), :]
            s = lax.dot_general(qs, kb, (((1,), (1,)), ((), ())), preferred_element_type=F32)
            s_ref[kvh] = s + alibi[kvh]

    def softmax(s_ref, p_ref, d_ref, valid):
        for kvh in range(A_KV_HEADS):
            s = s_ref[kvh] if valid is None else jnp.where(valid, s_ref[kvh], NEG_INF)
            m = jnp.maximum(jnp.max(s, axis=1, keepdims=True), sinks[kvh])
            p = jnp.exp(s - m)
            denom = jnp.sum(p, axis=1, keepdims=True) + jnp.exp(sinks[kvh] - m)
            d_ref[kvh] = jnp.broadcast_to(denom, (rows, LANES))
            p_ref[kvh] = p.astype(BF16)

    def emit(n, p_ref, d_ref):
        q0 = _row_start(n, CHUNK)
        for kvh in range(A_KV_HEADS):
            vb = vd_ref[kvh, pl.ds(band_start(n), A_BAND), :]
            r = jnp.dot(p_ref[kvh], vb, preferred_element_type=F32) / d_ref[kvh]
            for pr in range(A_GROUP // 2):
                even = r[(2 * pr) * CHUNK:(2 * pr + 1) * CHUNK]
                odd = r[(2 * pr + 1) * CHUNK:(2 * pr + 2) * CHUNK]
                col = (kvh * (A_GROUP // 2) + pr) * LANES
                o_ref[pl.ds(q0, CHUNK), col:col + LANES] = jnp.where(first, even, odd).astype(BF16)

    def alibi_for(dist):
        return tuple(-slope * jnp.abs(dist).astype(F32) for slope in slopes)

    alibi = alibi_for(A_PREV * CHUNK + qi - si)
    def trip(i, carry=0, *, first_trip=False, last_trip=False):
        cur = i & 1
        even = 2 * i
        if first_trip:
            scores(1, s1_ref, alibi_for(CHUNK + qi - si))
            softmax(s0_ref.at[cur], p0_ref, d0_ref, si // CHUNK <= 0)
        else:
            scores(even + 1, s1_ref, alibi)
            emit(even - 1, p1_ref.at[1 - cur], d1_ref.at[1 - cur])
            softmax(s0_ref.at[cur], p0_ref, d0_ref, None)
        if not last_trip:
            scores(even + 2, s0_ref.at[1 - cur], alibi)
        emit(even, p0_ref, d0_ref)
        softmax(s1_ref, p1_ref.at[cur], d1_ref.at[cur], si // CHUNK <= 1 if first_trip else None)
        if last_trip:
            emit(even + 1, p1_ref.at[cur], d1_ref.at[cur])
        return carry

    scores(0, s0_ref.at[0], alibi_for(qi - si))
    trip(0, first_trip=True)
    lax.fori_loop(1, N_CHUNKS // 2 - 1, trip, 0)
    trip(N_CHUNKS // 2 - 1, last_trip=True)


def _swa_attention(proj, sinks, layer):
    return pl.pallas_call(
        functools.partial(_swa_kernel, layer=layer),
        grid=(BATCH,),
        in_specs=[
            pl.BlockSpec(memory_space=pltpu.SMEM),
            pl.BlockSpec((SEQ, A_HEADS * HEAD_DIM), lambda b: (b, QA_BLK // N_PAIRS)),
            pl.BlockSpec((SEQ, LANES), lambda b: (b, KA_BLK)),
            pl.BlockSpec((SEQ, LANES), lambda b: (b, VA_BLK)),
        ],
        out_specs=pl.BlockSpec((SEQ, BRANCH_WIDTH), lambda b: (b, 0)),
        out_shape=jax.ShapeDtypeStruct((TOKENS, BRANCH_WIDTH), BF16),
        scratch_shapes=[pltpu.VMEM((A_KV_HEADS, SEQ, LANES), BF16),
                        pltpu.VMEM((A_KV_HEADS, SEQ, LANES), BF16),
                        pltpu.VMEM((2, A_KV_HEADS, A_GROUP * CHUNK, A_BAND), F32),
                        pltpu.VMEM((A_KV_HEADS, A_GROUP * CHUNK, A_BAND), F32),
                        pltpu.VMEM((A_KV_HEADS, A_GROUP * CHUNK, A_BAND), BF16),
                        pltpu.VMEM((2, A_KV_HEADS, A_GROUP * CHUNK, A_BAND), BF16),
                        pltpu.VMEM((A_KV_HEADS, A_GROUP * CHUNK, LANES), F32),
                        pltpu.VMEM((2, A_KV_HEADS, A_GROUP * CHUNK, LANES), F32)],
        compiler_params=_params(1),
        name="swa_attention",
    )(sinks, proj, proj, proj)


def _chunked_kernel(q_ref, k_ref, v_ref, bias_ref, o_ref, kp_ref, vp_ref,
                    s0_ref, s1_ref, p0_ref, p1_ref, d0_ref, d1_ref):
    first = _lane_half((1, LANES))
    width = N_PAIRS * LANES
    kp_ref[0:C_PAD, :] = jnp.zeros((C_PAD, width), BF16)
    vp_ref[0:C_PAD, :] = jnp.zeros((C_PAD, width), BF16)
    kp_ref[C_PAD:, :] = k_ref[...]
    vp_ref[C_PAD:, :] = v_ref[...]
    si = lax.broadcasted_iota(jnp.int32, (2 * CHUNK, C_BAND), 1)

    def scores(n, s_ref):
        q0 = _row_start(n, CHUNK)
        for pair in range(N_PAIRS):
            cols = slice(pair * LANES, (pair + 1) * LANES)
            q2 = q_ref[pl.ds(q0, CHUNK), cols]
            zero = jnp.zeros_like(q2)
            qs = jnp.concatenate([jnp.where(first, q2, zero), jnp.where(first, zero, q2)], axis=0)
            kb = kp_ref[pl.ds(q0, C_BAND), cols]
            s = lax.dot_general(qs, kb, (((1,), (1,)), ((), ())), preferred_element_type=F32)
            s_ref[pair] = s + bias_ref[pair]

    def softmax(n, s_ref, p_ref, d_ref, masked):
        for pair in range(N_PAIRS):
            s = s_ref[pair]
            if masked:
                s = jnp.where(n * CHUNK + si >= C_PAD, s, NEG_INF)
            p = jnp.exp(s - jnp.max(s, axis=1, keepdims=True))
            d_ref[pair] = jnp.sum(p, axis=1, keepdims=True)
            p_ref[pair] = p.astype(BF16)

    def emit(n, p_ref, d_ref):
        q0 = _row_start(n, CHUNK)
        for pair in range(N_PAIRS):
            cols = slice(pair * LANES, (pair + 1) * LANES)
            vb = vp_ref[pl.ds(q0, C_BAND), cols]
            r = jnp.dot(p_ref[pair], vb, preferred_element_type=F32) / d_ref[pair]
            o_ref[pl.ds(q0, CHUNK), cols] = jnp.where(first, r[:CHUNK], r[CHUNK:]).astype(BF16)

    def trip(i, carry=0, *, masked, first_trip=False, last_trip=False):
        cur = i & 1
        even = 2 * i
        scores(even + 1, s1_ref)
        if not first_trip:
            emit(even - 1, p1_ref.at[1 - cur], d1_ref.at[1 - cur])
        softmax(even, s0_ref.at[cur], p0_ref, d0_ref, masked)
        if not last_trip:
            scores(even + 2, s0_ref.at[1 - cur])
        emit(even, p0_ref, d0_ref)
        softmax(even + 1, s1_ref, p1_ref.at[cur], d1_ref.at[cur], masked)
        if last_trip:
            emit(even + 1, p1_ref.at[cur], d1_ref.at[cur])
        return carry

    masked_trips = C_PREV // 2
    scores(0, s0_ref.at[0])
    trip(0, masked=True, first_trip=True)
    lax.fori_loop(1, masked_trips, functools.partial(trip, masked=True), 0)
    lax.fori_loop(masked_trips, N_CHUNKS // 2 - 1, functools.partial(trip, masked=False), 0)
    trip(N_CHUNKS // 2 - 1, masked=False, last_trip=True)


def _chunked_attention(proj, bias, layer):
    width = N_PAIRS * LANES
    return pl.pallas_call(
        _chunked_kernel,
        grid=(BATCH,),
        in_specs=[
            pl.BlockSpec((SEQ, width), lambda b: (b, QC_BLK // N_PAIRS)),
            pl.BlockSpec((SEQ, width), lambda b: (b, KC_BLK // N_PAIRS)),
            pl.BlockSpec((SEQ, width), lambda b: (b, VC_BLK // N_PAIRS)),
            pl.BlockSpec((None, N_PAIRS, 2 * CHUNK, C_BAND), lambda b: (layer, 0, 0, 0)),
        ],
        out_specs=pl.BlockSpec((SEQ, width), lambda b: (b, 0)),
        out_shape=jax.ShapeDtypeStruct((TOKENS, BRANCH_WIDTH), BF16),
        scratch_shapes=[pltpu.VMEM((C_PAD + SEQ, width), BF16), pltpu.VMEM((C_PAD + SEQ, width), BF16),
                        pltpu.VMEM((2, N_PAIRS, 2 * CHUNK, C_BAND), F32),
                        pltpu.VMEM((N_PAIRS, 2 * CHUNK, C_BAND), F32),
                        pltpu.VMEM((N_PAIRS, 2 * CHUNK, C_BAND), BF16),
                        pltpu.VMEM((2, N_PAIRS, 2 * CHUNK, C_BAND), BF16),
                        pltpu.VMEM((N_PAIRS, 2 * CHUNK, 1), F32),
                        pltpu.VMEM((2, N_PAIRS, 2 * CHUNK, 1), F32)],
        compiler_params=_params(1),
        name="chunked_attention",
    )(proj, proj, proj, bias)


def _merge_kernel(x_ref, mod_ref, g_ref, oa_ref, ob_ref, oc_ref, wb_ref, wg_ref, wo_ref, out_ref,
                  merged_ref, *, tn):
    h = _rms_mod(x_ref[...], g_ref[...], mod_ref[0:1, :], mod_ref[1:2, :]).astype(BF16)
    branches = (oa_ref[...], ob_ref[...], oc_ref[...])
    for n in range(D_MODEL // tn):
        acc = None
        for k, o in enumerate(branches):
            y = jnp.dot(o, wb_ref[k, :, n * tn:(n + 1) * tn], preferred_element_type=F32)
            gate = jnp.dot(h, wg_ref[:, k * D_MODEL + n * tn:k * D_MODEL + (n + 1) * tn],
                           preferred_element_type=F32)
            term = jax.nn.sigmoid(gate) * y
            acc = term if acc is None else acc + term
        merged_ref[:, n * tn:(n + 1) * tn] = acc.astype(BF16)
    merged = merged_ref[...]
    for n in range(D_MODEL // tn):
        sl = slice(n * tn, (n + 1) * tn)
        out = jnp.dot(merged, wo_ref[:, sl], preferred_element_type=F32)
        out_ref[:, sl] = x_ref[:, sl] + mod_ref[2:3, sl] * out


def _merge(x, mod, gain, o_a, o_b, o_c, w_branch, w_gate, w_out, layer):
    steps_per_batch = SEQ // TM
    row = lambda i: (i, 0)
    return pl.pallas_call(
        functools.partial(_merge_kernel, tn=256),
        grid=(TOKENS // TM,),
        in_specs=[
            pl.BlockSpec((TM, D_MODEL), row),
            pl.BlockSpec((None, None, 6, D_MODEL), lambda i: (layer, i // steps_per_batch, 0, 0)),
            pl.BlockSpec((None, 1, D_MODEL), lambda i: (layer, 0, 0)),
            pl.BlockSpec((TM, BRANCH_WIDTH), row),
            pl.BlockSpec((TM, BRANCH_WIDTH), row),
            pl.BlockSpec((TM, BRANCH_WIDTH), row),
            pl.BlockSpec((None, 3, BRANCH_WIDTH, D_MODEL), lambda i: (layer, 0, 0, 0)),
            pl.BlockSpec((None, D_MODEL, 3 * D_MODEL), lambda i: (layer, 0, 0)),
            pl.BlockSpec((None, D_MODEL, D_MODEL), lambda i: (layer, 0, 0)),
        ],
        out_specs=pl.BlockSpec((TM, D_MODEL), row),
        out_shape=jax.ShapeDtypeStruct((TOKENS, D_MODEL), F32),
        scratch_shapes=[pltpu.VMEM((TM, D_MODEL), BF16)],
        compiler_params=_params(1),
        name="merge_out",
    )(x, mod, gain, o_a, o_b, o_c, w_branch, w_gate, w_out)


def _ffn_kernel(x_ref, mod_ref, g_ref, gf_ref, wi_ref, wo_ref, out_ref, act_ref, *, tf, tn, final):
    h = _rms_mod(x_ref[...], g_ref[...], mod_ref[3:4, :], mod_ref[4:5, :]).astype(BF16)
    for c in range(FFN_HIDDEN // tf):
        gate = jnp.dot(h, wi_ref[:, c * tf:(c + 1) * tf], preferred_element_type=F32)
        up = jnp.dot(h, wi_ref[:, FFN_HIDDEN + c * tf:FFN_HIDDEN + (c + 1) * tf],
                     preferred_element_type=F32)
        act_ref[:, c * tf:(c + 1) * tf] = (gate * jax.nn.sigmoid(gate) * up).astype(BF16)
    act = act_ref[...]
    for n in range(D_MODEL // tn):
        sl = slice(n * tn, (n + 1) * tn)
        out = jnp.dot(act, wo_ref[:, sl], preferred_element_type=F32)
        out_ref[:, sl] = x_ref[:, sl] + mod_ref[5:6, sl] * out
    if final:
        y = out_ref[...]
        ms = jnp.mean(y * y, axis=-1, keepdims=True)
        out_ref[...] = y * lax.rsqrt(ms + EPS) * gf_ref[...]


def _ffn(x, mod, gain, final_gain, w_ffn_in, w_ffn_out, layer, final):
    steps_per_batch = SEQ // TM
    row = lambda i: (i, 0)
    return pl.pallas_call(
        functools.partial(_ffn_kernel, tf=256, tn=256, final=final),
        grid=(TOKENS // TM,),
        in_specs=[
            pl.BlockSpec((TM, D_MODEL), row),
            pl.BlockSpec((None, None, 6, D_MODEL), lambda i: (layer, i // steps_per_batch, 0, 0)),
            pl.BlockSpec((None, 1, D_MODEL), lambda i: (layer, 0, 0)),
            pl.BlockSpec((1, D_MODEL), lambda i: (0, 0)),
            pl.BlockSpec((None, D_MODEL, 2 * FFN_HIDDEN), lambda i: (layer, 0, 0)),
            pl.BlockSpec((None, FFN_HIDDEN, D_MODEL), lambda i: (layer, 0, 0)),
        ],
        out_specs=pl.BlockSpec((TM, D_MODEL), row),
        out_shape=jax.ShapeDtypeStruct((TOKENS, D_MODEL), F32),
        scratch_shapes=[pltpu.VMEM((TM, FFN_HIDDEN), BF16)],
        compiler_params=_params(1),
        name="ffn",
    )(x, mod, gain, final_gain, w_ffn_in, w_ffn_out)


def kernel(x, c, norm_mix_g, norm_ffn_g, w_ada, b_ada, w_in, b_forget, sinks, rel_bias,
           w_branch, w_out, w_ffn_in, w_ffn_out, final_norm_g):
    scale = HEAD_DIM ** -0.5
    w_in_t = jnp.swapaxes(w_in, 1, 2)
    w_qkv = _pack_w_in(w_in_t, (IN_B0, IN_C0, IN_A0), (IN_FB0 - IN_B0, IN_GATE0 - IN_C0, IN_B0 - IN_A0),
                       (scale * LOG2E, scale, scale))
    w_gate = _pack_w_in(w_in_t, (IN_GATE0,), (N_IN_COLS - IN_GATE0,), (None,))
    w_fb = jnp.pad(w_in[:, :, IN_FB0:IN_FB0 + B_HEADS], ((0, 0), (0, 0), (0, LANES - B_HEADS))).astype(BF16)
    w_branch_b = w_branch.astype(BF16)
    w_out_b = w_out.astype(BF16)
    w_ffn_in_b = w_ffn_in.astype(BF16)
    w_ffn_out_b = w_ffn_out.astype(BF16)
    b_forget_rows = jnp.pad(b_forget, ((0, 0), (0, LANES - B_HEADS))).reshape(DEPTH, 1, LANES)
    gain_mix = norm_mix_g.reshape(DEPTH, 1, D_MODEL)
    gain_ffn = norm_ffn_g.reshape(DEPTH, 1, D_MODEL)
    gain_final = final_norm_g.reshape(1, D_MODEL)

    mod = _ada_mod(c, w_ada, b_ada)
    bias_c = _rel_bias(rel_bias)

    xt = x.reshape(TOKENS, D_MODEL)
    for layer in range(DEPTH):
        proj, fb = _in_proj(xt, mod, gain_mix, w_qkv, w_fb, layer)
        ccol, crow = _forget_cumsum(fb, b_forget_rows[layer])
        o_a = _swa_attention(proj, sinks, layer)
        o_b = _fox_attention(proj, ccol, crow)
        o_c = _chunked_attention(proj, bias_c, layer)
        xt = _merge(xt, mod, gain_mix, o_a, o_b, o_c, w_branch_b, w_gate, w_out_b, layer)
        xt = _ffn(xt, mod, gain_ffn, gain_final, w_ffn_in_b, w_ffn_out_b, layer, layer == DEPTH - 1)
    return xt.reshape(BATCH, SEQ, D_MODEL)
```

```python
import functools

import jax
import jax.numpy as jnp
from jax import lax
from jax.experimental import pallas as pl
from jax.experimental.pallas import tpu as pltpu

F32 = jnp.float32
BF16 = jnp.bfloat16

D_MODEL = 1024
BATCH = 8
SEQ = 2048
TOKENS = BATCH * SEQ
DEPTH = 2
CHUNK = 64
HEAD_DIM = 64
EPS = 1e-6
NEG_INF = -1e30
LOG2E = 1.4426950408889634

A_HEADS = 8
A_KV_HEADS = 2
A_GROUP = A_HEADS // A_KV_HEADS
A_PREV = 2
A_BAND = (A_PREV + 1) * CHUNK
B_HEADS = 8
C_HEADS = 8
C_PREV = 8
C_PAD = C_PREV * CHUNK
C_BAND = (C_PREV + 1) * CHUNK
REL_CLIP = 128
N_REL = 2 * REL_CLIP + 1
BRANCH_WIDTH = 512
FFN_HIDDEN = 2816
N_CHUNKS = SEQ // CHUNK
N_IN_COLS = 6920

LANES = 128
N_PAIRS = 4

PROJ_COLS = 3840
QB_BLK, KB_BLK, VB_BLK = 0, 4, 8
QC_BLK, KC_BLK, VC_BLK = 12, 16, 20
QA_BLK, KA_BLK, VA_BLK = 24, 28, 29

TM = 512
FOX_TQ = 512
FOX_TK = 256
FOX_VROWS = HEAD_DIM + 16
FOX_PAIRS = 2
CUM_BLK = 256

VMEM_LIMIT = 56 * 1024 * 1024


def _params(n_axes):
    return pltpu.CompilerParams(dimension_semantics=("arbitrary",) * n_axes,
                                vmem_limit_bytes=VMEM_LIMIT)


def _rms_mod(x, g, shift, scale):
    ms = jnp.mean(x * x, axis=-1, keepdims=True)
    y = x * lax.rsqrt(ms + EPS) * g
    return y * (1.0 + scale) + shift


def _row_start(index, size):
    if isinstance(index, int):
        return index * size
    return pl.multiple_of(index * size, size)


def _lane_half(shape):
    return lax.broadcasted_iota(jnp.int32, shape, len(shape) - 1) < HEAD_DIM


def _ada_kernel(c_ref, w_ref, b_ref, o_ref):
    c = c_ref[...]
    cond = c * jax.nn.sigmoid(c)
    o_ref[...] = jnp.dot(cond.astype(BF16), w_ref[...].astype(BF16),
                         preferred_element_type=F32) + b_ref[...]


def _ada_mod(c, w_ada, b_ada):
    n_blk = 6
    out = pl.pallas_call(
        _ada_kernel,
        grid=(DEPTH, n_blk),
        in_specs=[
            pl.BlockSpec((BATCH, D_MODEL), lambda l, j: (0, 0)),
            pl.BlockSpec((None, D_MODEL, D_MODEL), lambda l, j: (l, 0, j)),
            pl.BlockSpec((None, 1, D_MODEL), lambda l, j: (l, 0, j)),
        ],
        out_specs=pl.BlockSpec((None, BATCH, D_MODEL), lambda l, j: (l, 0, j)),
        out_shape=jax.ShapeDtypeStruct((DEPTH, BATCH, n_blk * D_MODEL), F32),
        compiler_params=_params(2),
        name="ada_mod",
    )(c, w_ada, b_ada.reshape(DEPTH, 1, n_blk * D_MODEL))
    return out.reshape(DEPTH, BATCH, n_blk, D_MODEL)


def _relbias_kernel(rev_ref, o_ref):
    near = C_PAD - REL_CLIP
    width = 2 * REL_CLIP
    q = lax.broadcasted_iota(jnp.int32, (CHUNK, width), 0)
    c = lax.broadcasted_iota(jnp.int32, (CHUNK, width), 1)
    for h in range(C_HEADS):
        rev = rev_ref[h:h + 1, :]
        far = rev[:, 0:1]
        rolled = pltpu.roll(jnp.broadcast_to(rev, (CHUNK, width)), 0, 1, stride=1, stride_axis=0)
        o_ref[h, :, 0:near] = jnp.broadcast_to(far, (CHUNK, near))
        o_ref[h, :, near:C_BAND] = jnp.where(c >= q, rolled, far)[:, :C_BAND - near]


def _rel_bias(rel_bias):
    rev = rel_bias[:, :, ::-1][:, :, :2 * REL_CLIP]
    out = pl.pallas_call(
        _relbias_kernel,
        grid=(DEPTH,),
        in_specs=[pl.BlockSpec((None, C_HEADS, 2 * REL_CLIP), lambda l: (l, 0, 0))],
        out_specs=pl.BlockSpec((None, C_HEADS, CHUNK, C_BAND), lambda l: (l, 0, 0, 0)),
        out_shape=jax.ShapeDtypeStruct((DEPTH, C_HEADS, CHUNK, C_BAND), F32),
        compiler_params=_params(1),
        name="rel_bias",
    )(rev)
    return out.reshape(DEPTH, N_PAIRS, 2 * CHUNK, C_BAND)


IN_A0, IN_B0, IN_FB0, IN_C0, IN_GATE0 = 0, 768, 2304, 2312, 3848
PACK_ROWS = 384


def _pack_kernel(wt_ref, o_ref, *, region_starts, region_steps, q_scales):
    j = pl.program_id(1)
    w = wt_ref[0]
    first_step = 0
    row = lax.broadcasted_iota(jnp.int32, (PACK_ROWS, 1), 0)
    scale = jnp.ones((PACK_ROWS, 1), F32)
    for steps, q_scale in zip(region_steps, q_scales):
        if q_scale is not None:
            in_region = (j >= first_step) & (j < first_step + steps)
            col_in_region = (j - first_step) * PACK_ROWS + row
            scale = jnp.where(in_region & (col_in_region < A_HEADS * HEAD_DIM), q_scale, scale)
        first_step += steps
    o_ref[...] = (w * scale).T.astype(BF16)


def _pack_w_in(w_in_t, region_starts, region_widths, q_scales):
    region_steps = tuple(w // PACK_ROWS for w in region_widths)

    def source_row(j):
        row = jnp.int32(0)
        first_step = 0
        for start, steps in zip(region_starts, region_steps):
            row = jnp.where(j >= first_step, start + (j - first_step) * PACK_ROWS, row)
            first_step += steps
        return pl.multiple_of(row, 8)

    return pl.pallas_call(
        functools.partial(_pack_kernel, region_starts=region_starts, region_steps=region_steps, q_scales=q_scales),
        grid=(DEPTH, sum(region_steps)),
        in_specs=[pl.BlockSpec((pl.Element(1), pl.Element(PACK_ROWS), pl.Element(D_MODEL)),
                               lambda l, j: (l, source_row(j), 0))],
        out_specs=pl.BlockSpec((None, D_MODEL, PACK_ROWS), lambda l, j: (l, 0, j)),
        out_shape=jax.ShapeDtypeStruct((DEPTH, D_MODEL, sum(region_widths)), BF16),
        compiler_params=_params(2),
        name="pack_w_in",
    )(w_in_t)


def _inproj_kernel(x_ref, mod_ref, g_ref, w_ref, wfb_ref, proj_ref, fb_ref, *, tn):
    h = _rms_mod(x_ref[...], g_ref[...], mod_ref[0:1, :], mod_ref[1:2, :]).astype(BF16)
    for j in range(PROJ_COLS // tn):
        sl = slice(j * tn, (j + 1) * tn)
        proj_ref[:, sl] = jnp.dot(h, w_ref[:, sl], preferred_element_type=F32).astype(BF16)
    fb_ref[...] = jnp.dot(h, wfb_ref[...], preferred_element_type=F32)


def _in_proj(x, mod, gain, w_qkv, w_fb, layer):
    tn = 768
    steps_per_batch = SEQ // TM
    return pl.pallas_call(
        functools.partial(_inproj_kernel, tn=tn),
        grid=(TOKENS // TM,),
        in_specs=[
            pl.BlockSpec((TM, D_MODEL), lambda i: (i, 0)),
            pl.BlockSpec((None, None, 6, D_MODEL), lambda i: (layer, i // steps_per_batch, 0, 0)),
            pl.BlockSpec((None, 1, D_MODEL), lambda i: (layer, 0, 0)),
            pl.BlockSpec((None, D_MODEL, PROJ_COLS), lambda i: (layer, 0, 0)),
            pl.BlockSpec((None, D_MODEL, LANES), lambda i: (layer, 0, 0)),
        ],
        out_specs=[
            pl.BlockSpec((TM, PROJ_COLS), lambda i: (i, 0)),
            pl.BlockSpec((TM, LANES), lambda i: (i, 0)),
        ],
        out_shape=[
            jax.ShapeDtypeStruct((TOKENS, PROJ_COLS), BF16),
            jax.ShapeDtypeStruct((TOKENS, LANES), F32),
        ],
        compiler_params=_params(1),
        name="in_proj",
    )(x, mod, gain, w_qkv, w_fb)


def _cumsum_kernel(fb_ref, bias_ref, col_ref, row_ref):
    r = lax.broadcasted_iota(jnp.int32, (CUM_BLK, CUM_BLK), 0)
    c = lax.broadcasted_iota(jnp.int32, (CUM_BLK, CUM_BLK), 1)
    tri = (r >= c).astype(F32)
    carry = jnp.zeros((1, LANES), F32)
    for blk in range(SEQ // CUM_BLK):
        rows = slice(blk * CUM_BLK, (blk + 1) * CUM_BLK)
        z = fb_ref[0, rows, :] + bias_ref[...]
        log_f = jnp.minimum(z, 0.0) - jnp.log1p(jnp.exp(-jnp.abs(z)))
        cum = jnp.dot(tri, log_f, preferred_element_type=F32,
                      precision=lax.Precision.HIGHEST) + carry
        col_ref[0, rows, :] = cum
        row_ref[0, :, rows] = cum.T[:B_HEADS, :]
        carry = cum[CUM_BLK - 1:CUM_BLK, :]


def _forget_cumsum(fb, b_forget_row):
    return pl.pallas_call(
        _cumsum_kernel,
        grid=(BATCH,),
        in_specs=[
            pl.BlockSpec((1, SEQ, LANES), lambda b: (b, 0, 0)),
            pl.BlockSpec((1, LANES), lambda b: (0, 0)),
        ],
        out_specs=[
            pl.BlockSpec((1, SEQ, LANES), lambda b: (b, 0, 0)),
            pl.BlockSpec((1, B_HEADS, SEQ), lambda b: (b, 0, 0)),
        ],
        out_shape=[
            jax.ShapeDtypeStruct((BATCH, SEQ, LANES), F32),
            jax.ShapeDtypeStruct((BATCH, B_HEADS, SEQ), F32),
        ],
        compiler_params=_params(1),
        name="forget_cumsum",
    )(fb.reshape(BATCH, SEQ, LANES), b_forget_row)


def _fox_kernel(q_ref, k_ref, v_ref, ccol_ref, crow_ref, o_ref, vt_ref, ckb_ref,
                s0_ref, s1_ref, p0_ref, p1_ref, acc_ref, m_ref, a0_ref, a1_ref, cq_ref, qm_ref):
    tq, tk = FOX_TQ, FOX_TK
    heads = 2 * FOX_PAIRS
    grp = pl.program_id(1)
    first = _lane_half((1, LANES))

    head_lane = lax.broadcasted_iota(jnp.int32, (1, LANES), 1)
    ones_row = (lax.broadcasted_iota(jnp.int32, (FOX_VROWS - HEAD_DIM, tk), 0) == 0).astype(BF16)
    for jt in range(SEQ // tk):
        rows = slice(jt * tk, (jt + 1) * tk)
        v_t = v_ref[rows, :].astype(F32).T.astype(BF16)
        ccol = ccol_ref[0, rows, :]
        for h in range(heads):
            vt_ref[jt, h * FOX_VROWS:h * FOX_VROWS + HEAD_DIM, :] = v_t[h * HEAD_DIM:(h + 1) * HEAD_DIM]
            vt_ref[jt, h * FOX_VROWS + HEAD_DIM:(h + 1) * FOX_VROWS, :] = ones_row
            col = jnp.sum(jnp.where(head_lane == grp * heads + h, ccol, 0.0), axis=1, keepdims=True)
            ckb_ref[h, rows, :] = jnp.broadcast_to(col * LOG2E, (tk, LANES))

    for qi in range(SEQ // tq):
        _fox_query_tile(qi, grp, first, q_ref, k_ref, crow_ref, o_ref, vt_ref, ckb_ref, s0_ref, s1_ref,
                        p0_ref, p1_ref, acc_ref, m_ref, a0_ref, a1_ref, cq_ref.at[qi & 1], qm_ref.at[qi & 1])


def _fox_query_tile(qi, grp, first, q_ref, k_ref, crow_ref, o_ref, vt_ref, ckb_ref, s0_ref, s1_ref,
                    p0_ref, p1_ref, acc_ref, m_ref, a0_ref, a1_ref, cq_ref, qm_ref):
    tq, tk = FOX_TQ, FOX_TK
    heads = 2 * FOX_PAIRS
    q_rows = slice(qi * tq, (qi + 1) * tq)

    for h in range(heads):
        q2 = q_ref[q_rows, (h // 2) * LANES:(h // 2 + 1) * LANES]
        qm_ref[h] = jnp.where(first if h % 2 == 0 else ~first, q2, jnp.zeros_like(q2))
        cq_ref[h] = crow_ref[0, grp * heads + h, qi:qi + 1, :] * LOG2E

    def scores(j, s_ref):
        row0 = _row_start(j, tk)
        for h in range(heads):
            kj = k_ref[pl.ds(row0, tk), (h // 2) * LANES:(h // 2 + 1) * LANES]
            s = lax.dot_general(kj, qm_ref[h], (((1,), (1,)), ((), ())), preferred_element_type=F32)
            ck = ckb_ref[h, pl.ds(row0, tk), :]
            s_ref[h] = s - jnp.concatenate([ck] * (tq // LANES), axis=1)

    def softmax(s_ref, p_ref, a_ref, mask):
        for h in range(heads):
            s = s_ref[h]
            if mask is not None:
                s = jnp.where(mask, s, NEG_INF)
            m_old = m_ref[h]
            cq = cq_ref[h]
            m_new = jnp.maximum(m_old, jnp.max(s, axis=0, keepdims=True) + cq)
            p_ref[h] = jnp.exp2(s + (cq - m_new)).astype(BF16)
            m_ref[h] = m_new
            a_ref[h] = jnp.exp2(m_old - m_new)

    def accumulate(j, p_ref, a_ref):
        tile = max(j, 0)
        for h in range(heads):
            vt = vt_ref[tile, h * FOX_VROWS:(h + 1) * FOX_VROWS, :]
            pv = jnp.dot(vt, p_ref[h], preferred_element_type=F32)
            acc_ref[h] = a_ref[h] * acc_ref[h] + pv

    def trip(i, mask_even, mask_odd, last):
        cur = i & 1
        even = 2 * i
        scores(even + 1, s1_ref)
        accumulate(even - 1, p1_ref.at[1 - cur], a1_ref.at[1 - cur])
        softmax(s0_ref.at[cur], p0_ref, a0_ref, mask_even)
        if not last:
            scores(even + 2, s0_ref.at[1 - cur])
        accumulate(even, p0_ref, a0_ref)
        softmax(s1_ref, p1_ref.at[cur], a1_ref.at[cur], mask_odd)
        if last:
            accumulate(even + 1, p1_ref.at[cur], a1_ref.at[cur])

    acc_ref[...] = jnp.zeros(acc_ref.shape, F32)
    m_ref[...] = jnp.full(m_ref.shape, NEG_INF, F32)
    p1_ref[1] = jnp.zeros(p1_ref.shape[1:], BF16)
    a1_ref[1] = jnp.ones(a1_ref.shape[1:], F32)
    scores(0, s0_ref.at[0])
    for i in range(qi):
        trip(i, None, None, last=False)
    kpos = lax.broadcasted_iota(jnp.int32, (tk, tq), 0)
    qpos = lax.broadcasted_iota(jnp.int32, (tk, tq), 1)
    trip(qi, kpos <= qpos, kpos + tk <= qpos, last=True)
    out_t = jnp.concatenate([acc_ref[h, :HEAD_DIM, :] / acc_ref[h, HEAD_DIM:HEAD_DIM + 1, :]
                             for h in range(heads)], axis=0)
    o_ref[q_rows, :] = out_t.T.astype(BF16)


def _fox_attention(proj, ccol, crow):
    tq, tk = FOX_TQ, FOX_TK
    nq = SEQ // tq
    heads = 2 * FOX_PAIRS
    width = FOX_PAIRS * LANES
    return pl.pallas_call(
        _fox_kernel,
        grid=(BATCH, N_PAIRS // FOX_PAIRS),
        in_specs=[
            pl.BlockSpec((SEQ, width), lambda b, g: (b, QB_BLK // FOX_PAIRS + g)),
            pl.BlockSpec((SEQ, width), lambda b, g: (b, KB_BLK // FOX_PAIRS + g)),
            pl.BlockSpec((SEQ, width), lambda b, g: (b, VB_BLK // FOX_PAIRS + g)),
            pl.BlockSpec((1, SEQ, LANES), lambda b, g: (b, 0, 0)),
            pl.BlockSpec((1, B_HEADS, nq, tq), lambda b, g: (b, 0, 0, 0)),
        ],
        out_specs=pl.BlockSpec((SEQ, width), lambda b, g: (b, g)),
        out_shape=jax.ShapeDtypeStruct((TOKENS, BRANCH_WIDTH), BF16),
        scratch_shapes=[pltpu.VMEM((SEQ // tk, heads * FOX_VROWS, tk), BF16),
                        pltpu.VMEM((heads, SEQ, LANES), F32),
                        pltpu.VMEM((2, heads, tk, tq), F32), pltpu.VMEM((heads, tk, tq), F32),
                        pltpu.VMEM((heads, tk, tq), BF16), pltpu.VMEM((2, heads, tk, tq), BF16),
                        pltpu.VMEM((heads, FOX_VROWS, tq), F32),
                        pltpu.VMEM((heads, 1, tq), F32),
                        pltpu.VMEM((heads, 1, tq), F32),
                        pltpu.VMEM((2, heads, 1, tq), F32),
                        pltpu.VMEM((2, heads, 1, tq), F32),
                        pltpu.VMEM((2, heads, tq, LANES), BF16)],
        compiler_params=_params(2),
        name="fox_attention",
    )(proj, proj, proj, ccol, crow.reshape(BATCH, B_HEADS, nq, tq))


def _swa_kernel(sink_ref, q_ref, k_ref, v_ref, o_ref, kd_ref, vd_ref,
                s0_ref, s1_ref, p0_ref, p1_ref, d0_ref, d1_ref, *, layer):
    rows = A_GROUP * CHUNK
    first = _lane_half((1, LANES))
    grp = lax.broadcasted_iota(jnp.int32, (rows, 1), 0) // CHUNK
    qi = lax.broadcasted_iota(jnp.int32, (rows, A_BAND), 0) % CHUNK
    si = lax.broadcasted_iota(jnp.int32, (rows, A_BAND), 1)
    sel_r = lax.broadcasted_iota(jnp.int32, (LANES, LANES), 0)
    sel_c = lax.broadcasted_iota(jnp.int32, (LANES, LANES), 1)

    slopes, sinks = [], []
    for kvh in range(A_KV_HEADS):
        sel = (sel_r == kvh * HEAD_DIM + sel_c % HEAD_DIM).astype(BF16)
        kd_ref[kvh] = jnp.dot(k_ref[...], sel, preferred_element_type=F32).astype(BF16)
        vd_ref[kvh] = jnp.dot(v_ref[...], sel, preferred_element_type=F32).astype(BF16)
        slope = jnp.zeros((rows, 1), F32)
        sink = jnp.zeros((rows, 1), F32)
        for g in range(A_GROUP):
            head = kvh * A_GROUP + g
            slope = jnp.where(grp == g, 2.0 ** -(head + 1), slope)
            sink = jnp.where(grp == g, sink_ref[layer, head], sink)
        slopes.append(slope)
        sinks.append(sink)

    def band_start(n):
        if isinstance(n, int):
            return max(n - A_PREV, 0) * CHUNK
        return _row_start(jnp.maximum(n - A_PREV, 0), CHUNK)

    def scores(n, s_ref, alibi):
        q0 = _row_start(n, CHUNK)
        for kvh in range(A_KV_HEADS):
            parts = []
            for g in range(A_GROUP):
                head = kvh * A_GROUP + g
                blk = q_ref[pl.ds(q0, CHUNK), (head // 2) * LANES:(head // 2 + 1) * LANES]
                parts.append(jnp.where(first if head % 2 == 0 else ~first, blk, jnp.zeros_like(blk)))
            qs = jnp.concatenate(parts, axis=0)
            kb = kd_ref[kvh, pl.ds(band_start(n), A_BAND), :]
            s = lax.dot_general(qs, kb, (((1,), (1,)), ((), ())), preferred_element_type=F32)
            s_ref[kvh] = s + alibi[kvh]

    def softmax(s_ref, p_ref, d_ref, valid):
        for kvh in range(A_KV_HEADS):
            s = s_ref[kvh] if valid is None else jnp.where(valid, s_ref[kvh], NEG_INF)
            m = jnp.maximum(jnp.max(s, axis=1, keepdims=True), sinks[kvh])
            p = jnp.exp(s - m)
            denom = jnp.sum(p, axis=1, keepdims=True) + jnp.exp(sinks[kvh] - m)
            d_ref[kvh] = jnp.broadcast_to(denom, (rows, LANES))
            p_ref[kvh] = p.astype(BF16)

    def emit(n, p_ref, d_ref):
        q0 = _row_start(n, CHUNK)
        for kvh in range(A_KV_HEADS):
            vb = vd_ref[kvh, pl.ds(band_start(n), A_BAND), :]
            r = jnp.dot(p_ref[kvh], vb, preferred_element_type=F32) / d_ref[kvh]
            for pr in range(A_GROUP // 2):
                even = r[(2 * pr) * CHUNK:(2 * pr + 1) * CHUNK]
                odd = r[(2 * pr + 1) * CHUNK:(2 * pr + 2) * CHUNK]
                col = (kvh * (A_GROUP // 2) + pr) * LANES
                o_ref[pl.ds(q0, CHUNK), col:col + LANES] = jnp.where(first, even, odd).astype(BF16)

    def alibi_for(dist):
        return tuple(-slope * jnp.abs(dist).astype(F32) for slope in slopes)

    alibi = alibi_for(A_PREV * CHUNK + qi - si)
    def trip(i, carry=0, *, first_trip=False, last_trip=False):
        cur = i & 1
        even = 2 * i
        if first_trip:
            scores(1, s1_ref, alibi_for(CHUNK + qi - si))
            softmax(s0_ref.at[cur], p0_ref, d0_ref, si // CHUNK <= 0)
        else:
            scores(even + 1, s1_ref, alibi)
            emit(even - 1, p1_ref.at[1 - cur], d1_ref.at[1 - cur])
            softmax(s0_ref.at[cur], p0_ref, d0_ref, None)
        if not last_trip:
            scores(even + 2, s0_ref.at[1 - cur], alibi)
        emit(even, p0_ref, d0_ref)
        softmax(s1_ref, p1_ref.at[cur], d1_ref.at[cur], si // CHUNK <= 1 if first_trip else None)
        if last_trip:
            emit(even + 1, p1_ref.at[cur], d1_ref.at[cur])
        return carry

    scores(0, s0_ref.at[0], alibi_for(qi - si))
    trip(0, first_trip=True)
    for i in range(1, N_CHUNKS // 2 - 1):
        trip(i)
    trip(N_CHUNKS // 2 - 1, last_trip=True)


def _swa_attention(proj, sinks, layer):
    return pl.pallas_call(
        functools.partial(_swa_kernel, layer=layer),
        grid=(BATCH,),
        in_specs=[
            pl.BlockSpec(memory_space=pltpu.SMEM),
            pl.BlockSpec((SEQ, A_HEADS * HEAD_DIM), lambda b: (b, QA_BLK // N_PAIRS)),
            pl.BlockSpec((SEQ, LANES), lambda b: (b, KA_BLK)),
            pl.BlockSpec((SEQ, LANES), lambda b: (b, VA_BLK)),
        ],
        out_specs=pl.BlockSpec((SEQ, BRANCH_WIDTH), lambda b: (b, 0)),
        out_shape=jax.ShapeDtypeStruct((TOKENS, BRANCH_WIDTH), BF16),
        scratch_shapes=[pltpu.VMEM((A_KV_HEADS, SEQ, LANES), BF16),
                        pltpu.VMEM((A_KV_HEADS, SEQ, LANES), BF16),
                        pltpu.VMEM((2, A_KV_HEADS, A_GROUP * CHUNK, A_BAND), F32),
                        pltpu.VMEM((A_KV_HEADS, A_GROUP * CHUNK, A_BAND), F32),
                        pltpu.VMEM((A_KV_HEADS, A_GROUP * CHUNK, A_BAND), BF16),
                        pltpu.VMEM((2, A_KV_HEADS, A_GROUP * CHUNK, A_BAND), BF16),
                        pltpu.VMEM((A_KV_HEADS, A_GROUP * CHUNK, LANES), F32),
                        pltpu.VMEM((2, A_KV_HEADS, A_GROUP * CHUNK, LANES), F32)],
        compiler_params=_params(1),
        name="swa_attention",
    )(sinks, proj, proj, proj)


def _chunked_kernel(q_ref, k_ref, v_ref, bias_ref, o_ref, kp_ref, vp_ref,
                    s0_ref, s1_ref, p0_ref, p1_ref, d0_ref, d1_ref):
    first = _lane_half((1, LANES))
    width = N_PAIRS * LANES
    kp_ref[0:C_PAD, :] = jnp.zeros((C_PAD, width), BF16)
    vp_ref[0:C_PAD, :] = jnp.zeros((C_PAD, width), BF16)
    kp_ref[C_PAD:, :] = k_ref[...]
    vp_ref[C_PAD:, :] = v_ref[...]
    si = lax.broadcasted_iota(jnp.int32, (2 * CHUNK, C_BAND), 1)

    def scores(n, s_ref):
        q0 = _row_start(n, CHUNK)
        for pair in range(N_PAIRS):
            cols = slice(pair * LANES, (pair + 1) * LANES)
            q2 = q_ref[pl.ds(q0, CHUNK), cols]
            zero = jnp.zeros_like(q2)
            qs = jnp.concatenate([jnp.where(first, q2, zero), jnp.where(first, zero, q2)], axis=0)
            kb = kp_ref[pl.ds(q0, C_BAND), cols]
            s = lax.dot_general(qs, kb, (((1,), (1,)), ((), ())), preferred_element_type=F32)
            s_ref[pair] = s + bias_ref[pair]

    def softmax(n, s_ref, p_ref, d_ref, masked):
        for pair in range(N_PAIRS):
            s = s_ref[pair]
            if masked:
                s = jnp.where(n * CHUNK + si >= C_PAD, s, NEG_INF)
            p = jnp.exp(s - jnp.max(s, axis=1, keepdims=True))
            d_ref[pair] = jnp.sum(p, axis=1, keepdims=True)
            p_ref[pair] = p.astype(BF16)

    def emit(n, p_ref, d_ref):
        q0 = _row_start(n, CHUNK)
        for pair in range(N_PAIRS):
            cols = slice(pair * LANES, (pair + 1) * LANES)
            vb = vp_ref[pl.ds(q0, C_BAND), cols]
            r = jnp.dot(p_ref[pair], vb, preferred_element_type=F32) / d_ref[pair]
            o_ref[pl.ds(q0, CHUNK), cols] = jnp.where(first, r[:CHUNK], r[CHUNK:]).astype(BF16)

    def trip(i, carry=0, *, masked, first_trip=False, last_trip=False):
        cur = i & 1
        even = 2 * i
        scores(even + 1, s1_ref)
        if not first_trip:
            emit(even - 1, p1_ref.at[1 - cur], d1_ref.at[1 - cur])
        softmax(even, s0_ref.at[cur], p0_ref, d0_ref, masked)
        if not last_trip:
            scores(even + 2, s0_ref.at[1 - cur])
        emit(even, p0_ref, d0_ref)
        softmax(even + 1, s1_ref, p1_ref.at[cur], d1_ref.at[cur], masked)
        if last_trip:
            emit(even + 1, p1_ref.at[cur], d1_ref.at[cur])
        return carry

    masked_trips = C_PREV // 2
    scores(0, s0_ref.at[0])
    trip(0, masked=True, first_trip=True)
    lax.fori_loop(1, masked_trips, functools.partial(trip, masked=True), 0)
    lax.fori_loop(masked_trips, N_CHUNKS // 2 - 1, functools.partial(trip, masked=False), 0)
    trip(N_CHUNKS // 2 - 1, masked=False, last_trip=True)


def _chunked_attention(proj, bias, layer):
    width = N_PAIRS * LANES
    return pl.pallas_call(
        _chunked_kernel,
        grid=(BATCH,),
        in_specs=[
            pl.BlockSpec((SEQ, width), lambda b: (b, QC_BLK // N_PAIRS)),
            pl.BlockSpec((SEQ, width), lambda b: (b, KC_BLK // N_PAIRS)),
            pl.BlockSpec((SEQ, width), lambda b: (b, VC_BLK // N_PAIRS)),
            pl.BlockSpec((None, N_PAIRS, 2 * CHUNK, C_BAND), lambda b: (layer, 0, 0, 0)),
        ],
        out_specs=pl.BlockSpec((SEQ, width), lambda b: (b, 0)),
        out_shape=jax.ShapeDtypeStruct((TOKENS, BRANCH_WIDTH), BF16),
        scratch_shapes=[pltpu.VMEM((C_PAD + SEQ, width), BF16), pltpu.VMEM((C_PAD + SEQ, width), BF16),
                        pltpu.VMEM((2, N_PAIRS, 2 * CHUNK, C_BAND), F32),
                        pltpu.VMEM((N_PAIRS, 2 * CHUNK, C_BAND), F32),
                        pltpu.VMEM((N_PAIRS, 2 * CHUNK, C_BAND), BF16),
                        pltpu.VMEM((2, N_PAIRS, 2 * CHUNK, C_BAND), BF16),
                        pltpu.VMEM((N_PAIRS, 2 * CHUNK, 1), F32),
                        pltpu.VMEM((2, N_PAIRS, 2 * CHUNK, 1), F32)],
        compiler_params=_params(1),
        name="chunked_attention",
    )(proj, proj, proj, bias)


def _merge_kernel(x_ref, mod_ref, g_ref, oa_ref, ob_ref, oc_ref, wb_ref, wg_ref, wo_ref, out_ref,
                  merged_ref, *, tn):
    h = _rms_mod(x_ref[...], g_ref[...], mod_ref[0:1, :], mod_ref[1:2, :]).astype(BF16)
    branches = (oa_ref[...], ob_ref[...], oc_ref[...])
    for n in range(D_MODEL // tn):
        acc = None
        for k, o in enumerate(branches):
            y = jnp.dot(o, wb_ref[k, :, n * tn:(n + 1) * tn], preferred_element_type=F32)
            gate = jnp.dot(h, wg_ref[:, k * D_MODEL + n * tn:k * D_MODEL + (n + 1) * tn],
                           preferred_element_type=F32)
            term = jax.nn.sigmoid(gate) * y
            acc = term if acc is None else acc + term
        merged_ref[:, n * tn:(n + 1) * tn] = acc.astype(BF16)
    merged = merged_ref[...]
    for n in range(D_MODEL // tn):
        sl = slice(n * tn, (n + 1) * tn)
        out = jnp.dot(merged, wo_ref[:, sl], preferred_element_type=F32)
        out_ref[:, sl] = x_ref[:, sl] + mod_ref[2:3, sl] * out


def _merge(x, mod, gain, o_a, o_b, o_c, w_branch, w_gate, w_out, layer):
    steps_per_batch = SEQ // TM
    row = lambda i: (i, 0)
    return pl.pallas_call(
        functools.partial(_merge_kernel, tn=256),
        grid=(TOKENS // TM,),
        in_specs=[
            pl.BlockSpec((TM, D_MODEL), row),
            pl.BlockSpec((None, None, 6, D_MODEL), lambda i: (layer, i // steps_per_batch, 0, 0)),
            pl.BlockSpec((None, 1, D_MODEL), lambda i: (layer, 0, 0)),
            pl.BlockSpec((TM, BRANCH_WIDTH), row),
            pl.BlockSpec((TM, BRANCH_WIDTH), row),
            pl.BlockSpec((TM, BRANCH_WIDTH), row),
            pl.BlockSpec((None, 3, BRANCH_WIDTH, D_MODEL), lambda i: (layer, 0, 0, 0)),
            pl.BlockSpec((None, D_MODEL, 3 * D_MODEL), lambda i: (layer, 0, 0)),
            pl.BlockSpec((None, D_MODEL, D_MODEL), lambda i: (layer, 0, 0)),
        ],
        out_specs=pl.BlockSpec((TM, D_MODEL), row),
        out_shape=jax.ShapeDtypeStruct((TOKENS, D_MODEL), F32),
        scratch_shapes=[pltpu.VMEM((TM, D_MODEL), BF16)],
        compiler_params=_params(1),
        name="merge_out",
    )(x, mod, gain, o_a, o_b, o_c, w_branch, w_gate, w_out)


def _ffn_kernel(x_ref, mod_ref, g_ref, gf_ref, wi_ref, wo_ref, out_ref, act_ref, *, tf, tn, final):
    h = _rms_mod(x_ref[...], g_ref[...], mod_ref[3:4, :], mod_ref[4:5, :]).astype(BF16)
    for c in range(FFN_HIDDEN // tf):
        gate = jnp.dot(h, wi_ref[:, c * tf:(c + 1) * tf], preferred_element_type=F32)
        up = jnp.dot(h, wi_ref[:, FFN_HIDDEN + c * tf:FFN_HIDDEN + (c + 1) * tf],
                     preferred_element_type=F32)
        act_ref[:, c * tf:(c + 1) * tf] = (gate * jax.nn.sigmoid(gate) * up).astype(BF16)
    act = act_ref[...]
    for n in range(D_MODEL // tn):
        sl = slice(n * tn, (n + 1) * tn)
        out = jnp.dot(act, wo_ref[:, sl], preferred_element_type=F32)
        out_ref[:, sl] = x_ref[:, sl] + mod_ref[5:6, sl] * out
    if final:
        y = out_ref[...]
        ms = jnp.mean(y * y, axis=-1, keepdims=True)
        out_ref[...] = y * lax.rsqrt(ms + EPS) * gf_ref[...]


def _ffn(x, mod, gain, final_gain, w_ffn_in, w_ffn_out, layer, final):
    steps_per_batch = SEQ // TM
    row = lambda i: (i, 0)
    return pl.pallas_call(
        functools.partial(_ffn_kernel, tf=256, tn=256, final=final),
        grid=(TOKENS // TM,),
        in_specs=[
            pl.BlockSpec((TM, D_MODEL), row),
            pl.BlockSpec((None, None, 6, D_MODEL), lambda i: (layer, i // steps_per_batch, 0, 0)),
            pl.BlockSpec((None, 1, D_MODEL), lambda i: (layer, 0, 0)),
            pl.BlockSpec((1, D_MODEL), lambda i: (0, 0)),
            pl.BlockSpec((None, D_MODEL, 2 * FFN_HIDDEN), lambda i: (layer, 0, 0)),
            pl.BlockSpec((None, FFN_HIDDEN, D_MODEL), lambda i: (layer, 0, 0)),
        ],
        out_specs=pl.BlockSpec((TM, D_MODEL), row),
        out_shape=jax.ShapeDtypeStruct((TOKENS, D_MODEL), F32),
        scratch_shapes=[pltpu.VMEM((TM, FFN_HIDDEN), BF16)],
        compiler_params=_params(1),
        name="ffn",
    )(x, mod, gain, final_gain, w_ffn_in, w_ffn_out)


def kernel(x, c, norm_mix_g, norm_ffn_g, w_ada, b_ada, w_in, b_forget, sinks, rel_bias,
           w_branch, w_out, w_ffn_in, w_ffn_out, final_norm_g):
    scale = HEAD_DIM ** -0.5
    w_in_t = jnp.swapaxes(w_in, 1, 2)
    w_qkv = _pack_w_in(w_in_t, (IN_B0, IN_C0, IN_A0), (IN_FB0 - IN_B0, IN_GATE0 - IN_C0, IN_B0 - IN_A0),
                       (scale * LOG2E, scale, scale))
    w_gate = _pack_w_in(w_in_t, (IN_GATE0,), (N_IN_COLS - IN_GATE0,), (None,))
    w_fb = jnp.pad(w_in[:, :, IN_FB0:IN_FB0 + B_HEADS], ((0, 0), (0, 0), (0, LANES - B_HEADS))).astype(BF16)
    w_branch_b = w_branch.astype(BF16)
    w_out_b = w_out.astype(BF16)
    w_ffn_in_b = w_ffn_in.astype(BF16)
    w_ffn_out_b = w_ffn_out.astype(BF16)
    b_forget_rows = jnp.pad(b_forget, ((0, 0), (0, LANES - B_HEADS))).reshape(DEPTH, 1, LANES)
    gain_mix = norm_mix_g.reshape(DEPTH, 1, D_MODEL)
    gain_ffn = norm_ffn_g.reshape(DEPTH, 1, D_MODEL)
    gain_final = final_norm_g.reshape(1, D_MODEL)

    mod = _ada_mod(c, w_ada, b_ada)
    bias_c = _rel_bias(rel_bias)

    xt = x.reshape(TOKENS, D_MODEL)
    for layer in range(DEPTH):
        proj, fb = _in_proj(xt, mod, gain_mix, w_qkv, w_fb, layer)
        ccol, crow = _forget_cumsum(fb, b_forget_rows[layer])
        o_a = _swa_attention(proj, sinks, layer)
        o_b = _fox_attention(proj, ccol, crow)
        o_c = _chunked_attention(proj, bias_c, layer)
        xt = _merge(xt, mod, gain_mix, o_a, o_b, o_c, w_branch_b, w_gate, w_out_b, layer)
        xt = _ffn(xt, mod, gain_ffn, gain_final, w_ffn_in_b, w_ffn_out_b, layer, layer == DEPTH - 1)
    return xt.reshape(BATCH, SEQ, D_MODEL)
```

```python
import functools

import jax
import jax.numpy as jnp
from jax import lax
from jax.experimental import pallas as pl
from jax.experimental.pallas import tpu as pltpu

F32 = jnp.float32
BF16 = jnp.bfloat16

D_MODEL = 1024
BATCH = 8
SEQ = 2048
TOKENS = BATCH * SEQ
DEPTH = 2
CHUNK = 64
HEAD_DIM = 64
EPS = 1e-6
NEG_INF = -1e30
LOG2E = 1.4426950408889634

A_HEADS = 8
A_KV_HEADS = 2
A_GROUP = A_HEADS // A_KV_HEADS
A_PREV = 2
A_BAND = (A_PREV + 1) * CHUNK
B_HEADS = 8
C_HEADS = 8
C_PREV = 8
C_PAD = C_PREV * CHUNK
C_BAND = (C_PREV + 1) * CHUNK
REL_CLIP = 128
N_REL = 2 * REL_CLIP + 1
BRANCH_WIDTH = 512
FFN_HIDDEN = 2816
N_CHUNKS = SEQ // CHUNK
N_IN_COLS = 6920

LANES = 128
N_PAIRS = 4

PROJ_COLS = 3840
QB_BLK, KB_BLK, VB_BLK = 0, 4, 8
QC_BLK, KC_BLK, VC_BLK = 12, 16, 20
QA_BLK, KA_BLK, VA_BLK = 24, 28, 29

TM = 512
FOX_TQ = 512
FOX_TK = 256
FOX_VROWS = HEAD_DIM + 16
FOX_PAIRS = 2
CUM_BLK = 256

VMEM_LIMIT = 56 * 1024 * 1024


def _params(n_axes):
    return pltpu.CompilerParams(dimension_semantics=("arbitrary",) * n_axes,
                                vmem_limit_bytes=VMEM_LIMIT)


def _rms_mod(x, g, shift, scale):
    ms = jnp.mean(x * x, axis=-1, keepdims=True)
    y = x * lax.rsqrt(ms + EPS) * g
    return y * (1.0 + scale) + shift


def _row_start(index, size):
    if isinstance(index, int):
        return index * size
    return pl.multiple_of(index * size, size)


def _lane_half(shape):
    return lax.broadcasted_iota(jnp.int32, shape, len(shape) - 1) < HEAD_DIM


def _ada_kernel(c_ref, w_ref, b_ref, o_ref):
    c = c_ref[...]
    cond = c * jax.nn.sigmoid(c)
    o_ref[...] = jnp.dot(cond.astype(BF16), w_ref[...].astype(BF16),
                         preferred_element_type=F32) + b_ref[...]


def _ada_mod(c, w_ada, b_ada):
    n_blk = 6
    out = pl.pallas_call(
        _ada_kernel,
        grid=(DEPTH, n_blk),
        in_specs=[
            pl.BlockSpec((BATCH, D_MODEL), lambda l, j: (0, 0)),
            pl.BlockSpec((None, D_MODEL, D_MODEL), lambda l, j: (l, 0, j)),
            pl.BlockSpec((None, 1, D_MODEL), lambda l, j: (l, 0, j)),
        ],
        out_specs=pl.BlockSpec((None, BATCH, D_MODEL), lambda l, j: (l, 0, j)),
        out_shape=jax.ShapeDtypeStruct((DEPTH, BATCH, n_blk * D_MODEL), F32),
        compiler_params=_params(2),
        name="ada_mod",
    )(c, w_ada, b_ada.reshape(DEPTH, 1, n_blk * D_MODEL))
    return out.reshape(DEPTH, BATCH, n_blk, D_MODEL)


def _relbias_kernel(rev_ref, o_ref):
    near = C_PAD - REL_CLIP
    width = 2 * REL_CLIP
    q = lax.broadcasted_iota(jnp.int32, (CHUNK, width), 0)
    c = lax.broadcasted_iota(jnp.int32, (CHUNK, width), 1)
    for h in range(C_HEADS):
        rev = rev_ref[h:h + 1, :] * LOG2E
        far = rev[:, 0:1]
        rolled = pltpu.roll(jnp.broadcast_to(rev, (CHUNK, width)), 0, 1, stride=1, stride_axis=0)
        o_ref[h, :, 0:near] = jnp.broadcast_to(far, (CHUNK, near))
        o_ref[h, :, near:C_BAND] = jnp.where(c >= q, rolled, far)[:, :C_BAND - near]


def _rel_bias(rel_bias):
    rev = rel_bias[:, :, ::-1][:, :, :2 * REL_CLIP]
    out = pl.pallas_call(
        _relbias_kernel,
        grid=(DEPTH,),
        in_specs=[pl.BlockSpec((None, C_HEADS, 2 * REL_CLIP), lambda l: (l, 0, 0))],
        out_specs=pl.BlockSpec((None, C_HEADS, CHUNK, C_BAND), lambda l: (l, 0, 0, 0)),
        out_shape=jax.ShapeDtypeStruct((DEPTH, C_HEADS, CHUNK, C_BAND), F32),
        compiler_params=_params(1),
        name="rel_bias",
    )(rev)
    return out.reshape(DEPTH, N_PAIRS, 2 * CHUNK, C_BAND)


IN_A0, IN_B0, IN_FB0, IN_C0, IN_GATE0 = 0, 768, 2304, 2312, 3848
PACK_ROWS = 384


def _pack_kernel(wt_ref, o_ref, *, region_starts, region_steps, q_scales):
    j = pl.program_id(1)
    w = wt_ref[0]
    first_step = 0
    row = lax.broadcasted_iota(jnp.int32, (PACK_ROWS, 1), 0)
    scale = jnp.ones((PACK_ROWS, 1), F32)
    for steps, q_scale in zip(region_steps, q_scales):
        if q_scale is not None:
            in_region = (j >= first_step) & (j < first_step + steps)
            col_in_region = (j - first_step) * PACK_ROWS + row
            scale = jnp.where(in_region & (col_in_region < A_HEADS * HEAD_DIM), q_scale, scale)
        first_step += steps
    o_ref[...] = (w * scale).T.astype(BF16)


def _pack_w_in(w_in_t, region_starts, region_widths, q_scales):
    region_steps = tuple(w // PACK_ROWS for w in region_widths)

    def source_row(j):
        row = jnp.int32(0)
        first_step = 0
        for start, steps in zip(region_starts, region_steps):
            row = jnp.where(j >= first_step, start + (j - first_step) * PACK_ROWS, row)
            first_step += steps
        return pl.multiple_of(row, 8)

    return pl.pallas_call(
        functools.partial(_pack_kernel, region_starts=region_starts, region_steps=region_steps, q_scales=q_scales),
        grid=(DEPTH, sum(region_steps)),
        in_specs=[pl.BlockSpec((pl.Element(1), pl.Element(PACK_ROWS), pl.Element(D_MODEL)),
                               lambda l, j: (l, source_row(j), 0))],
        out_specs=pl.BlockSpec((None, D_MODEL, PACK_ROWS), lambda l, j: (l, 0, j)),
        out_shape=jax.ShapeDtypeStruct((DEPTH, D_MODEL, sum(region_widths)), BF16),
        compiler_params=_params(2),
        name="pack_w_in",
    )(w_in_t)


def _inproj_kernel(x_ref, mod_ref, g_ref, w_ref, wfb_ref, proj_ref, fb_ref, *, tn):
    h = _rms_mod(x_ref[...], g_ref[...], mod_ref[0:1, :], mod_ref[1:2, :]).astype(BF16)
    for j in range(PROJ_COLS // tn):
        sl = slice(j * tn, (j + 1) * tn)
        proj_ref[:, sl] = jnp.dot(h, w_ref[:, sl], preferred_element_type=F32).astype(BF16)
    fb_ref[...] = jnp.dot(h, wfb_ref[...], preferred_element_type=F32)


def _in_proj(x, mod, gain, w_qkv, w_fb, layer):
    tn = 768
    steps_per_batch = SEQ // TM
    return pl.pallas_call(
        functools.partial(_inproj_kernel, tn=tn),
        grid=(TOKENS // TM,),
        in_specs=[
            pl.BlockSpec((TM, D_MODEL), lambda i: (i, 0)),
            pl.BlockSpec((None, None, 6, D_MODEL), lambda i: (layer, i // steps_per_batch, 0, 0)),
            pl.BlockSpec((None, 1, D_MODEL), lambda i: (layer, 0, 0)),
            pl.BlockSpec((None, D_MODEL, PROJ_COLS), lambda i: (layer, 0, 0)),
            pl.BlockSpec((None, D_MODEL, LANES), lambda i: (layer, 0, 0)),
        ],
        out_specs=[
            pl.BlockSpec((TM, PROJ_COLS), lambda i: (i, 0)),
            pl.BlockSpec((TM, LANES), lambda i: (i, 0)),
        ],
        out_shape=[
            jax.ShapeDtypeStruct((TOKENS, PROJ_COLS), BF16),
            jax.ShapeDtypeStruct((TOKENS, LANES), F32),
        ],
        compiler_params=_params(1),
        name="in_proj",
    )(x, mod, gain, w_qkv, w_fb)


def _cumsum_kernel(fb_ref, bias_ref, col_ref, row_ref):
    r = lax.broadcasted_iota(jnp.int32, (CUM_BLK, CUM_BLK), 0)
    c = lax.broadcasted_iota(jnp.int32, (CUM_BLK, CUM_BLK), 1)
    tri = (r >= c).astype(F32)
    carry = jnp.zeros((1, LANES), F32)
    for blk in range(SEQ // CUM_BLK):
        rows = slice(blk * CUM_BLK, (blk + 1) * CUM_BLK)
        z = fb_ref[0, rows, :] + bias_ref[...]
        log_f = jnp.minimum(z, 0.0) - jnp.log1p(jnp.exp(-jnp.abs(z)))
        cum = jnp.dot(tri, log_f, preferred_element_type=F32,
                      precision=lax.Precision.HIGHEST) + carry
        col_ref[0, rows, :] = cum
        row_ref[0, :, rows] = cum.T[:B_HEADS, :]
        carry = cum[CUM_BLK - 1:CUM_BLK, :]


def _forget_cumsum(fb, b_forget_row):
    return pl.pallas_call(
        _cumsum_kernel,
        grid=(BATCH,),
        in_specs=[
            pl.BlockSpec((1, SEQ, LANES), lambda b: (b, 0, 0)),
            pl.BlockSpec((1, LANES), lambda b: (0, 0)),
        ],
        out_specs=[
            pl.BlockSpec((1, SEQ, LANES), lambda b: (b, 0, 0)),
            pl.BlockSpec((1, B_HEADS, SEQ), lambda b: (b, 0, 0)),
        ],
        out_shape=[
            jax.ShapeDtypeStruct((BATCH, SEQ, LANES), F32),
            jax.ShapeDtypeStruct((BATCH, B_HEADS, SEQ), F32),
        ],
        compiler_params=_params(1),
        name="forget_cumsum",
    )(fb.reshape(BATCH, SEQ, LANES), b_forget_row)


def _fox_kernel(q_ref, k_ref, v_ref, ccol_ref, crow_ref, o_ref, vt_ref, ckb_ref,
                s0_ref, s1_ref, p0_ref, p1_ref, acc_ref, m_ref, a0_ref, a1_ref, cq_ref, qm_ref):
    tq, tk = FOX_TQ, FOX_TK
    heads = 2 * FOX_PAIRS
    grp = pl.program_id(1)
    first = _lane_half((1, LANES))

    head_lane = lax.broadcasted_iota(jnp.int32, (1, LANES), 1)
    ones_row = (lax.broadcasted_iota(jnp.int32, (FOX_VROWS - HEAD_DIM, tk), 0) == 0).astype(BF16)
    for jt in range(SEQ // tk):
        rows = slice(jt * tk, (jt + 1) * tk)
        v_t = v_ref[rows, :].astype(F32).T.astype(BF16)
        ccol = ccol_ref[0, rows, :]
        for h in range(heads):
            vt_ref[jt, h * FOX_VROWS:h * FOX_VROWS + HEAD_DIM, :] = v_t[h * HEAD_DIM:(h + 1) * HEAD_DIM]
            vt_ref[jt, h * FOX_VROWS + HEAD_DIM:(h + 1) * FOX_VROWS, :] = ones_row
            col = jnp.sum(jnp.where(head_lane == grp * heads + h, ccol, 0.0), axis=1, keepdims=True)
            ckb_ref[h, rows, :] = jnp.broadcast_to(col * LOG2E, (tk, LANES))

    for qi in range(SEQ // tq):
        _fox_query_tile(qi, grp, first, q_ref, k_ref, crow_ref, o_ref, vt_ref, ckb_ref, s0_ref, s1_ref,
                        p0_ref, p1_ref, acc_ref, m_ref, a0_ref, a1_ref, cq_ref.at[qi & 1], qm_ref.at[qi & 1])


def _fox_query_tile(qi, grp, first, q_ref, k_ref, crow_ref, o_ref, vt_ref, ckb_ref, s0_ref, s1_ref,
                    p0_ref, p1_ref, acc_ref, m_ref, a0_ref, a1_ref, cq_ref, qm_ref):
    tq, tk = FOX_TQ, FOX_TK
    heads = 2 * FOX_PAIRS
    q_rows = slice(qi * tq, (qi + 1) * tq)

    for h in range(heads):
        q2 = q_ref[q_rows, (h // 2) * LANES:(h // 2 + 1) * LANES]
        qm_ref[h] = jnp.where(first if h % 2 == 0 else ~first, q2, jnp.zeros_like(q2))
        cq_ref[h] = crow_ref[0, grp * heads + h, qi:qi + 1, :] * LOG2E

    def scores(j, s_ref):
        row0 = _row_start(j, tk)
        for h in range(heads):
            kj = k_ref[pl.ds(row0, tk), (h // 2) * LANES:(h // 2 + 1) * LANES]
            s = lax.dot_general(kj, qm_ref[h], (((1,), (1,)), ((), ())), preferred_element_type=F32)
            ck = ckb_ref[h, pl.ds(row0, tk), :]
            s_ref[h] = s - jnp.concatenate([ck] * (tq // LANES), axis=1)

    def softmax(s_ref, p_ref, a_ref, mask):
        for h in range(heads):
            s = s_ref[h]
            if mask is not None:
                s = jnp.where(mask, s, NEG_INF)
            m_old = m_ref[h]
            cq = cq_ref[h]
            m_new = jnp.maximum(m_old, jnp.max(s, axis=0, keepdims=True) + cq)
            p_ref[h] = jnp.exp2(s + (cq - m_new)).astype(BF16)
            m_ref[h] = m_new
            a_ref[h] = jnp.exp2(m_old - m_new)

    def accumulate(j, p_ref, a_ref):
        tile = max(j, 0)
        for h in range(heads):
            vt = vt_ref[tile, h * FOX_VROWS:(h + 1) * FOX_VROWS, :]
            pv = jnp.dot(vt, p_ref[h], preferred_element_type=F32)
            acc_ref[h] = a_ref[h] * acc_ref[h] + pv

    def trip(i, mask_even, mask_odd, last):
        cur = i & 1
        even = 2 * i
        scores(even + 1, s1_ref)
        accumulate(even - 1, p1_ref.at[1 - cur], a1_ref.at[1 - cur])
        softmax(s0_ref.at[cur], p0_ref, a0_ref, mask_even)
        if not last:
            scores(even + 2, s0_ref.at[1 - cur])
        accumulate(even, p0_ref, a0_ref)
        softmax(s1_ref, p1_ref.at[cur], a1_ref.at[cur], mask_odd)
        if last:
            accumulate(even + 1, p1_ref.at[cur], a1_ref.at[cur])

    acc_ref[...] = jnp.zeros(acc_ref.shape, F32)
    m_ref[...] = jnp.full(m_ref.shape, NEG_INF, F32)
    p1_ref[1] = jnp.zeros(p1_ref.shape[1:], BF16)
    a1_ref[1] = jnp.ones(a1_ref.shape[1:], F32)
    scores(0, s0_ref.at[0])
    for i in range(qi):
        trip(i, None, None, last=False)
    kpos = lax.broadcasted_iota(jnp.int32, (tk, tq), 0)
    qpos = lax.broadcasted_iota(jnp.int32, (tk, tq), 1)
    trip(qi, kpos <= qpos, kpos + tk <= qpos, last=True)
    out_t = jnp.concatenate([acc_ref[h, :HEAD_DIM, :] / acc_ref[h, HEAD_DIM:HEAD_DIM + 1, :]
                             for h in range(heads)], axis=0)
    o_ref[q_rows, :] = out_t.T.astype(BF16)


def _fox_attention(proj, ccol, crow):
    tq, tk = FOX_TQ, FOX_TK
    nq = SEQ // tq
    heads = 2 * FOX_PAIRS
    width = FOX_PAIRS * LANES
    return pl.pallas_call(
        _fox_kernel,
        grid=(BATCH, N_PAIRS // FOX_PAIRS),
        in_specs=[
            pl.BlockSpec((SEQ, width), lambda b, g: (b, QB_BLK // FOX_PAIRS + g)),
            pl.BlockSpec((SEQ, width), lambda b, g: (b, KB_BLK // FOX_PAIRS + g)),
            pl.BlockSpec((SEQ, width), lambda b, g: (b, VB_BLK // FOX_PAIRS + g)),
            pl.BlockSpec((1, SEQ, LANES), lambda b, g: (b, 0, 0)),
            pl.BlockSpec((1, B_HEADS, nq, tq), lambda b, g: (b, 0, 0, 0)),
        ],
        out_specs=pl.BlockSpec((SEQ, width), lambda b, g: (b, g)),
        out_shape=jax.ShapeDtypeStruct((TOKENS, BRANCH_WIDTH), BF16),
        scratch_shapes=[pltpu.VMEM((SEQ // tk, heads * FOX_VROWS, tk), BF16),
                        pltpu.VMEM((heads, SEQ, LANES), F32),
                        pltpu.VMEM((2, heads, tk, tq), F32), pltpu.VMEM((heads, tk, tq), F32),
                        pltpu.VMEM((heads, tk, tq), BF16), pltpu.VMEM((2, heads, tk, tq), BF16),
                        pltpu.VMEM((heads, FOX_VROWS, tq), F32),
                        pltpu.VMEM((heads, 1, tq), F32),
                        pltpu.VMEM((heads, 1, tq), F32),
                        pltpu.VMEM((2, heads, 1, tq), F32),
                        pltpu.VMEM((2, heads, 1, tq), F32),
                        pltpu.VMEM((2, heads, tq, LANES), BF16)],
        compiler_params=_params(2),
        name="fox_attention",
    )(proj, proj, proj, ccol, crow.reshape(BATCH, B_HEADS, nq, tq))


def _swa_kernel(sink_ref, q_ref, k_ref, v_ref, o_ref, kd_ref, vd_ref,
                s0_ref, s1_ref, p0_ref, p1_ref, d0_ref, d1_ref, *, layer):
    rows = A_GROUP * CHUNK
    first = _lane_half((1, LANES))
    grp = lax.broadcasted_iota(jnp.int32, (rows, 1), 0) // CHUNK
    qi = lax.broadcasted_iota(jnp.int32, (rows, A_BAND), 0) % CHUNK
    si = lax.broadcasted_iota(jnp.int32, (rows, A_BAND), 1)
    sel_r = lax.broadcasted_iota(jnp.int32, (LANES, LANES), 0)
    sel_c = lax.broadcasted_iota(jnp.int32, (LANES, LANES), 1)

    slopes, sinks = [], []
    for kvh in range(A_KV_HEADS):
        sel = (sel_r == kvh * HEAD_DIM + sel_c % HEAD_DIM).astype(BF16)
        kd_ref[kvh] = jnp.dot(k_ref[...], sel, preferred_element_type=F32).astype(BF16)
        vd_ref[kvh] = jnp.dot(v_ref[...], sel, preferred_element_type=F32).astype(BF16)
        slope = jnp.zeros((rows, 1), F32)
        sink = jnp.zeros((rows, 1), F32)
        for g in range(A_GROUP):
            head = kvh * A_GROUP + g
            slope = jnp.where(grp == g, 2.0 ** -(head + 1) * LOG2E, slope)
            sink = jnp.where(grp == g, sink_ref[layer, head] * LOG2E, sink)
        slopes.append(slope)
        sinks.append(sink)

    def band_start(n):
        if isinstance(n, int):
            return max(n - A_PREV, 0) * CHUNK
        return _row_start(jnp.maximum(n - A_PREV, 0), CHUNK)

    def scores(n, s_ref, alibi):
        q0 = _row_start(n, CHUNK)
        for kvh in range(A_KV_HEADS):
            parts = []
            for g in range(A_GROUP):
                head = kvh * A_GROUP + g
                blk = q_ref[pl.ds(q0, CHUNK), (head // 2) * LANES:(head // 2 + 1) * LANES]
                parts.append(jnp.where(first if head % 2 == 0 else ~first, blk, jnp.zeros_like(blk)))
            qs = jnp.concatenate(parts, axis=0)
            kb = kd_ref[kvh, pl.ds(band_start(n), A_BAND), :]
            s = lax.dot_general(qs, kb, (((1,), (1,)), ((), ())), preferred_element_type=F32)
            s_ref[kvh] = s + alibi[kvh]

    def softmax(s_ref, p_ref, d_ref, valid):
        for kvh in range(A_KV_HEADS):
            s = s_ref[kvh] if valid is None else jnp.where(valid, s_ref[kvh], NEG_INF)
            m = jnp.maximum(jnp.max(s, axis=1, keepdims=True), sinks[kvh])
            p = jnp.exp2(s - m)
            denom = jnp.sum(p, axis=1, keepdims=True) + jnp.exp2(sinks[kvh] - m)
            d_ref[kvh] = jnp.broadcast_to(denom, (rows, LANES))
            p_ref[kvh] = p.astype(BF16)

    def emit(n, p_ref, d_ref):
        q0 = _row_start(n, CHUNK)
        for kvh in range(A_KV_HEADS):
            vb = vd_ref[kvh, pl.ds(band_start(n), A_BAND), :]
            r = jnp.dot(p_ref[kvh], vb, preferred_element_type=F32) / d_ref[kvh]
            for pr in range(A_GROUP // 2):
                even = r[(2 * pr) * CHUNK:(2 * pr + 1) * CHUNK]
                odd = r[(2 * pr + 1) * CHUNK:(2 * pr + 2) * CHUNK]
                col = (kvh * (A_GROUP // 2) + pr) * LANES
                o_ref[pl.ds(q0, CHUNK), col:col + LANES] = jnp.where(first, even, odd).astype(BF16)

    def alibi_for(dist):
        return tuple(-slope * jnp.abs(dist).astype(F32) for slope in slopes)

    alibi = alibi_for(A_PREV * CHUNK + qi - si)
    def trip(i, carry=0, *, first_trip=False, last_trip=False):
        cur = i & 1
        even = 2 * i
        if first_trip:
            scores(1, s1_ref, alibi_for(CHUNK + qi - si))
            softmax(s0_ref.at[cur], p0_ref, d0_ref, si // CHUNK <= 0)
        else:
            scores(even + 1, s1_ref, alibi)
            emit(even - 1, p1_ref.at[1 - cur], d1_ref.at[1 - cur])
            softmax(s0_ref.at[cur], p0_ref, d0_ref, None)
        if not last_trip:
            scores(even + 2, s0_ref.at[1 - cur], alibi)
        emit(even, p0_ref, d0_ref)
        softmax(s1_ref, p1_ref.at[cur], d1_ref.at[cur], si // CHUNK <= 1 if first_trip else None)
        if last_trip:
            emit(even + 1, p1_ref.at[cur], d1_ref.at[cur])
        return carry

    scores(0, s0_ref.at[0], alibi_for(qi - si))
    trip(0, first_trip=True)
    for i in range(1, N_CHUNKS // 2 - 1):
        trip(i)
    trip(N_CHUNKS // 2 - 1, last_trip=True)


def _swa_attention(proj, sinks, layer):
    return pl.pallas_call(
        functools.partial(_swa_kernel, layer=layer),
        grid=(BATCH,),
        in_specs=[
            pl.BlockSpec(memory_space=pltpu.SMEM),
            pl.BlockSpec((SEQ, A_HEADS * HEAD_DIM), lambda b: (b, QA_BLK // N_PAIRS)),
            pl.BlockSpec((SEQ, LANES), lambda b: (b, KA_BLK)),
            pl.BlockSpec((SEQ, LANES), lambda b: (b, VA_BLK)),
        ],
        out_specs=pl.BlockSpec((SEQ, BRANCH_WIDTH), lambda b: (b, 0)),
        out_shape=jax.ShapeDtypeStruct((TOKENS, BRANCH_WIDTH), BF16),
        scratch_shapes=[pltpu.VMEM((A_KV_HEADS, SEQ, LANES), BF16),
                        pltpu.VMEM((A_KV_HEADS, SEQ, LANES), BF16),
                        pltpu.VMEM((2, A_KV_HEADS, A_GROUP * CHUNK, A_BAND), F32),
                        pltpu.VMEM((A_KV_HEADS, A_GROUP * CHUNK, A_BAND), F32),
                        pltpu.VMEM((A_KV_HEADS, A_GROUP * CHUNK, A_BAND), BF16),
                        pltpu.VMEM((2, A_KV_HEADS, A_GROUP * CHUNK, A_BAND), BF16),
                        pltpu.VMEM((A_KV_HEADS, A_GROUP * CHUNK, LANES), F32),
                        pltpu.VMEM((2, A_KV_HEADS, A_GROUP * CHUNK, LANES), F32)],
        compiler_params=_params(1),
        name="swa_attention",
    )(sinks, proj, proj, proj)


def _chunked_kernel(q_ref, k_ref, v_ref, bias_ref, o_ref, kp_ref, vp_ref,
                    s0_ref, s1_ref, p0_ref, p1_ref, d0_ref, d1_ref):
    first = _lane_half((1, LANES))
    width = N_PAIRS * LANES
    kp_ref[0:C_PAD, :] = jnp.zeros((C_PAD, width), BF16)
    vp_ref[0:C_PAD, :] = jnp.zeros((C_PAD, width), BF16)
    kp_ref[C_PAD:, :] = k_ref[...]
    vp_ref[C_PAD:, :] = v_ref[...]
    si = lax.broadcasted_iota(jnp.int32, (2 * CHUNK, C_BAND), 1)

    def scores(n, s_ref):
        q0 = _row_start(n, CHUNK)
        for pair in range(N_PAIRS):
            cols = slice(pair * LANES, (pair + 1) * LANES)
            q2 = q_ref[pl.ds(q0, CHUNK), cols]
            zero = jnp.zeros_like(q2)
            qs = jnp.concatenate([jnp.where(first, q2, zero), jnp.where(first, zero, q2)], axis=0)
            kb = kp_ref[pl.ds(q0, C_BAND), cols]
            s = lax.dot_general(qs, kb, (((1,), (1,)), ((), ())), preferred_element_type=F32)
            s_ref[pair] = s + bias_ref[pair]

    def softmax(n, s_ref, p_ref, d_ref, masked):
        for pair in range(N_PAIRS):
            s = s_ref[pair]
            if masked:
                s = jnp.where(n * CHUNK + si >= C_PAD, s, NEG_INF)
            p = jnp.exp2(s - jnp.max(s, axis=1, keepdims=True))
            d_ref[pair] = jnp.sum(p, axis=1, keepdims=True)
            p_ref[pair] = p.astype(BF16)

    def emit(n, p_ref, d_ref):
        q0 = _row_start(n, CHUNK)
        for pair in range(N_PAIRS):
            cols = slice(pair * LANES, (pair + 1) * LANES)
            vb = vp_ref[pl.ds(q0, C_BAND), cols]
            r = jnp.dot(p_ref[pair], vb, preferred_element_type=F32) / d_ref[pair]
            o_ref[pl.ds(q0, CHUNK), cols] = jnp.where(first, r[:CHUNK], r[CHUNK:]).astype(BF16)

    def trip(i, carry=0, *, masked, first_trip=False, last_trip=False):
        cur = i & 1
        even = 2 * i
        scores(even + 1, s1_ref)
        if not first_trip:
            emit(even - 1, p1_ref.at[1 - cur], d1_ref.at[1 - cur])
        softmax(even, s0_ref.at[cur], p0_ref, d0_ref, masked)
        if not last_trip:
            scores(even + 2, s0_ref.at[1 - cur])
        emit(even, p0_ref, d0_ref)
        softmax(even + 1, s1_ref, p1_ref.at[cur], d1_ref.at[cur], masked)
        if last_trip:
            emit(even + 1, p1_ref.at[cur], d1_ref.at[cur])
        return carry

    masked_trips = C_PREV // 2
    scores(0, s0_ref.at[0])
    trip(0, masked=True, first_trip=True)
    lax.fori_loop(1, masked_trips, functools.partial(trip, masked=True), 0)
    lax.fori_loop(masked_trips, N_CHUNKS // 2 - 1, functools.partial(trip, masked=False), 0)
    trip(N_CHUNKS // 2 - 1, masked=False, last_trip=True)


def _chunked_attention(proj, bias, layer):
    width = N_PAIRS * LANES
    return pl.pallas_call(
        _chunked_kernel,
        grid=(BATCH,),
        in_specs=[
            pl.BlockSpec((SEQ, width), lambda b: (b, QC_BLK // N_PAIRS)),
            pl.BlockSpec((SEQ, width), lambda b: (b, KC_BLK // N_PAIRS)),
            pl.BlockSpec((SEQ, width), lambda b: (b, VC_BLK // N_PAIRS)),
            pl.BlockSpec((None, N_PAIRS, 2 * CHUNK, C_BAND), lambda b: (layer, 0, 0, 0)),
        ],
        out_specs=pl.BlockSpec((SEQ, width), lambda b: (b, 0)),
        out_shape=jax.ShapeDtypeStruct((TOKENS, BRANCH_WIDTH), BF16),
        scratch_shapes=[pltpu.VMEM((C_PAD + SEQ, width), BF16), pltpu.VMEM((C_PAD + SEQ, width), BF16),
                        pltpu.VMEM((2, N_PAIRS, 2 * CHUNK, C_BAND), F32),
                        pltpu.VMEM((N_PAIRS, 2 * CHUNK, C_BAND), F32),
                        pltpu.VMEM((N_PAIRS, 2 * CHUNK, C_BAND), BF16),
                        pltpu.VMEM((2, N_PAIRS, 2 * CHUNK, C_BAND), BF16),
                        pltpu.VMEM((N_PAIRS, 2 * CHUNK, 1), F32),
                        pltpu.VMEM((2, N_PAIRS, 2 * CHUNK, 1), F32)],
        compiler_params=_params(1),
        name="chunked_attention",
    )(proj, proj, proj, bias)


def _merge_kernel(x_ref, mod_ref, g_ref, oa_ref, ob_ref, oc_ref, wb_ref, wg_ref, wo_ref, out_ref,
                  merged_ref, *, tn):
    h = _rms_mod(x_ref[...], g_ref[...], mod_ref[0:1, :], mod_ref[1:2, :]).astype(BF16)
    branches = (oa_ref[...], ob_ref[...], oc_ref[...])
    for n in range(D_MODEL // tn):
        acc = None
        for k, o in enumerate(branches):
            y = jnp.dot(o, wb_ref[k, :, n * tn:(n + 1) * tn], preferred_element_type=F32)
            gate = jnp.dot(h, wg_ref[:, k * D_MODEL + n * tn:k * D_MODEL + (n + 1) * tn],
                           preferred_element_type=F32)
            term = jax.nn.sigmoid(gate) * y
            acc = term if acc is None else acc + term
        merged_ref[:, n * tn:(n + 1) * tn] = acc.astype(BF16)
    merged = merged_ref[...]
    for n in range(D_MODEL // tn):
        sl = slice(n * tn, (n + 1) * tn)
        out = jnp.dot(merged, wo_ref[:, sl], preferred_element_type=F32)
        out_ref[:, sl] = x_ref[:, sl] + mod_ref[2:3, sl] * out


def _merge(x, mod, gain, o_a, o_b, o_c, w_branch, w_gate, w_out, layer):
    steps_per_batch = SEQ // TM
    row = lambda i: (i, 0)
    return pl.pallas_call(
        functools.partial(_merge_kernel, tn=256),
        grid=(TOKENS // TM,),
        in_specs=[
            pl.BlockSpec((TM, D_MODEL), row),
            pl.BlockSpec((None, None, 6, D_MODEL), lambda i: (layer, i // steps_per_batch, 0, 0)),
            pl.BlockSpec((None, 1, D_MODEL), lambda i: (layer, 0, 0)),
            pl.BlockSpec((TM, BRANCH_WIDTH), row),
            pl.BlockSpec((TM, BRANCH_WIDTH), row),
            pl.BlockSpec((TM, BRANCH_WIDTH), row),
            pl.BlockSpec((None, 3, BRANCH_WIDTH, D_MODEL), lambda i: (layer, 0, 0, 0)),
            pl.BlockSpec((None, D_MODEL, 3 * D_MODEL), lambda i: (layer, 0, 0)),
            pl.BlockSpec((None, D_MODEL, D_MODEL), lambda i: (layer, 0, 0)),
        ],
        out_specs=pl.BlockSpec((TM, D_MODEL), row),
        out_shape=jax.ShapeDtypeStruct((TOKENS, D_MODEL), F32),
        scratch_shapes=[pltpu.VMEM((TM, D_MODEL), BF16)],
        compiler_params=_params(1),
        name="merge_out",
    )(x, mod, gain, o_a, o_b, o_c, w_branch, w_gate, w_out)


def _ffn_kernel(x_ref, mod_ref, g_ref, gf_ref, wi_ref, wo_ref, out_ref, act_ref, *, tf, tn, final):
    h = _rms_mod(x_ref[...], g_ref[...], mod_ref[3:4, :], mod_ref[4:5, :]).astype(BF16)
    for c in range(FFN_HIDDEN // tf):
        gate = jnp.dot(h, wi_ref[:, c * tf:(c + 1) * tf], preferred_element_type=F32)
        up = jnp.dot(h, wi_ref[:, FFN_HIDDEN + c * tf:FFN_HIDDEN + (c + 1) * tf],
                     preferred_element_type=F32)
        act_ref[:, c * tf:(c + 1) * tf] = (gate * jax.nn.sigmoid(gate) * up).astype(BF16)
    act = act_ref[...]
    for n in range(D_MODEL // tn):
        sl = slice(n * tn, (n + 1) * tn)
        out = jnp.dot(act, wo_ref[:, sl], preferred_element_type=F32)
        out_ref[:, sl] = x_ref[:, sl] + mod_ref[5:6, sl] * out
    if final:
        y = out_ref[...]
        ms = jnp.mean(y * y, axis=-1, keepdims=True)
        out_ref[...] = y * lax.rsqrt(ms + EPS) * gf_ref[...]


def _ffn(x, mod, gain, final_gain, w_ffn_in, w_ffn_out, layer, final):
    steps_per_batch = SEQ // TM
    row = lambda i: (i, 0)
    return pl.pallas_call(
        functools.partial(_ffn_kernel, tf=256, tn=256, final=final),
        grid=(TOKENS // TM,),
        in_specs=[
            pl.BlockSpec((TM, D_MODEL), row),
            pl.BlockSpec((None, None, 6, D_MODEL), lambda i: (layer, i // steps_per_batch, 0, 0)),
            pl.BlockSpec((None, 1, D_MODEL), lambda i: (layer, 0, 0)),
            pl.BlockSpec((1, D_MODEL), lambda i: (0, 0)),
            pl.BlockSpec((None, D_MODEL, 2 * FFN_HIDDEN), lambda i: (layer, 0, 0)),
            pl.BlockSpec((None, FFN_HIDDEN, D_MODEL), lambda i: (layer, 0, 0)),
        ],
        out_specs=pl.BlockSpec((TM, D_MODEL), row),
        out_shape=jax.ShapeDtypeStruct((TOKENS, D_MODEL), F32),
        scratch_shapes=[pltpu.VMEM((TM, FFN_HIDDEN), BF16)],
        compiler_params=_params(1),
        name="ffn",
    )(x, mod, gain, final_gain, w_ffn_in, w_ffn_out)


def kernel(x, c, norm_mix_g, norm_ffn_g, w_ada, b_ada, w_in, b_forget, sinks, rel_bias,
           w_branch, w_out, w_ffn_in, w_ffn_out, final_norm_g):
    scale = HEAD_DIM ** -0.5
    w_in_t = jnp.swapaxes(w_in, 1, 2)
    w_qkv = _pack_w_in(w_in_t, (IN_B0, IN_C0, IN_A0), (IN_FB0 - IN_B0, IN_GATE0 - IN_C0, IN_B0 - IN_A0),
                       (scale * LOG2E,) * 3)
    w_gate = _pack_w_in(w_in_t, (IN_GATE0,), (N_IN_COLS - IN_GATE0,), (None,))
    w_fb = jnp.pad(w_in[:, :, IN_FB0:IN_FB0 + B_HEADS], ((0, 0), (0, 0), (0, LANES - B_HEADS))).astype(BF16)
    w_branch_b = w_branch.astype(BF16)
    w_out_b = w_out.astype(BF16)
    w_ffn_in_b = w_ffn_in.astype(BF16)
    w_ffn_out_b = w_ffn_out.astype(BF16)
    b_forget_rows = jnp.pad(b_forget, ((0, 0), (0, LANES - B_HEADS))).reshape(DEPTH, 1, LANES)
    gain_mix = norm_mix_g.reshape(DEPTH, 1, D_MODEL)
    gain_ffn = norm_ffn_g.reshape(DEPTH, 1, D_MODEL)
    gain_final = final_norm_g.reshape(1, D_MODEL)

    mod = _ada_mod(c, w_ada, b_ada)
    bias_c = _rel_bias(rel_bias)

    xt = x.reshape(TOKENS, D_MODEL)
    for layer in range(DEPTH):
        proj, fb = _in_proj(xt, mod, gain_mix, w_qkv, w_fb, layer)
        ccol, crow = _forget_cumsum(fb, b_forget_rows[layer])
        o_a = _swa_attention(proj, sinks, layer)
        o_b = _fox_attention(proj, ccol, crow)
        o_c = _chunked_attention(proj, bias_c, layer)
        xt = _merge(xt, mod, gain_mix, o_a, o_b, o_c, w_branch_b, w_gate, w_out_b, layer)
        xt = _ffn(xt, mod, gain_ffn, gain_final, w_ffn_in_b, w_ffn_out_b, layer, layer == DEPTH - 1)
    return xt.reshape(BATCH, SEQ, D_MODEL)
```

```python
import functools

import jax
import jax.numpy as jnp
from jax import lax
from jax.experimental import pallas as pl
from jax.experimental.pallas import tpu as pltpu

F32 = jnp.float32
BF16 = jnp.bfloat16

D_MODEL = 1024
BATCH = 8
SEQ = 2048
TOKENS = BATCH * SEQ
DEPTH = 2
CHUNK = 64
HEAD_DIM = 64
EPS = 1e-6
NEG_INF = -1e30
LOG2E = 1.4426950408889634

A_HEADS = 8
A_KV_HEADS = 2
A_GROUP = A_HEADS // A_KV_HEADS
A_PREV = 2
A_BAND = (A_PREV + 1) * CHUNK
B_HEADS = 8
C_HEADS = 8
C_PREV = 8
C_PAD = C_PREV * CHUNK
C_BAND = (C_PREV + 1) * CHUNK
REL_CLIP = 128
N_REL = 2 * REL_CLIP + 1
BRANCH_WIDTH = 512
FFN_HIDDEN = 2816
N_CHUNKS = SEQ // CHUNK
N_IN_COLS = 6920

LANES = 128
N_PAIRS = 4

PROJ_COLS = 3840
QB_BLK, KB_BLK, VB_BLK = 0, 4, 8
QC_BLK, KC_BLK, VC_BLK = 12, 16, 20
QA_BLK, KA_BLK, VA_BLK = 24, 28, 29

TM = 512
FOX_TQ = 512
FOX_TK = 256
FOX_VROWS = HEAD_DIM + 16
FOX_PAIRS = 2
SOFTMAX_ROWS = 32
CUM_BLK = 256

VMEM_LIMIT = 56 * 1024 * 1024


def _params(n_axes):
    return pltpu.CompilerParams(dimension_semantics=("arbitrary",) * n_axes,
                                vmem_limit_bytes=VMEM_LIMIT)


def _rms_mod(x, g, shift, scale):
    ms = jnp.mean(x * x, axis=-1, keepdims=True)
    y = x * lax.rsqrt(ms + EPS) * g
    return y * (1.0 + scale) + shift


def _row_start(index, size):
    if isinstance(index, int):
        return index * size
    return pl.multiple_of(index * size, size)


def _lane_half(shape):
    return lax.broadcasted_iota(jnp.int32, shape, len(shape) - 1) < HEAD_DIM


def _ada_kernel(c_ref, w_ref, b_ref, o_ref):
    c = c_ref[...]
    cond = c * jax.nn.sigmoid(c)
    o_ref[...] = jnp.dot(cond.astype(BF16), w_ref[...].astype(BF16),
                         preferred_element_type=F32) + b_ref[...]


def _ada_mod(c, w_ada, b_ada):
    n_blk = 6
    out = pl.pallas_call(
        _ada_kernel,
        grid=(DEPTH, n_blk),
        in_specs=[
            pl.BlockSpec((BATCH, D_MODEL), lambda l, j: (0, 0)),
            pl.BlockSpec((None, D_MODEL, D_MODEL), lambda l, j: (l, 0, j)),
            pl.BlockSpec((None, 1, D_MODEL), lambda l, j: (l, 0, j)),
        ],
        out_specs=pl.BlockSpec((None, BATCH, D_MODEL), lambda l, j: (l, 0, j)),
        out_shape=jax.ShapeDtypeStruct((DEPTH, BATCH, n_blk * D_MODEL), F32),
        compiler_params=_params(2),
        name="ada_mod",
    )(c, w_ada, b_ada.reshape(DEPTH, 1, n_blk * D_MODEL))
    return out.reshape(DEPTH, BATCH, n_blk, D_MODEL)


def _relbias_kernel(rev_ref, o_ref):
    near = C_PAD - REL_CLIP
    width = 2 * REL_CLIP
    q = lax.broadcasted_iota(jnp.int32, (CHUNK, width), 0)
    c = lax.broadcasted_iota(jnp.int32, (CHUNK, width), 1)
    for h in range(C_HEADS):
        rev = rev_ref[h:h + 1, :] * LOG2E
        far = rev[:, 0:1]
        rolled = pltpu.roll(jnp.broadcast_to(rev, (CHUNK, width)), 0, 1, stride=1, stride_axis=0)
        o_ref[h, :, 0:near] = jnp.broadcast_to(far, (CHUNK, near))
        o_ref[h, :, near:C_BAND] = jnp.where(c >= q, rolled, far)[:, :C_BAND - near]


def _rel_bias(rel_bias):
    rev = rel_bias[:, :, ::-1][:, :, :2 * REL_CLIP]
    out = pl.pallas_call(
        _relbias_kernel,
        grid=(DEPTH,),
        in_specs=[pl.BlockSpec((None, C_HEADS, 2 * REL_CLIP), lambda l: (l, 0, 0))],
        out_specs=pl.BlockSpec((None, C_HEADS, CHUNK, C_BAND), lambda l: (l, 0, 0, 0)),
        out_shape=jax.ShapeDtypeStruct((DEPTH, C_HEADS, CHUNK, C_BAND), F32),
        compiler_params=_params(1),
        name="rel_bias",
    )(rev)
    return out.reshape(DEPTH, N_PAIRS, 2 * CHUNK, C_BAND)


IN_A0, IN_B0, IN_FB0, IN_C0, IN_GATE0 = 0, 768, 2304, 2312, 3848
PACK_ROWS = 384


def _pack_kernel(wt_ref, o_ref, *, region_starts, region_steps, q_scales):
    j = pl.program_id(1)
    w = wt_ref[0]
    first_step = 0
    row = lax.broadcasted_iota(jnp.int32, (PACK_ROWS, 1), 0)
    scale = jnp.ones((PACK_ROWS, 1), F32)
    for steps, q_scale in zip(region_steps, q_scales):
        if q_scale is not None:
            in_region = (j >= first_step) & (j < first_step + steps)
            col_in_region = (j - first_step) * PACK_ROWS + row
            scale = jnp.where(in_region & (col_in_region < A_HEADS * HEAD_DIM), q_scale, scale)
        first_step += steps
    o_ref[...] = (w * scale).T.astype(BF16)


def _pack_w_in(w_in_t, region_starts, region_widths, q_scales):
    region_steps = tuple(w // PACK_ROWS for w in region_widths)

    def source_row(j):
        row = jnp.int32(0)
        first_step = 0
        for start, steps in zip(region_starts, region_steps):
            row = jnp.where(j >= first_step, start + (j - first_step) * PACK_ROWS, row)
            first_step += steps
        return pl.multiple_of(row, 8)

    return pl.pallas_call(
        functools.partial(_pack_kernel, region_starts=region_starts, region_steps=region_steps, q_scales=q_scales),
        grid=(DEPTH, sum(region_steps)),
        in_specs=[pl.BlockSpec((pl.Element(1), pl.Element(PACK_ROWS), pl.Element(D_MODEL)),
                               lambda l, j: (l, source_row(j), 0))],
        out_specs=pl.BlockSpec((None, D_MODEL, PACK_ROWS), lambda l, j: (l, 0, j)),
        out_shape=jax.ShapeDtypeStruct((DEPTH, D_MODEL, sum(region_widths)), BF16),
        compiler_params=_params(2),
        name="pack_w_in",
    )(w_in_t)


def _inproj_kernel(x_ref, mod_ref, g_ref, w_ref, wfb_ref, proj_ref, fb_ref, *, tn):
    h = _rms_mod(x_ref[...], g_ref[...], mod_ref[0:1, :], mod_ref[1:2, :]).astype(BF16)
    for j in range(PROJ_COLS // tn):
        sl = slice(j * tn, (j + 1) * tn)
        proj_ref[:, sl] = jnp.dot(h, w_ref[:, sl], preferred_element_type=F32).astype(BF16)
    fb_ref[...] = jnp.dot(h, wfb_ref[...], preferred_element_type=F32)


def _in_proj(x, mod, gain, w_qkv, w_fb, layer):
    tn = 768
    steps_per_batch = SEQ // TM
    return pl.pallas_call(
        functools.partial(_inproj_kernel, tn=tn),
        grid=(TOKENS // TM,),
        in_specs=[
            pl.BlockSpec((TM, D_MODEL), lambda i: (i, 0)),
            pl.BlockSpec((None, None, 6, D_MODEL), lambda i: (layer, i // steps_per_batch, 0, 0)),
            pl.BlockSpec((None, 1, D_MODEL), lambda i: (layer, 0, 0)),
            pl.BlockSpec((None, D_MODEL, PROJ_COLS), lambda i: (layer, 0, 0)),
            pl.BlockSpec((None, D_MODEL, LANES), lambda i: (layer, 0, 0)),
        ],
        out_specs=[
            pl.BlockSpec((TM, PROJ_COLS), lambda i: (i, 0)),
            pl.BlockSpec((TM, LANES), lambda i: (i, 0)),
        ],
        out_shape=[
            jax.ShapeDtypeStruct((TOKENS, PROJ_COLS), BF16),
            jax.ShapeDtypeStruct((TOKENS, LANES), F32),
        ],
        compiler_params=_params(1),
        name="in_proj",
    )(x, mod, gain, w_qkv, w_fb)


def _cumsum_kernel(fb_ref, bias_ref, col_ref, row_ref):
    r = lax.broadcasted_iota(jnp.int32, (CUM_BLK, CUM_BLK), 0)
    c = lax.broadcasted_iota(jnp.int32, (CUM_BLK, CUM_BLK), 1)
    tri = (r >= c).astype(BF16)
    carry = jnp.zeros((1, LANES), F32)
    for blk in range(SEQ // CUM_BLK):
        rows = slice(blk * CUM_BLK, (blk + 1) * CUM_BLK)
        z = fb_ref[0, rows, :] + bias_ref[...]
        log_f = jnp.minimum(z, 0.0) - jnp.log1p(jnp.exp(-jnp.abs(z)))
        hi = log_f.astype(BF16)
        rest = log_f - hi.astype(F32)
        mid = rest.astype(BF16)
        lo = (rest - mid.astype(F32)).astype(BF16)
        cum = (jnp.dot(tri, hi, preferred_element_type=F32) + jnp.dot(tri, mid, preferred_element_type=F32)
               + jnp.dot(tri, lo, preferred_element_type=F32)) + carry
        col_ref[0, rows, :] = cum
        row_ref[0, :, rows] = cum.T[:B_HEADS, :]
        carry = cum[CUM_BLK - 1:CUM_BLK, :]


def _forget_cumsum(fb, b_forget_row):
    return pl.pallas_call(
        _cumsum_kernel,
        grid=(BATCH,),
        in_specs=[
            pl.BlockSpec((1, SEQ, LANES), lambda b: (b, 0, 0)),
            pl.BlockSpec((1, LANES), lambda b: (0, 0)),
        ],
        out_specs=[
            pl.BlockSpec((1, SEQ, LANES), lambda b: (b, 0, 0)),
            pl.BlockSpec((1, B_HEADS, SEQ), lambda b: (b, 0, 0)),
        ],
        out_shape=[
            jax.ShapeDtypeStruct((BATCH, SEQ, LANES), F32),
            jax.ShapeDtypeStruct((BATCH, B_HEADS, SEQ), F32),
        ],
        compiler_params=_params(1),
        name="forget_cumsum",
    )(fb.reshape(BATCH, SEQ, LANES), b_forget_row)


def _fox_kernel(q_ref, k_ref, v_ref, ccol_ref, crow_ref, o_ref, vt_ref, ckb_ref,
                s0_ref, s1_ref, p0_ref, p1_ref, acc_ref, m_ref, a0_ref, a1_ref, cq_ref, qm_ref):
    tq, tk = FOX_TQ, FOX_TK
    heads = 2 * FOX_PAIRS
    grp = pl.program_id(1)
    first = _lane_half((1, LANES))

    head_lane = lax.broadcasted_iota(jnp.int32, (1, LANES), 1)
    ones_row = (lax.broadcasted_iota(jnp.int32, (FOX_VROWS - HEAD_DIM, tk), 0) == 0).astype(BF16)
    for jt in range(SEQ // tk):
        rows = slice(jt * tk, (jt + 1) * tk)
        v_t = v_ref[rows, :].astype(F32).T.astype(BF16)
        ccol = ccol_ref[0, rows, :]
        for h in range(heads):
            vt_ref[jt, h * FOX_VROWS:h * FOX_VROWS + HEAD_DIM, :] = v_t[h * HEAD_DIM:(h + 1) * HEAD_DIM]
            vt_ref[jt, h * FOX_VROWS + HEAD_DIM:(h + 1) * FOX_VROWS, :] = ones_row
            col = jnp.sum(jnp.where(head_lane == grp * heads + h, ccol, 0.0), axis=1, keepdims=True)
            ckb_ref[h, rows, :] = jnp.broadcast_to(col * LOG2E, (tk, LANES))

    for qi in range(SEQ // tq):
        _fox_query_tile(qi, grp, first, q_ref, k_ref, crow_ref, o_ref, vt_ref, ckb_ref, s0_ref, s1_ref,
                        p0_ref, p1_ref, acc_ref, m_ref, a0_ref, a1_ref, cq_ref.at[qi & 1], qm_ref.at[qi & 1])


def _fox_query_tile(qi, grp, first, q_ref, k_ref, crow_ref, o_ref, vt_ref, ckb_ref, s0_ref, s1_ref,
                    p0_ref, p1_ref, acc_ref, m_ref, a0_ref, a1_ref, cq_ref, qm_ref):
    tq, tk = FOX_TQ, FOX_TK
    heads = 2 * FOX_PAIRS
    q_rows = slice(qi * tq, (qi + 1) * tq)

    for h in range(heads):
        q2 = q_ref[q_rows, (h // 2) * LANES:(h // 2 + 1) * LANES]
        qm_ref[h] = jnp.where(first if h % 2 == 0 else ~first, q2, jnp.zeros_like(q2))
        cq_ref[h] = crow_ref[0, grp * heads + h, qi:qi + 1, :] * LOG2E

    whole, upper = slice(0, tq), slice(tq // 2, tq)

    def scores(j, s_ref, q=whole):
        row0 = _row_start(j, tk)
        for h in range(heads):
            kj = k_ref[pl.ds(row0, tk), (h // 2) * LANES:(h // 2 + 1) * LANES]
            s = lax.dot_general(kj, qm_ref[h, q, :], (((1,), (1,)), ((), ())), preferred_element_type=F32)
            ck = ckb_ref[h, pl.ds(row0, tk), :]
            s_ref[h, :, q] = s - jnp.concatenate([ck] * ((q.stop - q.start) // LANES), axis=1)

    def softmax(s_ref, p_ref, a_ref, mask, q=whole):
        for h in range(heads):
            s = s_ref[h, :, q]
            if mask is not None:
                s = jnp.where(mask, s, NEG_INF)
            m_old = m_ref[h, :, q]
            cq = cq_ref[h, :, q]
            m_new = jnp.maximum(m_old, jnp.max(s, axis=0, keepdims=True) + cq)
            p_ref[h, :, q] = jnp.exp2(s + (cq - m_new)).astype(BF16)
            m_ref[h, :, q] = m_new
            a_ref[h, :, q] = jnp.exp2(m_old - m_new)

    def accumulate(j, p_ref, a_ref, q=whole):
        tile = max(j, 0)
        for h in range(heads):
            vt = vt_ref[tile, h * FOX_VROWS:(h + 1) * FOX_VROWS, :]
            pv = jnp.dot(vt, p_ref[h, :, q], preferred_element_type=F32)
            acc_ref[h, :, q] = a_ref[h, :, q] * acc_ref[h, :, q] + pv

    def trip(i, last):
        cur = i & 1
        even = 2 * i
        odd_q = upper if last else whole
        mask_even = mask_odd = None
        if last:
            causal = lambda width: (lax.broadcasted_iota(jnp.int32, (tk, width), 0)
                                    <= lax.broadcasted_iota(jnp.int32, (tk, width), 1))
            mask_even = causal(tq)
            mask_odd = causal(tq // 2)
        scores(even + 1, s1_ref, odd_q)
        accumulate(even - 1, p1_ref.at[1 - cur], a1_ref.at[1 - cur])
        softmax(s0_ref.at[cur], p0_ref, a0_ref, mask_even)
        if not last:
            scores(even + 2, s0_ref.at[1 - cur])
        accumulate(even, p0_ref, a0_ref)
        softmax(s1_ref, p1_ref.at[cur], a1_ref.at[cur], mask_odd, odd_q)
        if last:
            accumulate(even + 1, p1_ref.at[cur], a1_ref.at[cur], odd_q)

    acc_ref[...] = jnp.zeros(acc_ref.shape, F32)
    m_ref[...] = jnp.full(m_ref.shape, NEG_INF, F32)
    p1_ref[1] = jnp.zeros(p1_ref.shape[1:], BF16)
    a1_ref[1] = jnp.ones(a1_ref.shape[1:], F32)
    scores(0, s0_ref.at[0])
    for i in range(qi):
        trip(i, last=False)
    trip(qi, last=True)
    out_t = jnp.concatenate([acc_ref[h, :HEAD_DIM, :] / acc_ref[h, HEAD_DIM:HEAD_DIM + 1, :]
                             for h in range(heads)], axis=0)
    o_ref[q_rows, :] = out_t.T.astype(BF16)


def _fox_attention(proj, ccol, crow):
    tq, tk = FOX_TQ, FOX_TK
    nq = SEQ // tq
    heads = 2 * FOX_PAIRS
    width = FOX_PAIRS * LANES
    return pl.pallas_call(
        _fox_kernel,
        grid=(BATCH, N_PAIRS // FOX_PAIRS),
        in_specs=[
            pl.BlockSpec((SEQ, width), lambda b, g: (b, QB_BLK // FOX_PAIRS + g)),
            pl.BlockSpec((SEQ, width), lambda b, g: (b, KB_BLK // FOX_PAIRS + g)),
            pl.BlockSpec((SEQ, width), lambda b, g: (b, VB_BLK // FOX_PAIRS + g)),
            pl.BlockSpec((1, SEQ, LANES), lambda b, g: (b, 0, 0)),
            pl.BlockSpec((1, B_HEADS, nq, tq), lambda b, g: (b, 0, 0, 0)),
        ],
        out_specs=pl.BlockSpec((SEQ, width), lambda b, g: (b, g)),
        out_shape=jax.ShapeDtypeStruct((TOKENS, BRANCH_WIDTH), BF16),
        scratch_shapes=[pltpu.VMEM((SEQ // tk, heads * FOX_VROWS, tk), BF16),
                        pltpu.VMEM((heads, SEQ, LANES), F32),
                        pltpu.VMEM((2, heads, tk, tq), F32), pltpu.VMEM((heads, tk, tq), F32),
                        pltpu.VMEM((heads, tk, tq), BF16), pltpu.VMEM((2, heads, tk, tq), BF16),
                        pltpu.VMEM((heads, FOX_VROWS, tq), F32),
                        pltpu.VMEM((heads, 1, tq), F32),
                        pltpu.VMEM((heads, 1, tq), F32),
                        pltpu.VMEM((2, heads, 1, tq), F32),
                        pltpu.VMEM((2, heads, 1, tq), F32),
                        pltpu.VMEM((2, heads, tq, LANES), BF16)],
        compiler_params=_params(2),
        name="fox_attention",
    )(proj, proj, proj, ccol, crow.reshape(BATCH, B_HEADS, nq, tq))


def _swa_kernel(sink_ref, q_ref, k_ref, v_ref, o_ref, kd_ref, vd_ref,
                s0_ref, s1_ref, p0_ref, p1_ref, d0_ref, d1_ref, *, layer):
    rows = A_GROUP * CHUNK
    first = _lane_half((1, LANES))
    grp = lax.broadcasted_iota(jnp.int32, (rows, 1), 0) // CHUNK
    qi = lax.broadcasted_iota(jnp.int32, (rows, A_BAND), 0) % CHUNK
    si = lax.broadcasted_iota(jnp.int32, (rows, A_BAND), 1)
    sel_r = lax.broadcasted_iota(jnp.int32, (LANES, LANES), 0)
    sel_c = lax.broadcasted_iota(jnp.int32, (LANES, LANES), 1)

    slopes = []
    for kvh in range(A_KV_HEADS):
        sel = (sel_r == kvh * HEAD_DIM + sel_c % HEAD_DIM).astype(BF16)
        kd_ref[kvh] = jnp.dot(k_ref[...], sel, preferred_element_type=F32).astype(BF16)
        vd_ref[kvh] = jnp.dot(v_ref[...], sel, preferred_element_type=F32).astype(BF16)
        slope = jnp.zeros((rows, 1), F32)
        for g in range(A_GROUP):
            slope = jnp.where(grp == g, 2.0 ** -(kvh * A_GROUP + g + 1) * LOG2E, slope)
        slopes.append(slope)

    def band_start(n):
        if isinstance(n, int):
            return max(n - A_PREV, 0) * CHUNK
        return _row_start(jnp.maximum(n - A_PREV, 0), CHUNK)

    def scores(n, s_ref, alibi):
        q0 = _row_start(n, CHUNK)
        for kvh in range(A_KV_HEADS):
            parts = []
            for g in range(A_GROUP):
                head = kvh * A_GROUP + g
                blk = q_ref[pl.ds(q0, CHUNK), (head // 2) * LANES:(head // 2 + 1) * LANES]
                parts.append(jnp.where(first if head % 2 == 0 else ~first, blk, jnp.zeros_like(blk)))
            qs = jnp.concatenate(parts, axis=0)
            kb = kd_ref[kvh, pl.ds(band_start(n), A_BAND), :]
            s = lax.dot_general(qs, kb, (((1,), (1,)), ((), ())), preferred_element_type=F32)
            s_ref[kvh] = s + alibi[kvh]

    def softmax(s_ref, p_ref, d_ref, last_key_chunk):
        key_chunk = lax.broadcasted_iota(jnp.int32, (SOFTMAX_ROWS, A_BAND), 1) // CHUNK
        for kvh in range(A_KV_HEADS):
            for r0 in range(0, rows, SOFTMAX_ROWS):
                blk = slice(r0, r0 + SOFTMAX_ROWS)
                sink = sink_ref[layer, kvh * A_GROUP + r0 // CHUNK] * LOG2E
                s = s_ref[kvh, blk, :]
                if last_key_chunk is not None:
                    s = jnp.where(key_chunk <= last_key_chunk, s, NEG_INF)
                m = jnp.maximum(jnp.max(s, axis=1, keepdims=True), sink)
                p = jnp.exp2(s - m)
                denom = jnp.sum(p, axis=1, keepdims=True) + jnp.exp2(sink - m)
                d_ref[kvh, blk, :] = jnp.broadcast_to(denom, (SOFTMAX_ROWS, LANES))
                p_ref[kvh, blk, :] = p.astype(BF16)

    def emit(n, p_ref, d_ref):
        q0 = _row_start(n, CHUNK)
        for kvh in range(A_KV_HEADS):
            vb = vd_ref[kvh, pl.ds(band_start(n), A_BAND), :]
            r = jnp.dot(p_ref[kvh], vb, preferred_element_type=F32) / d_ref[kvh]
            for pr in range(A_GROUP // 2):
                even = r[(2 * pr) * CHUNK:(2 * pr + 1) * CHUNK]
                odd = r[(2 * pr + 1) * CHUNK:(2 * pr + 2) * CHUNK]
                col = (kvh * (A_GROUP // 2) + pr) * LANES
                o_ref[pl.ds(q0, CHUNK), col:col + LANES] = jnp.where(first, even, odd).astype(BF16)

    def alibi_for(dist):
        return tuple(-slope * jnp.abs(dist).astype(F32) for slope in slopes)

    alibi = alibi_for(A_PREV * CHUNK + qi - si)
    def trip(i, carry=0, *, first_trip=False, last_trip=False):
        cur = i & 1
        even = 2 * i
        if first_trip:
            scores(1, s1_ref, alibi_for(CHUNK + qi - si))
            softmax(s0_ref.at[cur], p0_ref, d0_ref, 0)
        else:
            scores(even + 1, s1_ref, alibi)
            emit(even - 1, p1_ref.at[1 - cur], d1_ref.at[1 - cur])
            softmax(s0_ref.at[cur], p0_ref, d0_ref, None)
        if not last_trip:
            scores(even + 2, s0_ref.at[1 - cur], alibi)
        emit(even, p0_ref, d0_ref)
        softmax(s1_ref, p1_ref.at[cur], d1_ref.at[cur], 1 if first_trip else None)
        if last_trip:
            emit(even + 1, p1_ref.at[cur], d1_ref.at[cur])
        return carry

    scores(0, s0_ref.at[0], alibi_for(qi - si))
    trip(0, first_trip=True)
    for i in range(1, N_CHUNKS // 2 - 1):
        trip(i)
    trip(N_CHUNKS // 2 - 1, last_trip=True)


def _swa_attention(proj, sinks, layer):
    return pl.pallas_call(
        functools.partial(_swa_kernel, layer=layer),
        grid=(BATCH,),
        in_specs=[
            pl.BlockSpec(memory_space=pltpu.SMEM),
            pl.BlockSpec((SEQ, A_HEADS * HEAD_DIM), lambda b: (b, QA_BLK // N_PAIRS)),
            pl.BlockSpec((SEQ, LANES), lambda b: (b, KA_BLK)),
            pl.BlockSpec((SEQ, LANES), lambda b: (b, VA_BLK)),
        ],
        out_specs=pl.BlockSpec((SEQ, BRANCH_WIDTH), lambda b: (b, 0)),
        out_shape=jax.ShapeDtypeStruct((TOKENS, BRANCH_WIDTH), BF16),
        scratch_shapes=[pltpu.VMEM((A_KV_HEADS, SEQ, LANES), BF16),
                        pltpu.VMEM((A_KV_HEADS, SEQ, LANES), BF16),
                        pltpu.VMEM((2, A_KV_HEADS, A_GROUP * CHUNK, A_BAND), F32),
                        pltpu.VMEM((A_KV_HEADS, A_GROUP * CHUNK, A_BAND), F32),
                        pltpu.VMEM((A_KV_HEADS, A_GROUP * CHUNK, A_BAND), BF16),
                        pltpu.VMEM((2, A_KV_HEADS, A_GROUP * CHUNK, A_BAND), BF16),
                        pltpu.VMEM((A_KV_HEADS, A_GROUP * CHUNK, LANES), F32),
                        pltpu.VMEM((2, A_KV_HEADS, A_GROUP * CHUNK, LANES), F32)],
        compiler_params=_params(1),
        name="swa_attention",
    )(sinks, proj, proj, proj)


def _chunked_kernel(q_ref, k_ref, v_ref, bias_ref, o_ref, kp_ref, vp_ref,
                    s0_ref, s1_ref, p0_ref, p1_ref, d0_ref, d1_ref):
    first = _lane_half((1, LANES))
    width = N_PAIRS * LANES
    kp_ref[0:C_PAD, :] = jnp.zeros((C_PAD, width), BF16)
    vp_ref[0:C_PAD, :] = jnp.zeros((C_PAD, width), BF16)
    kp_ref[C_PAD:, :] = k_ref[...]
    vp_ref[C_PAD:, :] = v_ref[...]
    si = lax.broadcasted_iota(jnp.int32, (2 * CHUNK, C_BAND), 1)

    def scores(n, s_ref):
        q0 = _row_start(n, CHUNK)
        for pair in range(N_PAIRS):
            cols = slice(pair * LANES, (pair + 1) * LANES)
            q2 = q_ref[pl.ds(q0, CHUNK), cols]
            zero = jnp.zeros_like(q2)
            qs = jnp.concatenate([jnp.where(first, q2, zero), jnp.where(first, zero, q2)], axis=0)
            kb = kp_ref[pl.ds(q0, C_BAND), cols]
            s = lax.dot_general(qs, kb, (((1,), (1,)), ((), ())), preferred_element_type=F32)
            s_ref[pair] = s + bias_ref[pair]

    def softmax(n, s_ref, p_ref, d_ref, masked):
        for pair in range(N_PAIRS):
            s = s_ref[pair]
            if masked:
                s = jnp.where(n * CHUNK + si >= C_PAD, s, NEG_INF)
            p = jnp.exp2(s - jnp.max(s, axis=1, keepdims=True))
            d_ref[pair] = jnp.sum(p, axis=1, keepdims=True)
            p_ref[pair] = p.astype(BF16)

    def emit(n, p_ref, d_ref):
        q0 = _row_start(n, CHUNK)
        for pair in range(N_PAIRS):
            cols = slice(pair * LANES, (pair + 1) * LANES)
            vb = vp_ref[pl.ds(q0, C_BAND), cols]
            r = jnp.dot(p_ref[pair], vb, preferred_element_type=F32) / d_ref[pair]
            o_ref[pl.ds(q0, CHUNK), cols] = jnp.where(first, r[:CHUNK], r[CHUNK:]).astype(BF16)

    def trip(i, carry=0, *, masked, first_trip=False, last_trip=False):
        cur = i & 1
        even = 2 * i
        scores(even + 1, s1_ref)
        if not first_trip:
            emit(even - 1, p1_ref.at[1 - cur], d1_ref.at[1 - cur])
        softmax(even, s0_ref.at[cur], p0_ref, d0_ref, masked)
        if not last_trip:
            scores(even + 2, s0_ref.at[1 - cur])
        emit(even, p0_ref, d0_ref)
        softmax(even + 1, s1_ref, p1_ref.at[cur], d1_ref.at[cur], masked)
        if last_trip:
            emit(even + 1, p1_ref.at[cur], d1_ref.at[cur])
        return carry

    masked_trips = C_PREV // 2
    scores(0, s0_ref.at[0])
    trip(0, masked=True, first_trip=True)
    lax.fori_loop(1, masked_trips, functools.partial(trip, masked=True), 0)
    lax.fori_loop(masked_trips, N_CHUNKS // 2 - 1, functools.partial(trip, masked=False), 0)
    trip(N_CHUNKS // 2 - 1, masked=False, last_trip=True)


def _chunked_attention(proj, bias, layer):
    width = N_PAIRS * LANES
    return pl.pallas_call(
        _chunked_kernel,
        grid=(BATCH,),
        in_specs=[
            pl.BlockSpec((SEQ, width), lambda b: (b, QC_BLK // N_PAIRS)),
            pl.BlockSpec((SEQ, width), lambda b: (b, KC_BLK // N_PAIRS)),
            pl.BlockSpec((SEQ, width), lambda b: (b, VC_BLK // N_PAIRS)),
            pl.BlockSpec((None, N_PAIRS, 2 * CHUNK, C_BAND), lambda b: (layer, 0, 0, 0)),
        ],
        out_specs=pl.BlockSpec((SEQ, width), lambda b: (b, 0)),
        out_shape=jax.ShapeDtypeStruct((TOKENS, BRANCH_WIDTH), BF16),
        scratch_shapes=[pltpu.VMEM((C_PAD + SEQ, width), BF16), pltpu.VMEM((C_PAD + SEQ, width), BF16),
                        pltpu.VMEM((2, N_PAIRS, 2 * CHUNK, C_BAND), F32),
                        pltpu.VMEM((N_PAIRS, 2 * CHUNK, C_BAND), F32),
                        pltpu.VMEM((N_PAIRS, 2 * CHUNK, C_BAND), BF16),
                        pltpu.VMEM((2, N_PAIRS, 2 * CHUNK, C_BAND), BF16),
                        pltpu.VMEM((N_PAIRS, 2 * CHUNK, 1), F32),
                        pltpu.VMEM((2, N_PAIRS, 2 * CHUNK, 1), F32)],
        compiler_params=_params(1),
        name="chunked_attention",
    )(proj, proj, proj, bias)


def _merge_kernel(x_ref, mod_ref, g_ref, oa_ref, ob_ref, oc_ref, wb_ref, wg_ref, wo_ref, out_ref,
                  merged_ref, *, tn):
    h = _rms_mod(x_ref[...], g_ref[...], mod_ref[0:1, :], mod_ref[1:2, :]).astype(BF16)
    branches = (oa_ref[...], ob_ref[...], oc_ref[...])
    for n in range(D_MODEL // tn):
        acc = None
        for k, o in enumerate(branches):
            y = jnp.dot(o, wb_ref[k, :, n * tn:(n + 1) * tn], preferred_element_type=F32)
            gate = jnp.dot(h, wg_ref[:, k * D_MODEL + n * tn:k * D_MODEL + (n + 1) * tn],
                           preferred_element_type=F32)
            term = jax.nn.sigmoid(gate) * y
            acc = term if acc is None else acc + term
        merged_ref[:, n * tn:(n + 1) * tn] = acc.astype(BF16)
    merged = merged_ref[...]
    for n in range(D_MODEL // tn):
        sl = slice(n * tn, (n + 1) * tn)
        out = jnp.dot(merged, wo_ref[:, sl], preferred_element_type=F32)
        out_ref[:, sl] = x_ref[:, sl] + mod_ref[2:3, sl] * out


def _merge(x, mod, gain, o_a, o_b, o_c, w_branch, w_gate, w_out, layer):
    steps_per_batch = SEQ // TM
    row = lambda i: (i, 0)
    return pl.pallas_call(
        functools.partial(_merge_kernel, tn=256),
        grid=(TOKENS // TM,),
        in_specs=[
            pl.BlockSpec((TM, D_MODEL), row),
            pl.BlockSpec((None, None, 6, D_MODEL), lambda i: (layer, i // steps_per_batch, 0, 0)),
            pl.BlockSpec((None, 1, D_MODEL), lambda i: (layer, 0, 0)),
            pl.BlockSpec((TM, BRANCH_WIDTH), row),
            pl.BlockSpec((TM, BRANCH_WIDTH), row),
            pl.BlockSpec((TM, BRANCH_WIDTH), row),
            pl.BlockSpec((None, 3, BRANCH_WIDTH, D_MODEL), lambda i: (layer, 0, 0, 0)),
            pl.BlockSpec((None, D_MODEL, 3 * D_MODEL), lambda i: (layer, 0, 0)),
            pl.BlockSpec((None, D_MODEL, D_MODEL), lambda i: (layer, 0, 0)),
        ],
        out_specs=pl.BlockSpec((TM, D_MODEL), row),
        out_shape=jax.ShapeDtypeStruct((TOKENS, D_MODEL), F32),
        scratch_shapes=[pltpu.VMEM((TM, D_MODEL), BF16)],
        compiler_params=_params(1),
        name="merge_out",
    )(x, mod, gain, o_a, o_b, o_c, w_branch, w_gate, w_out)


def _ffn_kernel(x_ref, mod_ref, g_ref, gf_ref, wi_ref, wo_ref, out_ref, act_ref, *, tf, tn, final):
    h = _rms_mod(x_ref[...], g_ref[...], mod_ref[3:4, :], mod_ref[4:5, :]).astype(BF16)
    for c in range(FFN_HIDDEN // tf):
        gate = jnp.dot(h, wi_ref[:, c * tf:(c + 1) * tf], preferred_element_type=F32)
        up = jnp.dot(h, wi_ref[:, FFN_HIDDEN + c * tf:FFN_HIDDEN + (c + 1) * tf],
                     preferred_element_type=F32)
        act_ref[:, c * tf:(c + 1) * tf] = (gate * jax.nn.sigmoid(gate) * up).astype(BF16)
    act = act_ref[...]
    for n in range(D_MODEL // tn):
        sl = slice(n * tn, (n + 1) * tn)
        out = jnp.dot(act, wo_ref[:, sl], preferred_element_type=F32)
        out_ref[:, sl] = x_ref[:, sl] + mod_ref[5:6, sl] * out
    if final:
        y = out_ref[...]
        ms = jnp.mean(y * y, axis=-1, keepdims=True)
        out_ref[...] = y * lax.rsqrt(ms + EPS) * gf_ref[...]


def _ffn(x, mod, gain, final_gain, w_ffn_in, w_ffn_out, layer, final):
    steps_per_batch = SEQ // TM
    row = lambda i: (i, 0)
    return pl.pallas_call(
        functools.partial(_ffn_kernel, tf=256, tn=256, final=final),
        grid=(TOKENS // TM,),
        in_specs=[
            pl.BlockSpec((TM, D_MODEL), row),
            pl.BlockSpec((None, None, 6, D_MODEL), lambda i: (layer, i // steps_per_batch, 0, 0)),
            pl.BlockSpec((None, 1, D_MODEL), lambda i: (layer, 0, 0)),
            pl.BlockSpec((1, D_MODEL), lambda i: (0, 0)),
            pl.BlockSpec((None, D_MODEL, 2 * FFN_HIDDEN), lambda i: (layer, 0, 0)),
            pl.BlockSpec((None, FFN_HIDDEN, D_MODEL), lambda i: (layer, 0, 0)),
        ],
        out_specs=pl.BlockSpec((TM, D_MODEL), row),
        out_shape=jax.ShapeDtypeStruct((TOKENS, D_MODEL), F32),
        scratch_shapes=[pltpu.VMEM((TM, FFN_HIDDEN), BF16)],
        compiler_params=_params(1),
        name="ffn",
    )(x, mod, gain, final_gain, w_ffn_in, w_ffn_out)


def kernel(x, c, norm_mix_g, norm_ffn_g, w_ada, b_ada, w_in, b_forget, sinks, rel_bias,
           w_branch, w_out, w_ffn_in, w_ffn_out, final_norm_g):
    scale = HEAD_DIM ** -0.5
    w_in_t = jnp.swapaxes(w_in, 1, 2)
    w_qkv = _pack_w_in(w_in_t, (IN_B0, IN_C0, IN_A0), (IN_FB0 - IN_B0, IN_GATE0 - IN_C0, IN_B0 - IN_A0),
                       (scale * LOG2E,) * 3)
    w_gate = _pack_w_in(w_in_t, (IN_GATE0,), (N_IN_COLS - IN_GATE0,), (None,))
    w_fb = jnp.pad(w_in[:, :, IN_FB0:IN_FB0 + B_HEADS], ((0, 0), (0, 0), (0, LANES - B_HEADS))).astype(BF16)
    w_branch_b = w_branch.astype(BF16)
    w_out_b = w_out.astype(BF16)
    w_ffn_in_b = w_ffn_in.astype(BF16)
    w_ffn_out_b = w_ffn_out.astype(BF16)
    b_forget_rows = jnp.pad(b_forget, ((0, 0), (0, LANES - B_HEADS))).reshape(DEPTH, 1, LANES)
    gain_mix = norm_mix_g.reshape(DEPTH, 1, D_MODEL)
    gain_ffn = norm_ffn_g.reshape(DEPTH, 1, D_MODEL)
    gain_final = final_norm_g.reshape(1, D_MODEL)

    mod = _ada_mod(c, w_ada, b_ada)
    bias_c = _rel_bias(rel_bias)

    xt = x.reshape(TOKENS, D_MODEL)
    for layer in range(DEPTH):
        proj, fb = _in_proj(xt, mod, gain_mix, w_qkv, w_fb, layer)
        ccol, crow = _forget_cumsum(fb, b_forget_rows[layer])
        o_a = _swa_attention(proj, sinks, layer)
        o_b = _fox_attention(proj, ccol, crow)
        o_c = _chunked_attention(proj, bias_c, layer)
        xt = _merge(xt, mod, gain_mix, o_a, o_b, o_c, w_branch_b, w_gate, w_out_b, layer)
        xt = _ffn(xt, mod, gain_ffn, gain_final, w_ffn_in_b, w_ffn_out_b, layer, layer == DEPTH - 1)
    return xt.reshape(BATCH, SEQ, D_MODEL)
```

```python
import functools

import jax
import jax.numpy as jnp
from jax import lax
from jax.experimental import pallas as pl
from jax.experimental.pallas import tpu as pltpu

F32 = jnp.float32
BF16 = jnp.bfloat16

D_MODEL = 1024
BATCH = 8
SEQ = 2048
TOKENS = BATCH * SEQ
DEPTH = 2
CHUNK = 64
HEAD_DIM = 64
EPS = 1e-6
NEG_INF = -1e30
LOG2E = 1.4426950408889634

A_HEADS = 8
A_KV_HEADS = 2
A_GROUP = A_HEADS // A_KV_HEADS
A_PREV = 2
A_BAND = (A_PREV + 1) * CHUNK
B_HEADS = 8
C_HEADS = 8
C_PREV = 8
C_PAD = C_PREV * CHUNK
C_BAND = (C_PREV + 1) * CHUNK
REL_CLIP = 128
N_REL = 2 * REL_CLIP + 1
BRANCH_WIDTH = 512
FFN_HIDDEN = 2816
N_CHUNKS = SEQ // CHUNK
N_IN_COLS = 6920

LANES = 128
N_PAIRS = 4

PROJ_COLS = 3840
QB_BLK, KB_BLK, VB_BLK = 0, 4, 8
QC_BLK, KC_BLK, VC_BLK = 12, 16, 20
QA_BLK, KA_BLK, VA_BLK = 24, 28, 29

TM = 1024
FOX_TQ = 512
FOX_TK = 256
FOX_VROWS = HEAD_DIM + 16
FOX_PAIRS = 2
SOFTMAX_ROWS = 32
CUM_BLK = 256

VMEM_LIMIT = 56 * 1024 * 1024


def _params(n_axes):
    return pltpu.CompilerParams(dimension_semantics=("arbitrary",) * n_axes,
                                vmem_limit_bytes=VMEM_LIMIT)


def _rms_mod(x, g, shift, scale):
    ms = jnp.mean(x * x, axis=-1, keepdims=True)
    y = x * lax.rsqrt(ms + EPS) * g
    return y * (1.0 + scale) + shift


def _row_start(index, size):
    if isinstance(index, int):
        return index * size
    return pl.multiple_of(index * size, size)


def _lane_half(shape):
    return lax.broadcasted_iota(jnp.int32, shape, len(shape) - 1) < HEAD_DIM


def _ada_kernel(c_ref, w_ref, b_ref, o_ref):
    c = c_ref[...]
    cond = c * jax.nn.sigmoid(c)
    o_ref[...] = jnp.dot(cond.astype(BF16), w_ref[...].astype(BF16),
                         preferred_element_type=F32) + b_ref[...]


def _ada_mod(c, w_ada, b_ada):
    n_blk = 6
    out = pl.pallas_call(
        _ada_kernel,
        grid=(DEPTH, n_blk),
        in_specs=[
            pl.BlockSpec((BATCH, D_MODEL), lambda l, j: (0, 0)),
            pl.BlockSpec((None, D_MODEL, D_MODEL), lambda l, j: (l, 0, j)),
            pl.BlockSpec((None, 1, D_MODEL), lambda l, j: (l, 0, j)),
        ],
        out_specs=pl.BlockSpec((None, BATCH, D_MODEL), lambda l, j: (l, 0, j)),
        out_shape=jax.ShapeDtypeStruct((DEPTH, BATCH, n_blk * D_MODEL), F32),
        compiler_params=_params(2),
        name="ada_mod",
    )(c, w_ada, b_ada.reshape(DEPTH, 1, n_blk * D_MODEL))
    return out.reshape(DEPTH, BATCH, n_blk, D_MODEL)


def _relbias_kernel(rev_ref, o_ref):
    near = C_PAD - REL_CLIP
    width = 2 * REL_CLIP
    q = lax.broadcasted_iota(jnp.int32, (CHUNK, width), 0)
    c = lax.broadcasted_iota(jnp.int32, (CHUNK, width), 1)
    for h in range(C_HEADS):
        rev = rev_ref[h:h + 1, :] * LOG2E
        far = rev[:, 0:1]
        rolled = pltpu.roll(jnp.broadcast_to(rev, (CHUNK, width)), 0, 1, stride=1, stride_axis=0)
        o_ref[h, :, 0:near] = jnp.broadcast_to(far, (CHUNK, near))
        o_ref[h, :, near:C_BAND] = jnp.where(c >= q, rolled, far)[:, :C_BAND - near]


def _rel_bias(rel_bias):
    rev = rel_bias[:, :, ::-1][:, :, :2 * REL_CLIP]
    out = pl.pallas_call(
        _relbias_kernel,
        grid=(DEPTH,),
        in_specs=[pl.BlockSpec((None, C_HEADS, 2 * REL_CLIP), lambda l: (l, 0, 0))],
        out_specs=pl.BlockSpec((None, C_HEADS, CHUNK, C_BAND), lambda l: (l, 0, 0, 0)),
        out_shape=jax.ShapeDtypeStruct((DEPTH, C_HEADS, CHUNK, C_BAND), F32),
        compiler_params=_params(1),
        name="rel_bias",
    )(rev)
    return out.reshape(DEPTH, N_PAIRS, 2 * CHUNK, C_BAND)


IN_A0, IN_B0, IN_FB0, IN_C0, IN_GATE0 = 0, 768, 2304, 2312, 3848
PACK_ROWS = 384


def _pack_kernel(wt_ref, o_ref, *, region_starts, region_steps, q_scales):
    j = pl.program_id(1)
    w = wt_ref[0]
    first_step = 0
    row = lax.broadcasted_iota(jnp.int32, (PACK_ROWS, 1), 0)
    scale = jnp.ones((PACK_ROWS, 1), F32)
    for steps, q_scale in zip(region_steps, q_scales):
        if q_scale is not None:
            in_region = (j >= first_step) & (j < first_step + steps)
            col_in_region = (j - first_step) * PACK_ROWS + row
            scale = jnp.where(in_region & (col_in_region < A_HEADS * HEAD_DIM), q_scale, scale)
        first_step += steps
    o_ref[...] = (w * scale).T.astype(BF16)


def _pack_w_in(w_in_t, region_starts, region_widths, q_scales):
    region_steps = tuple(w // PACK_ROWS for w in region_widths)

    def source_row(j):
        row = jnp.int32(0)
        first_step = 0
        for start, steps in zip(region_starts, region_steps):
            row = jnp.where(j >= first_step, start + (j - first_step) * PACK_ROWS, row)
            first_step += steps
        return pl.multiple_of(row, 8)

    return pl.pallas_call(
        functools.partial(_pack_kernel, region_starts=region_starts, region_steps=region_steps, q_scales=q_scales),
        grid=(DEPTH, sum(region_steps)),
        in_specs=[pl.BlockSpec((pl.Element(1), pl.Element(PACK_ROWS), pl.Element(D_MODEL)),
                               lambda l, j: (l, source_row(j), 0))],
        out_specs=pl.BlockSpec((None, D_MODEL, PACK_ROWS), lambda l, j: (l, 0, j)),
        out_shape=jax.ShapeDtypeStruct((DEPTH, D_MODEL, sum(region_widths)), BF16),
        compiler_params=_params(2),
        name="pack_w_in",
    )(w_in_t)


def _inproj_kernel(x_ref, mod_ref, g_ref, w_ref, wfb_ref, proj_ref, fb_ref, *, tn):
    h = _rms_mod(x_ref[...], g_ref[...], mod_ref[0:1, :], mod_ref[1:2, :]).astype(BF16)
    for j in range(PROJ_COLS // tn):
        sl = slice(j * tn, (j + 1) * tn)
        proj_ref[:, sl] = jnp.dot(h, w_ref[:, sl], preferred_element_type=F32).astype(BF16)
    fb_ref[...] = jnp.dot(h, wfb_ref[...], preferred_element_type=F32)


def _in_proj(x, mod, gain, w_qkv, w_fb, layer):
    tn = 768
    steps_per_batch = SEQ // TM
    return pl.pallas_call(
        functools.partial(_inproj_kernel, tn=tn),
        grid=(TOKENS // TM,),
        in_specs=[
            pl.BlockSpec((TM, D_MODEL), lambda i: (i, 0)),
            pl.BlockSpec((None, None, 6, D_MODEL), lambda i: (layer, i // steps_per_batch, 0, 0)),
            pl.BlockSpec((None, 1, D_MODEL), lambda i: (layer, 0, 0)),
            pl.BlockSpec((None, D_MODEL, PROJ_COLS), lambda i: (layer, 0, 0)),
            pl.BlockSpec((None, D_MODEL, LANES), lambda i: (layer, 0, 0)),
        ],
        out_specs=[
            pl.BlockSpec((TM, PROJ_COLS), lambda i: (i, 0)),
            pl.BlockSpec((TM, LANES), lambda i: (i, 0)),
        ],
        out_shape=[
            jax.ShapeDtypeStruct((TOKENS, PROJ_COLS), BF16),
            jax.ShapeDtypeStruct((TOKENS, LANES), F32),
        ],
        compiler_params=_params(1),
        name="in_proj",
    )(x, mod, gain, w_qkv, w_fb)


def _cumsum_kernel(fb_ref, bias_ref, col_ref, row_ref):
    r = lax.broadcasted_iota(jnp.int32, (CUM_BLK, CUM_BLK), 0)
    c = lax.broadcasted_iota(jnp.int32, (CUM_BLK, CUM_BLK), 1)
    tri = (r >= c).astype(BF16)
    carry = jnp.zeros((1, LANES), F32)
    for blk in range(SEQ // CUM_BLK):
        rows = slice(blk * CUM_BLK, (blk + 1) * CUM_BLK)
        z = fb_ref[0, rows, :] + bias_ref[...]
        log_f = jnp.minimum(z, 0.0) - jnp.log1p(jnp.exp(-jnp.abs(z)))
        hi = log_f.astype(BF16)
        rest = log_f - hi.astype(F32)
        mid = rest.astype(BF16)
        lo = (rest - mid.astype(F32)).astype(BF16)
        cum = (jnp.dot(tri, hi, preferred_element_type=F32) + jnp.dot(tri, mid, preferred_element_type=F32)
               + jnp.dot(tri, lo, preferred_element_type=F32)) + carry
        col_ref[0, rows, :] = cum
        row_ref[0, :, rows] = cum.T[:B_HEADS, :]
        carry = cum[CUM_BLK - 1:CUM_BLK, :]


def _forget_cumsum(fb, b_forget_row):
    return pl.pallas_call(
        _cumsum_kernel,
        grid=(BATCH,),
        in_specs=[
            pl.BlockSpec((1, SEQ, LANES), lambda b: (b, 0, 0)),
            pl.BlockSpec((1, LANES), lambda b: (0, 0)),
        ],
        out_specs=[
            pl.BlockSpec((1, SEQ, LANES), lambda b: (b, 0, 0)),
            pl.BlockSpec((1, B_HEADS, SEQ), lambda b: (b, 0, 0)),
        ],
        out_shape=[
            jax.ShapeDtypeStruct((BATCH, SEQ, LANES), F32),
            jax.ShapeDtypeStruct((BATCH, B_HEADS, SEQ), F32),
        ],
        compiler_params=_params(1),
        name="forget_cumsum",
    )(fb.reshape(BATCH, SEQ, LANES), b_forget_row)


def _fox_kernel(q_ref, k_ref, v_ref, ccol_ref, crow_ref, o_ref, vt_ref, ckb_ref,
                s0_ref, s1_ref, p0_ref, p1_ref, acc_ref, m_ref, a0_ref, a1_ref, cq_ref, qm_ref):
    tq, tk = FOX_TQ, FOX_TK
    heads = 2 * FOX_PAIRS
    grp = pl.program_id(1)
    first = _lane_half((1, LANES))

    head_lane = lax.broadcasted_iota(jnp.int32, (1, LANES), 1)
    ones_row = (lax.broadcasted_iota(jnp.int32, (FOX_VROWS - HEAD_DIM, tk), 0) == 0).astype(BF16)
    for jt in range(SEQ // tk):
        rows = slice(jt * tk, (jt + 1) * tk)
        v_t = v_ref[rows, :].astype(F32).T.astype(BF16)
        ccol = ccol_ref[0, rows, :]
        for h in range(heads):
            vt_ref[jt, h * FOX_VROWS:h * FOX_VROWS + HEAD_DIM, :] = v_t[h * HEAD_DIM:(h + 1) * HEAD_DIM]
            vt_ref[jt, h * FOX_VROWS + HEAD_DIM:(h + 1) * FOX_VROWS, :] = ones_row
            col = jnp.sum(jnp.where(head_lane == grp * heads + h, ccol, 0.0), axis=1, keepdims=True)
            ckb_ref[h, rows, :] = jnp.broadcast_to(col * LOG2E, (tk, LANES))

    for qi in range(SEQ // tq):
        _fox_query_tile(qi, grp, first, q_ref, k_ref, crow_ref, o_ref, vt_ref, ckb_ref, s0_ref, s1_ref,
                        p0_ref, p1_ref, acc_ref, m_ref, a0_ref, a1_ref, cq_ref.at[qi & 1], qm_ref.at[qi & 1])


def _fox_query_tile(qi, grp, first, q_ref, k_ref, crow_ref, o_ref, vt_ref, ckb_ref, s0_ref, s1_ref,
                    p0_ref, p1_ref, acc_ref, m_ref, a0_ref, a1_ref, cq_ref, qm_ref):
    tq, tk = FOX_TQ, FOX_TK
    heads = 2 * FOX_PAIRS
    q_rows = slice(qi * tq, (qi + 1) * tq)

    for h in range(heads):
        q2 = q_ref[q_rows, (h // 2) * LANES:(h // 2 + 1) * LANES]
        qm_ref[h] = jnp.where(first if h % 2 == 0 else ~first, q2, jnp.zeros_like(q2))
        cq_ref[h] = crow_ref[0, grp * heads + h, qi:qi + 1, :] * LOG2E

    whole, upper = slice(0, tq), slice(tq // 2, tq)

    def scores(j, s_ref, q=whole):
        row0 = _row_start(j, tk)
        for h in range(heads):
            kj = k_ref[pl.ds(row0, tk), (h // 2) * LANES:(h // 2 + 1) * LANES]
            s = lax.dot_general(kj, qm_ref[h, q, :], (((1,), (1,)), ((), ())), preferred_element_type=F32)
            ck = ckb_ref[h, pl.ds(row0, tk), :]
            s_ref[h, :, q] = s - jnp.concatenate([ck] * ((q.stop - q.start) // LANES), axis=1)

    def softmax(s_ref, p_ref, a_ref, mask, q=whole):
        for h in range(heads):
            s = s_ref[h, :, q]
            if mask is not None:
                s = jnp.where(mask, s, NEG_INF)
            m_old = m_ref[h, :, q]
            cq = cq_ref[h, :, q]
            m_new = jnp.maximum(m_old, jnp.max(s, axis=0, keepdims=True) + cq)
            p_ref[h, :, q] = jnp.exp2(s + (cq - m_new)).astype(BF16)
            m_ref[h, :, q] = m_new
            a_ref[h, :, q] = jnp.exp2(m_old - m_new)

    def accumulate(j, p_ref, a_ref, q=whole):
        tile = max(j, 0)
        for h in range(heads):
            vt = vt_ref[tile, h * FOX_VROWS:(h + 1) * FOX_VROWS, :]
            pv = jnp.dot(vt, p_ref[h, :, q], preferred_element_type=F32)
            acc_ref[h, :, q] = a_ref[h, :, q] * acc_ref[h, :, q] + pv

    def trip(i, last):
        cur = i & 1
        even = 2 * i
        odd_q = upper if last else whole
        mask_even = mask_odd = None
        if last:
            causal = lambda width: (lax.broadcasted_iota(jnp.int32, (tk, width), 0)
                                    <= lax.broadcasted_iota(jnp.int32, (tk, width), 1))
            mask_even = causal(tq)
            mask_odd = causal(tq // 2)
        scores(even + 1, s1_ref, odd_q)
        accumulate(even - 1, p1_ref.at[1 - cur], a1_ref.at[1 - cur])
        softmax(s0_ref.at[cur], p0_ref, a0_ref, mask_even)
        if not last:
            scores(even + 2, s0_ref.at[1 - cur])
        accumulate(even, p0_ref, a0_ref)
        softmax(s1_ref, p1_ref.at[cur], a1_ref.at[cur], mask_odd, odd_q)
        if last:
            accumulate(even + 1, p1_ref.at[cur], a1_ref.at[cur], odd_q)

    acc_ref[...] = jnp.zeros(acc_ref.shape, F32)
    m_ref[...] = jnp.full(m_ref.shape, NEG_INF, F32)
    p1_ref[1] = jnp.zeros(p1_ref.shape[1:], BF16)
    a1_ref[1] = jnp.ones(a1_ref.shape[1:], F32)
    scores(0, s0_ref.at[0])
    for i in range(qi):
        trip(i, last=False)
    trip(qi, last=True)
    out_t = jnp.concatenate([acc_ref[h, :HEAD_DIM, :] / acc_ref[h, HEAD_DIM:HEAD_DIM + 1, :]
                             for h in range(heads)], axis=0)
    o_ref[q_rows, :] = out_t.T.astype(BF16)


def _fox_attention(proj, ccol, crow):
    tq, tk = FOX_TQ, FOX_TK
    nq = SEQ // tq
    heads = 2 * FOX_PAIRS
    width = FOX_PAIRS * LANES
    return pl.pallas_call(
        _fox_kernel,
        grid=(BATCH, N_PAIRS // FOX_PAIRS),
        in_specs=[
            pl.BlockSpec((SEQ, width), lambda b, g: (b, QB_BLK // FOX_PAIRS + g)),
            pl.BlockSpec((SEQ, width), lambda b, g: (b, KB_BLK // FOX_PAIRS + g)),
            pl.BlockSpec((SEQ, width), lambda b, g: (b, VB_BLK // FOX_PAIRS + g)),
            pl.BlockSpec((1, SEQ, LANES), lambda b, g: (b, 0, 0)),
            pl.BlockSpec((1, B_HEADS, nq, tq), lambda b, g: (b, 0, 0, 0)),
        ],
        out_specs=pl.BlockSpec((SEQ, width), lambda b, g: (b, g)),
        out_shape=jax.ShapeDtypeStruct((TOKENS, BRANCH_WIDTH), BF16),
        scratch_shapes=[pltpu.VMEM((SEQ // tk, heads * FOX_VROWS, tk), BF16),
                        pltpu.VMEM((heads, SEQ, LANES), F32),
                        pltpu.VMEM((2, heads, tk, tq), F32), pltpu.VMEM((heads, tk, tq), F32),
                        pltpu.VMEM((heads, tk, tq), BF16), pltpu.VMEM((2, heads, tk, tq), BF16),
                        pltpu.VMEM((heads, FOX_VROWS, tq), F32),
                        pltpu.VMEM((heads, 1, tq), F32),
                        pltpu.VMEM((heads, 1, tq), F32),
                        pltpu.VMEM((2, heads, 1, tq), F32),
                        pltpu.VMEM((2, heads, 1, tq), F32),
                        pltpu.VMEM((2, heads, tq, LANES), BF16)],
        compiler_params=_params(2),
        name="fox_attention",
    )(proj, proj, proj, ccol, crow.reshape(BATCH, B_HEADS, nq, tq))


def _swa_kernel(sink_ref, q_ref, k_ref, v_ref, o_ref, kd_ref, vd_ref,
                s0_ref, s1_ref, p0_ref, p1_ref, d0_ref, d1_ref, *, layer):
    rows = A_GROUP * CHUNK
    first = _lane_half((1, LANES))
    grp = lax.broadcasted_iota(jnp.int32, (rows, 1), 0) // CHUNK
    qi = lax.broadcasted_iota(jnp.int32, (rows, A_BAND), 0) % CHUNK
    si = lax.broadcasted_iota(jnp.int32, (rows, A_BAND), 1)
    sel_r = lax.broadcasted_iota(jnp.int32, (LANES, LANES), 0)
    sel_c = lax.broadcasted_iota(jnp.int32, (LANES, LANES), 1)

    slopes = []
    for kvh in range(A_KV_HEADS):
        sel = (sel_r == kvh * HEAD_DIM + sel_c % HEAD_DIM).astype(BF16)
        kd_ref[kvh] = jnp.dot(k_ref[...], sel, preferred_element_type=F32).astype(BF16)
        vd_ref[kvh] = jnp.dot(v_ref[...], sel, preferred_element_type=F32).astype(BF16)
        slope = jnp.zeros((rows, 1), F32)
        for g in range(A_GROUP):
            slope = jnp.where(grp == g, 2.0 ** -(kvh * A_GROUP + g + 1) * LOG2E, slope)
        slopes.append(slope)

    def band_start(n):
        if isinstance(n, int):
            return max(n - A_PREV, 0) * CHUNK
        return _row_start(jnp.maximum(n - A_PREV, 0), CHUNK)

    def scores(n, s_ref, alibi):
        q0 = _row_start(n, CHUNK)
        for kvh in range(A_KV_HEADS):
            parts = []
            for g in range(A_GROUP):
                head = kvh * A_GROUP + g
                blk = q_ref[pl.ds(q0, CHUNK), (head // 2) * LANES:(head // 2 + 1) * LANES]
                parts.append(jnp.where(first if head % 2 == 0 else ~first, blk, jnp.zeros_like(blk)))
            qs = jnp.concatenate(parts, axis=0)
            kb = kd_ref[kvh, pl.ds(band_start(n), A_BAND), :]
            s = lax.dot_general(qs, kb, (((1,), (1,)), ((), ())), preferred_element_type=F32)
            s_ref[kvh] = s + alibi[kvh]

    def softmax(s_ref, p_ref, d_ref, last_key_chunk):
        key_chunk = lax.broadcasted_iota(jnp.int32, (SOFTMAX_ROWS, A_BAND), 1) // CHUNK
        for kvh in range(A_KV_HEADS):
            for r0 in range(0, rows, SOFTMAX_ROWS):
                blk = slice(r0, r0 + SOFTMAX_ROWS)
                sink = sink_ref[layer, kvh * A_GROUP + r0 // CHUNK] * LOG2E
                s = s_ref[kvh, blk, :]
                if last_key_chunk is not None:
                    s = jnp.where(key_chunk <= last_key_chunk, s, NEG_INF)
                m = jnp.maximum(jnp.max(s, axis=1, keepdims=True), sink)
                p = jnp.exp2(s - m)
                denom = jnp.sum(p, axis=1, keepdims=True) + jnp.exp2(sink - m)
                d_ref[kvh, blk, :] = jnp.broadcast_to(denom, (SOFTMAX_ROWS, LANES))
                p_ref[kvh, blk, :] = p.astype(BF16)

    def emit(n, p_ref, d_ref):
        q0 = _row_start(n, CHUNK)
        for kvh in range(A_KV_HEADS):
            vb = vd_ref[kvh, pl.ds(band_start(n), A_BAND), :]
            r = jnp.dot(p_ref[kvh], vb, preferred_element_type=F32) / d_ref[kvh]
            for pr in range(A_GROUP // 2):
                even = r[(2 * pr) * CHUNK:(2 * pr + 1) * CHUNK]
                odd = r[(2 * pr + 1) * CHUNK:(2 * pr + 2) * CHUNK]
                col = (kvh * (A_GROUP // 2) + pr) * LANES
                o_ref[pl.ds(q0, CHUNK), col:col + LANES] = jnp.where(first, even, odd).astype(BF16)

    def alibi_for(dist):
        return tuple(-slope * jnp.abs(dist).astype(F32) for slope in slopes)

    alibi = alibi_for(A_PREV * CHUNK + qi - si)
    def trip(i, carry=0, *, first_trip=False, last_trip=False):
        cur = i & 1
        even = 2 * i
        if first_trip:
            scores(1, s1_ref, alibi_for(CHUNK + qi - si))
            softmax(s0_ref.at[cur], p0_ref, d0_ref, 0)
        else:
            scores(even + 1, s1_ref, alibi)
            emit(even - 1, p1_ref.at[1 - cur], d1_ref.at[1 - cur])
            softmax(s0_ref.at[cur], p0_ref, d0_ref, None)
        if not last_trip:
            scores(even + 2, s0_ref.at[1 - cur], alibi)
        emit(even, p0_ref, d0_ref)
        softmax(s1_ref, p1_ref.at[cur], d1_ref.at[cur], 1 if first_trip else None)
        if last_trip:
            emit(even + 1, p1_ref.at[cur], d1_ref.at[cur])
        return carry

    scores(0, s0_ref.at[0], alibi_for(qi - si))
    trip(0, first_trip=True)
    for i in range(1, N_CHUNKS // 2 - 1):
        trip(i)
    trip(N_CHUNKS // 2 - 1, last_trip=True)


def _swa_attention(proj, sinks, layer):
    return pl.pallas_call(
        functools.partial(_swa_kernel, layer=layer),
        grid=(BATCH,),
        in_specs=[
            pl.BlockSpec(memory_space=pltpu.SMEM),
            pl.BlockSpec((SEQ, A_HEADS * HEAD_DIM), lambda b: (b, QA_BLK // N_PAIRS)),
            pl.BlockSpec((SEQ, LANES), lambda b: (b, KA_BLK)),
            pl.BlockSpec((SEQ, LANES), lambda b: (b, VA_BLK)),
        ],
        out_specs=pl.BlockSpec((SEQ, BRANCH_WIDTH), lambda b: (b, 0)),
        out_shape=jax.ShapeDtypeStruct((TOKENS, BRANCH_WIDTH), BF16),
        scratch_shapes=[pltpu.VMEM((A_KV_HEADS, SEQ, LANES), BF16),
                        pltpu.VMEM((A_KV_HEADS, SEQ, LANES), BF16),
                        pltpu.VMEM((2, A_KV_HEADS, A_GROUP * CHUNK, A_BAND), F32),
                        pltpu.VMEM((A_KV_HEADS, A_GROUP * CHUNK, A_BAND), F32),
                        pltpu.VMEM((A_KV_HEADS, A_GROUP * CHUNK, A_BAND), BF16),
                        pltpu.VMEM((2, A_KV_HEADS, A_GROUP * CHUNK, A_BAND), BF16),
                        pltpu.VMEM((A_KV_HEADS, A_GROUP * CHUNK, LANES), F32),
                        pltpu.VMEM((2, A_KV_HEADS, A_GROUP * CHUNK, LANES), F32)],
        compiler_params=_params(1),
        name="swa_attention",
    )(sinks, proj, proj, proj)


def _chunked_kernel(q_ref, k_ref, v_ref, bias_ref, o_ref, kp_ref, vp_ref,
                    s0_ref, s1_ref, p0_ref, p1_ref, d0_ref, d1_ref):
    first = _lane_half((1, LANES))
    width = N_PAIRS * LANES
    kp_ref[0:C_PAD, :] = jnp.zeros((C_PAD, width), BF16)
    vp_ref[0:C_PAD, :] = jnp.zeros((C_PAD, width), BF16)
    kp_ref[C_PAD:, :] = k_ref[...]
    vp_ref[C_PAD:, :] = v_ref[...]
    si = lax.broadcasted_iota(jnp.int32, (2 * CHUNK, C_BAND), 1)

    def scores(n, s_ref):
        q0 = _row_start(n, CHUNK)
        for pair in range(N_PAIRS):
            cols = slice(pair * LANES, (pair + 1) * LANES)
            q2 = q_ref[pl.ds(q0, CHUNK), cols]
            zero = jnp.zeros_like(q2)
            qs = jnp.concatenate([jnp.where(first, q2, zero), jnp.where(first, zero, q2)], axis=0)
            kb = kp_ref[pl.ds(q0, C_BAND), cols]
            s = lax.dot_general(qs, kb, (((1,), (1,)), ((), ())), preferred_element_type=F32)
            s_ref[pair] = s + bias_ref[pair]

    def softmax(n, s_ref, p_ref, d_ref, masked):
        for pair in range(N_PAIRS):
            s = s_ref[pair]
            if masked:
                s = jnp.where(n * CHUNK + si >= C_PAD, s, NEG_INF)
            p = jnp.exp2(s - jnp.max(s, axis=1, keepdims=True))
            d_ref[pair] = jnp.sum(p, axis=1, keepdims=True)
            p_ref[pair] = p.astype(BF16)

    def emit(n, p_ref, d_ref):
        q0 = _row_start(n, CHUNK)
        for pair in range(N_PAIRS):
            cols = slice(pair * LANES, (pair + 1) * LANES)
            vb = vp_ref[pl.ds(q0, C_BAND), cols]
            r = jnp.dot(p_ref[pair], vb, preferred_element_type=F32) / d_ref[pair]
            o_ref[pl.ds(q0, CHUNK), cols] = jnp.where(first, r[:CHUNK], r[CHUNK:]).astype(BF16)

    def trip(i, carry=0, *, masked, first_trip=False, last_trip=False):
        cur = i & 1
        even = 2 * i
        scores(even + 1, s1_ref)
        if not first_trip:
            emit(even - 1, p1_ref.at[1 - cur], d1_ref.at[1 - cur])
        softmax(even, s0_ref.at[cur], p0_ref, d0_ref, masked)
        if not last_trip:
            scores(even + 2, s0_ref.at[1 - cur])
        emit(even, p0_ref, d0_ref)
        softmax(even + 1, s1_ref, p1_ref.at[cur], d1_ref.at[cur], masked)
        if last_trip:
            emit(even + 1, p1_ref.at[cur], d1_ref.at[cur])
        return carry

    masked_trips = C_PREV // 2
    scores(0, s0_ref.at[0])
    trip(0, masked=True, first_trip=True)
    lax.fori_loop(1, masked_trips, functools.partial(trip, masked=True), 0)
    lax.fori_loop(masked_trips, N_CHUNKS // 2 - 1, functools.partial(trip, masked=False), 0)
    trip(N_CHUNKS // 2 - 1, masked=False, last_trip=True)


def _chunked_attention(proj, bias, layer):
    width = N_PAIRS * LANES
    return pl.pallas_call(
        _chunked_kernel,
        grid=(BATCH,),
        in_specs=[
            pl.BlockSpec((SEQ, width), lambda b: (b, QC_BLK // N_PAIRS)),
            pl.BlockSpec((SEQ, width), lambda b: (b, KC_BLK // N_PAIRS)),
            pl.BlockSpec((SEQ, width), lambda b: (b, VC_BLK // N_PAIRS)),
            pl.BlockSpec((None, N_PAIRS, 2 * CHUNK, C_BAND), lambda b: (layer, 0, 0, 0)),
        ],
        out_specs=pl.BlockSpec((SEQ, width), lambda b: (b, 0)),
        out_shape=jax.ShapeDtypeStruct((TOKENS, BRANCH_WIDTH), BF16),
        scratch_shapes=[pltpu.VMEM((C_PAD + SEQ, width), BF16), pltpu.VMEM((C_PAD + SEQ, width), BF16),
                        pltpu.VMEM((2, N_PAIRS, 2 * CHUNK, C_BAND), F32),
                        pltpu.VMEM((N_PAIRS, 2 * CHUNK, C_BAND), F32),
                        pltpu.VMEM((N_PAIRS, 2 * CHUNK, C_BAND), BF16),
                        pltpu.VMEM((2, N_PAIRS, 2 * CHUNK, C_BAND), BF16),
                        pltpu.VMEM((N_PAIRS, 2 * CHUNK, 1), F32),
                        pltpu.VMEM((2, N_PAIRS, 2 * CHUNK, 1), F32)],
        compiler_params=_params(1),
        name="chunked_attention",
    )(proj, proj, proj, bias)


def _merge_kernel(x_ref, mod_ref, g_ref, oa_ref, ob_ref, oc_ref, wb_ref, wg_ref, wo_ref, out_ref,
                  merged_ref, *, tn):
    h = _rms_mod(x_ref[...], g_ref[...], mod_ref[0:1, :], mod_ref[1:2, :]).astype(BF16)
    branches = (oa_ref[...], ob_ref[...], oc_ref[...])
    for n in range(D_MODEL // tn):
        acc = None
        for k, o in enumerate(branches):
            y = jnp.dot(o, wb_ref[k, :, n * tn:(n + 1) * tn], preferred_element_type=F32)
            gate = jnp.dot(h, wg_ref[:, k * D_MODEL + n * tn:k * D_MODEL + (n + 1) * tn],
                           preferred_element_type=F32)
            term = jax.nn.sigmoid(gate) * y
            acc = term if acc is None else acc + term
        merged_ref[:, n * tn:(n + 1) * tn] = acc.astype(BF16)
    merged = merged_ref[...]
    for n in range(D_MODEL // tn):
        sl = slice(n * tn, (n + 1) * tn)
        out = jnp.dot(merged, wo_ref[:, sl], preferred_element_type=F32)
        out_ref[:, sl] = x_ref[:, sl] + mod_ref[2:3, sl] * out


def _merge(x, mod, gain, o_a, o_b, o_c, w_branch, w_gate, w_out, layer):
    steps_per_batch = SEQ // TM
    row = lambda i: (i, 0)
    return pl.pallas_call(
        functools.partial(_merge_kernel, tn=256),
        grid=(TOKENS // TM,),
        in_specs=[
            pl.BlockSpec((TM, D_MODEL), row),
            pl.BlockSpec((None, None, 6, D_MODEL), lambda i: (layer, i // steps_per_batch, 0, 0)),
            pl.BlockSpec((None, 1, D_MODEL), lambda i: (layer, 0, 0)),
            pl.BlockSpec((TM, BRANCH_WIDTH), row),
            pl.BlockSpec((TM, BRANCH_WIDTH), row),
            pl.BlockSpec((TM, BRANCH_WIDTH), row),
            pl.BlockSpec((None, 3, BRANCH_WIDTH, D_MODEL), lambda i: (layer, 0, 0, 0)),
            pl.BlockSpec((None, D_MODEL, 3 * D_MODEL), lambda i: (layer, 0, 0)),
            pl.BlockSpec((None, D_MODEL, D_MODEL), lambda i: (layer, 0, 0)),
        ],
        out_specs=pl.BlockSpec((TM, D_MODEL), row),
        out_shape=jax.ShapeDtypeStruct((TOKENS, D_MODEL), F32),
        scratch_shapes=[pltpu.VMEM((TM, D_MODEL), BF16)],
        compiler_params=_params(1),
        name="merge_out",
    )(x, mod, gain, o_a, o_b, o_c, w_branch, w_gate, w_out)


def _ffn_kernel(x_ref, mod_ref, g_ref, gf_ref, wi_ref, wo_ref, out_ref, act_ref, *, tf, tn, final):
    h = _rms_mod(x_ref[...], g_ref[...], mod_ref[3:4, :], mod_ref[4:5, :]).astype(BF16)
    for c in range(FFN_HIDDEN // tf):
        gate = jnp.dot(h, wi_ref[:, c * tf:(c + 1) * tf], preferred_element_type=F32)
        up = jnp.dot(h, wi_ref[:, FFN_HIDDEN + c * tf:FFN_HIDDEN + (c + 1) * tf],
                     preferred_element_type=F32)
        act_ref[:, c * tf:(c + 1) * tf] = (gate * jax.nn.sigmoid(gate) * up).astype(BF16)
    act = act_ref[...]
    for n in range(D_MODEL // tn):
        sl = slice(n * tn, (n + 1) * tn)
        out = jnp.dot(act, wo_ref[:, sl], preferred_element_type=F32)
        out_ref[:, sl] = x_ref[:, sl] + mod_ref[5:6, sl] * out
    if final:
        y = out_ref[...]
        ms = jnp.mean(y * y, axis=-1, keepdims=True)
        out_ref[...] = y * lax.rsqrt(ms + EPS) * gf_ref[...]


def _ffn(x, mod, gain, final_gain, w_ffn_in, w_ffn_out, layer, final):
    steps_per_batch = SEQ // TM
    row = lambda i: (i, 0)
    return pl.pallas_call(
        functools.partial(_ffn_kernel, tf=256, tn=256, final=final),
        grid=(TOKENS // TM,),
        in_specs=[
            pl.BlockSpec((TM, D_MODEL), row),
            pl.BlockSpec((None, None, 6, D_MODEL), lambda i: (layer, i // steps_per_batch, 0, 0)),
            pl.BlockSpec((None, 1, D_MODEL), lambda i: (layer, 0, 0)),
            pl.BlockSpec((1, D_MODEL), lambda i: (0, 0)),
            pl.BlockSpec((None, D_MODEL, 2 * FFN_HIDDEN), lambda i: (layer, 0, 0), pipeline_mode=pl.Buffered(1)),
            pl.BlockSpec((None, FFN_HIDDEN, D_MODEL), lambda i: (layer, 0, 0), pipeline_mode=pl.Buffered(1)),
        ],
        out_specs=pl.BlockSpec((TM, D_MODEL), row),
        out_shape=jax.ShapeDtypeStruct((TOKENS, D_MODEL), F32),
        scratch_shapes=[pltpu.VMEM((TM, FFN_HIDDEN), BF16)],
        compiler_params=_params(1),
        name="ffn",
    )(x, mod, gain, final_gain, w_ffn_in, w_ffn_out)


def kernel(x, c, norm_mix_g, norm_ffn_g, w_ada, b_ada, w_in, b_forget, sinks, rel_bias,
           w_branch, w_out, w_ffn_in, w_ffn_out, final_norm_g):
    scale = HEAD_DIM ** -0.5
    w_in_t = jnp.swapaxes(w_in, 1, 2)
    w_qkv = _pack_w_in(w_in_t, (IN_B0, IN_C0, IN_A0), (IN_FB0 - IN_B0, IN_GATE0 - IN_C0, IN_B0 - IN_A0),
                       (scale * LOG2E,) * 3)
    w_gate = _pack_w_in(w_in_t, (IN_GATE0,), (N_IN_COLS - IN_GATE0,), (None,))
    w_fb = jnp.pad(w_in[:, :, IN_FB0:IN_FB0 + B_HEADS], ((0, 0), (0, 0), (0, LANES - B_HEADS))).astype(BF16)
    w_branch_b = w_branch.astype(BF16)
    w_out_b = w_out.astype(BF16)
    w_ffn_in_b = w_ffn_in.astype(BF16)
    w_ffn_out_b = w_ffn_out.astype(BF16)
    b_forget_rows = jnp.pad(b_forget, ((0, 0), (0, LANES - B_HEADS))).reshape(DEPTH, 1, LANES)
    gain_mix = norm_mix_g.reshape(DEPTH, 1, D_MODEL)
    gain_ffn = norm_ffn_g.reshape(DEPTH, 1, D_MODEL)
    gain_final = final_norm_g.reshape(1, D_MODEL)

    mod = _ada_mod(c, w_ada, b_ada)
    bias_c = _rel_bias(rel_bias)

    xt = x.reshape(TOKENS, D_MODEL)
    for layer in range(DEPTH):
        proj, fb = _in_proj(xt, mod, gain_mix, w_qkv, w_fb, layer)
        ccol, crow = _forget_cumsum(fb, b_forget_rows[layer])
        o_a = _swa_attention(proj, sinks, layer)
        o_b = _fox_attention(proj, ccol, crow)
        o_c = _chunked_attention(proj, bias_c, layer)
        xt = _merge(xt, mod, gain_mix, o_a, o_b, o_c, w_branch_b, w_gate, w_out_b, layer)
        xt = _ffn(xt, mod, gain_ffn, gain_final, w_ffn_in_b, w_ffn_out_b, layer, layer == DEPTH - 1)
    return xt.reshape(BATCH, SEQ, D_MODEL)
```

```python
import functools

import jax
import jax.numpy as jnp
from jax import lax
from jax.experimental import pallas as pl
from jax.experimental.pallas import tpu as pltpu

F32 = jnp.float32
BF16 = jnp.bfloat16

D_MODEL = 1024
BATCH = 8
SEQ = 2048
TOKENS = BATCH * SEQ
DEPTH = 2
CHUNK = 64
HEAD_DIM = 64
EPS = 1e-6
NEG_INF = -1e30
LOG2E = 1.4426950408889634

A_HEADS = 8
A_KV_HEADS = 2
A_GROUP = A_HEADS // A_KV_HEADS
A_PREV = 2
A_BAND = (A_PREV + 1) * CHUNK
B_HEADS = 8
C_HEADS = 8
C_PREV = 8
C_PAD = C_PREV * CHUNK
C_BAND = (C_PREV + 1) * CHUNK
REL_CLIP = 128
N_REL = 2 * REL_CLIP + 1
BRANCH_WIDTH = 512
FFN_HIDDEN = 2816
N_CHUNKS = SEQ // CHUNK
N_IN_COLS = 6920

LANES = 128
N_PAIRS = 4

PROJ_COLS = 3840
QB_BLK, KB_BLK, VB_BLK = 0, 4, 8
QC_BLK, KC_BLK, VC_BLK = 12, 16, 20
QA_BLK, KA_BLK, VA_BLK = 24, 28, 29

TM = 1024
FOX_TQ = 512
FOX_TK = 256
FOX_VROWS = HEAD_DIM + 16
FOX_PAIRS = 2
SOFTMAX_ROWS = 32
CUM_BLK = 256

VMEM_LIMIT = 56 * 1024 * 1024


def _params(n_axes):
    return pltpu.CompilerParams(dimension_semantics=("arbitrary",) * n_axes,
                                vmem_limit_bytes=VMEM_LIMIT)


def _rms_mod(x, g, shift, scale):
    ms = jnp.mean(x * x, axis=-1, keepdims=True)
    y = x * lax.rsqrt(ms + EPS) * g
    return y * (1.0 + scale) + shift


def _row_start(index, size):
    if isinstance(index, int):
        return index * size
    return pl.multiple_of(index * size, size)


def _lane_half(shape):
    return lax.broadcasted_iota(jnp.int32, shape, len(shape) - 1) < HEAD_DIM


def _ada_kernel(c_ref, w_ref, b_ref, o_ref):
    c = c_ref[...]
    cond = c * jax.nn.sigmoid(c)
    o_ref[...] = jnp.dot(cond.astype(BF16), w_ref[...].astype(BF16),
                         preferred_element_type=F32) + b_ref[...]


def _ada_mod(c, w_ada, b_ada):
    n_blk = 6
    out = pl.pallas_call(
        _ada_kernel,
        grid=(DEPTH, n_blk),
        in_specs=[
            pl.BlockSpec((BATCH, D_MODEL), lambda l, j: (0, 0)),
            pl.BlockSpec((None, D_MODEL, D_MODEL), lambda l, j: (l, 0, j)),
            pl.BlockSpec((None, 1, D_MODEL), lambda l, j: (l, 0, j)),
        ],
        out_specs=pl.BlockSpec((None, BATCH, D_MODEL), lambda l, j: (l, 0, j)),
        out_shape=jax.ShapeDtypeStruct((DEPTH, BATCH, n_blk * D_MODEL), F32),
        compiler_params=_params(2),
        name="ada_mod",
    )(c, w_ada, b_ada.reshape(DEPTH, 1, n_blk * D_MODEL))
    return out.reshape(DEPTH, BATCH, n_blk, D_MODEL)


def _relbias_kernel(rev_ref, o_ref):
    near = C_PAD - REL_CLIP
    width = 2 * REL_CLIP
    q = lax.broadcasted_iota(jnp.int32, (CHUNK, width), 0)
    c = lax.broadcasted_iota(jnp.int32, (CHUNK, width), 1)
    for h in range(C_HEADS):
        rev = rev_ref[h:h + 1, :] * LOG2E
        far = rev[:, 0:1]
        rolled = pltpu.roll(jnp.broadcast_to(rev, (CHUNK, width)), 0, 1, stride=1, stride_axis=0)
        o_ref[h, :, 0:near] = jnp.broadcast_to(far, (CHUNK, near))
        o_ref[h, :, near:C_BAND] = jnp.where(c >= q, rolled, far)[:, :C_BAND - near]


def _rel_bias(rel_bias):
    rev = rel_bias[:, :, ::-1][:, :, :2 * REL_CLIP]
    out = pl.pallas_call(
        _relbias_kernel,
        grid=(DEPTH,),
        in_specs=[pl.BlockSpec((None, C_HEADS, 2 * REL_CLIP), lambda l: (l, 0, 0))],
        out_specs=pl.BlockSpec((None, C_HEADS, CHUNK, C_BAND), lambda l: (l, 0, 0, 0)),
        out_shape=jax.ShapeDtypeStruct((DEPTH, C_HEADS, CHUNK, C_BAND), F32),
        compiler_params=_params(1),
        name="rel_bias",
    )(rev)
    return out.reshape(DEPTH, N_PAIRS, 2 * CHUNK, C_BAND)


IN_A0, IN_B0, IN_FB0, IN_C0, IN_GATE0 = 0, 768, 2304, 2312, 3848
PACK_ROWS = 768


def _pack_kernel(wt_ref, o_ref, *, region_starts, region_steps, q_scales):
    j = pl.program_id(1)
    w = wt_ref[0]
    first_step = 0
    row = lax.broadcasted_iota(jnp.int32, (PACK_ROWS, 1), 0)
    scale = jnp.ones((PACK_ROWS, 1), F32)
    for steps, q_scale in zip(region_steps, q_scales):
        if q_scale is not None:
            in_region = (j >= first_step) & (j < first_step + steps)
            col_in_region = (j - first_step) * PACK_ROWS + row
            scale = jnp.where(in_region & (col_in_region < A_HEADS * HEAD_DIM), q_scale, scale)
        first_step += steps
    o_ref[...] = (w * scale).T.astype(BF16)


def _pack_w_in(w_in_t, region_starts, region_widths, q_scales):
    region_steps = tuple(w // PACK_ROWS for w in region_widths)

    def source_row(j):
        row = jnp.int32(0)
        first_step = 0
        for start, steps in zip(region_starts, region_steps):
            row = jnp.where(j >= first_step, start + (j - first_step) * PACK_ROWS, row)
            first_step += steps
        return pl.multiple_of(row, 8)

    return pl.pallas_call(
        functools.partial(_pack_kernel, region_starts=region_starts, region_steps=region_steps, q_scales=q_scales),
        grid=(DEPTH, sum(region_steps)),
        in_specs=[pl.BlockSpec((pl.Element(1), pl.Element(PACK_ROWS), pl.Element(D_MODEL)),
                               lambda l, j: (l, source_row(j), 0))],
        out_specs=pl.BlockSpec((None, D_MODEL, PACK_ROWS), lambda l, j: (l, 0, j)),
        out_shape=jax.ShapeDtypeStruct((DEPTH, D_MODEL, sum(region_widths)), BF16),
        compiler_params=_params(2),
        name="pack_w_in",
    )(w_in_t)


def _inproj_kernel(x_ref, mod_ref, g_ref, w_ref, wfb_ref, proj_ref, fb_ref, *, tn):
    h = _rms_mod(x_ref[...], g_ref[...], mod_ref[0:1, :], mod_ref[1:2, :]).astype(BF16)
    for j in range(PROJ_COLS // tn):
        sl = slice(j * tn, (j + 1) * tn)
        proj_ref[:, sl] = jnp.dot(h, w_ref[:, sl], preferred_element_type=F32).astype(BF16)
    fb_ref[...] = jnp.dot(h, wfb_ref[...], preferred_element_type=F32)


def _in_proj(x, mod, gain, w_qkv, w_fb, layer):
    tn = 768
    steps_per_batch = SEQ // TM
    return pl.pallas_call(
        functools.partial(_inproj_kernel, tn=tn),
        grid=(TOKENS // TM,),
        in_specs=[
            pl.BlockSpec((TM, D_MODEL), lambda i: (i, 0)),
            pl.BlockSpec((None, None, 6, D_MODEL), lambda i: (layer, i // steps_per_batch, 0, 0)),
            pl.BlockSpec((None, 1, D_MODEL), lambda i: (layer, 0, 0)),
            pl.BlockSpec((None, D_MODEL, PROJ_COLS), lambda i: (layer, 0, 0)),
            pl.BlockSpec((None, D_MODEL, LANES), lambda i: (layer, 0, 0)),
        ],
        out_specs=[
            pl.BlockSpec((TM, PROJ_COLS), lambda i: (i, 0)),
            pl.BlockSpec((TM, LANES), lambda i: (i, 0)),
        ],
        out_shape=[
            jax.ShapeDtypeStruct((TOKENS, PROJ_COLS), BF16),
            jax.ShapeDtypeStruct((TOKENS, LANES), F32),
        ],
        compiler_params=_params(1),
        name="in_proj",
    )(x, mod, gain, w_qkv, w_fb)


def _cumsum_kernel(fb_ref, bias_ref, col_ref, row_ref):
    r = lax.broadcasted_iota(jnp.int32, (CUM_BLK, CUM_BLK), 0)
    c = lax.broadcasted_iota(jnp.int32, (CUM_BLK, CUM_BLK), 1)
    tri = (r >= c).astype(BF16)
    carry = jnp.zeros((1, LANES), F32)
    for blk in range(SEQ // CUM_BLK):
        rows = slice(blk * CUM_BLK, (blk + 1) * CUM_BLK)
        z = fb_ref[0, rows, :] + bias_ref[...]
        log_f = jnp.minimum(z, 0.0) - jnp.log1p(jnp.exp(-jnp.abs(z)))
        hi = log_f.astype(BF16)
        rest = log_f - hi.astype(F32)
        mid = rest.astype(BF16)
        lo = (rest - mid.astype(F32)).astype(BF16)
        cum = (jnp.dot(tri, hi, preferred_element_type=F32) + jnp.dot(tri, mid, preferred_element_type=F32)
               + jnp.dot(tri, lo, preferred_element_type=F32)) + carry
        col_ref[0, rows, :] = cum
        row_ref[0, :, rows] = cum.T[:B_HEADS, :]
        carry = cum[CUM_BLK - 1:CUM_BLK, :]


def _forget_cumsum(fb, b_forget_row):
    return pl.pallas_call(
        _cumsum_kernel,
        grid=(BATCH,),
        in_specs=[
            pl.BlockSpec((1, SEQ, LANES), lambda b: (b, 0, 0)),
            pl.BlockSpec((1, LANES), lambda b: (0, 0)),
        ],
        out_specs=[
            pl.BlockSpec((1, SEQ, LANES), lambda b: (b, 0, 0)),
            pl.BlockSpec((1, B_HEADS, SEQ), lambda b: (b, 0, 0)),
        ],
        out_shape=[
            jax.ShapeDtypeStruct((BATCH, SEQ, LANES), F32),
            jax.ShapeDtypeStruct((BATCH, B_HEADS, SEQ), F32),
        ],
        compiler_params=_params(1),
        name="forget_cumsum",
    )(fb.reshape(BATCH, SEQ, LANES), b_forget_row)


def _fox_kernel(q_ref, k_ref, v_ref, ccol_ref, crow_ref, o_ref, vt_ref, ckb_ref,
                s0_ref, s1_ref, p0_ref, p1_ref, acc_ref, m_ref, a0_ref, a1_ref, cq_ref, qm_ref):
    tq, tk = FOX_TQ, FOX_TK
    heads = 2 * FOX_PAIRS
    grp = pl.program_id(1)
    first = _lane_half((1, LANES))

    head_lane = lax.broadcasted_iota(jnp.int32, (1, LANES), 1)
    ones_row = (lax.broadcasted_iota(jnp.int32, (FOX_VROWS - HEAD_DIM, tk), 0) == 0).astype(BF16)
    for jt in range(SEQ // tk):
        rows = slice(jt * tk, (jt + 1) * tk)
        v_t = v_ref[rows, :].astype(F32).T.astype(BF16)
        ccol = ccol_ref[0, rows, :]
        for h in range(heads):
            vt_ref[jt, h * FOX_VROWS:h * FOX_VROWS + HEAD_DIM, :] = v_t[h * HEAD_DIM:(h + 1) * HEAD_DIM]
            vt_ref[jt, h * FOX_VROWS + HEAD_DIM:(h + 1) * FOX_VROWS, :] = ones_row
            col = jnp.sum(jnp.where(head_lane == grp * heads + h, ccol, 0.0), axis=1, keepdims=True)
            ckb_ref[h, rows, :] = jnp.broadcast_to(col * LOG2E, (tk, LANES))

    for qi in range(SEQ // tq):
        _fox_query_tile(qi, grp, first, q_ref, k_ref, crow_ref, o_ref, vt_ref, ckb_ref, s0_ref, s1_ref,
                        p0_ref, p1_ref, acc_ref, m_ref, a0_ref, a1_ref, cq_ref.at[qi & 1], qm_ref.at[qi & 1])


def _fox_query_tile(qi, grp, first, q_ref, k_ref, crow_ref, o_ref, vt_ref, ckb_ref, s0_ref, s1_ref,
                    p0_ref, p1_ref, acc_ref, m_ref, a0_ref, a1_ref, cq_ref, qm_ref):
    tq, tk = FOX_TQ, FOX_TK
    heads = 2 * FOX_PAIRS
    q_rows = slice(qi * tq, (qi + 1) * tq)

    for h in range(heads):
        q2 = q_ref[q_rows, (h // 2) * LANES:(h // 2 + 1) * LANES]
        qm_ref[h] = jnp.where(first if h % 2 == 0 else ~first, q2, jnp.zeros_like(q2))
        cq_ref[h] = crow_ref[0, grp * heads + h, qi:qi + 1, :] * LOG2E

    whole, upper = slice(0, tq), slice(tq // 2, tq)

    def scores(j, s_ref, q=whole):
        row0 = _row_start(j, tk)
        for h in range(heads):
            kj = k_ref[pl.ds(row0, tk), (h // 2) * LANES:(h // 2 + 1) * LANES]
            s = lax.dot_general(kj, qm_ref[h, q, :], (((1,), (1,)), ((), ())), preferred_element_type=F32)
            ck = ckb_ref[h, pl.ds(row0, tk), :]
            s_ref[h, :, q] = s - jnp.concatenate([ck] * ((q.stop - q.start) // LANES), axis=1)

    def softmax(s_ref, p_ref, a_ref, mask, q=whole):
        for h in range(heads):
            s = s_ref[h, :, q]
            if mask is not None:
                s = jnp.where(mask, s, NEG_INF)
            m_old = m_ref[h, :, q]
            cq = cq_ref[h, :, q]
            m_new = jnp.maximum(m_old, jnp.max(s, axis=0, keepdims=True) + cq)
            p_ref[h, :, q] = jnp.exp2(s + (cq - m_new)).astype(BF16)
            m_ref[h, :, q] = m_new
            a_ref[h, :, q] = jnp.exp2(m_old - m_new)

    def accumulate(j, p_ref, a_ref, q=whole):
        tile = max(j, 0)
        for h in range(heads):
            vt = vt_ref[tile, h * FOX_VROWS:(h + 1) * FOX_VROWS, :]
            pv = jnp.dot(vt, p_ref[h, :, q], preferred_element_type=F32)
            acc_ref[h, :, q] = a_ref[h, :, q] * acc_ref[h, :, q] + pv

    def trip(i, last):
        cur = i & 1
        even = 2 * i
        odd_q = upper if last else whole
        mask_even = mask_odd = None
        if last:
            causal = lambda width: (lax.broadcasted_iota(jnp.int32, (tk, width), 0)
                                    <= lax.broadcasted_iota(jnp.int32, (tk, width), 1))
            mask_even = causal(tq)
            mask_odd = causal(tq // 2)
        scores(even + 1, s1_ref, odd_q)
        accumulate(even - 1, p1_ref.at[1 - cur], a1_ref.at[1 - cur])
        softmax(s0_ref.at[cur], p0_ref, a0_ref, mask_even)
        if not last:
            scores(even + 2, s0_ref.at[1 - cur])
        accumulate(even, p0_ref, a0_ref)
        softmax(s1_ref, p1_ref.at[cur], a1_ref.at[cur], mask_odd, odd_q)
        if last:
            accumulate(even + 1, p1_ref.at[cur], a1_ref.at[cur], odd_q)

    acc_ref[...] = jnp.zeros(acc_ref.shape, F32)
    m_ref[...] = jnp.full(m_ref.shape, NEG_INF, F32)
    p1_ref[1] = jnp.zeros(p1_ref.shape[1:], BF16)
    a1_ref[1] = jnp.ones(a1_ref.shape[1:], F32)
    scores(0, s0_ref.at[0])
    for i in range(qi):
        trip(i, last=False)
    trip(qi, last=True)
    out_t = jnp.concatenate([acc_ref[h, :HEAD_DIM, :] / acc_ref[h, HEAD_DIM:HEAD_DIM + 1, :]
                             for h in range(heads)], axis=0)
    o_ref[q_rows, :] = out_t.T.astype(BF16)


def _fox_attention(proj, ccol, crow):
    tq, tk = FOX_TQ, FOX_TK
    nq = SEQ // tq
    heads = 2 * FOX_PAIRS
    width = FOX_PAIRS * LANES
    return pl.pallas_call(
        _fox_kernel,
        grid=(BATCH, N_PAIRS // FOX_PAIRS),
        in_specs=[
            pl.BlockSpec((SEQ, width), lambda b, g: (b, QB_BLK // FOX_PAIRS + g)),
            pl.BlockSpec((SEQ, width), lambda b, g: (b, KB_BLK // FOX_PAIRS + g)),
            pl.BlockSpec((SEQ, width), lambda b, g: (b, VB_BLK // FOX_PAIRS + g)),
            pl.BlockSpec((1, SEQ, LANES), lambda b, g: (b, 0, 0)),
            pl.BlockSpec((1, B_HEADS, nq, tq), lambda b, g: (b, 0, 0, 0)),
        ],
        out_specs=pl.BlockSpec((SEQ, width), lambda b, g: (b, g)),
        out_shape=jax.ShapeDtypeStruct((TOKENS, BRANCH_WIDTH), BF16),
        scratch_shapes=[pltpu.VMEM((SEQ // tk, heads * FOX_VROWS, tk), BF16),
                        pltpu.VMEM((heads, SEQ, LANES), F32),
                        pltpu.VMEM((2, heads, tk, tq), F32), pltpu.VMEM((heads, tk, tq), F32),
                        pltpu.VMEM((heads, tk, tq), BF16), pltpu.VMEM((2, heads, tk, tq), BF16),
                        pltpu.VMEM((heads, FOX_VROWS, tq), F32),
                        pltpu.VMEM((heads, 1, tq), F32),
                        pltpu.VMEM((heads, 1, tq), F32),
                        pltpu.VMEM((2, heads, 1, tq), F32),
                        pltpu.VMEM((2, heads, 1, tq), F32),
                        pltpu.VMEM((2, heads, tq, LANES), BF16)],
        compiler_params=_params(2),
        name="fox_attention",
    )(proj, proj, proj, ccol, crow.reshape(BATCH, B_HEADS, nq, tq))


def _swa_kernel(sink_ref, q_ref, k_ref, v_ref, o_ref, kd_ref, vd_ref,
                s0_ref, s1_ref, p0_ref, p1_ref, d0_ref, d1_ref, *, layer):
    rows = A_GROUP * CHUNK
    first = _lane_half((1, LANES))
    grp = lax.broadcasted_iota(jnp.int32, (rows, 1), 0) // CHUNK
    qi = lax.broadcasted_iota(jnp.int32, (rows, A_BAND), 0) % CHUNK
    si = lax.broadcasted_iota(jnp.int32, (rows, A_BAND), 1)

    for src_ref, dst_ref in ((k_ref, kd_ref), (v_ref, vd_ref)):
        for r0 in range(0, SEQ, 4 * CHUNK):
            x = src_ref[r0:r0 + 4 * CHUNK, :].astype(F32)
            swapped = pltpu.roll(x, HEAD_DIM, 1)
            dst_ref[0, r0:r0 + 4 * CHUNK, :] = jnp.where(first, x, swapped).astype(BF16)
            dst_ref[1, r0:r0 + 4 * CHUNK, :] = jnp.where(first, swapped, x).astype(BF16)

    slopes = []
    for kvh in range(A_KV_HEADS):
        slope = jnp.zeros((rows, 1), F32)
        for g in range(A_GROUP):
            slope = jnp.where(grp == g, 2.0 ** -(kvh * A_GROUP + g + 1) * LOG2E, slope)
        slopes.append(slope)

    def band_start(n):
        if isinstance(n, int):
            return max(n - A_PREV, 0) * CHUNK
        return _row_start(jnp.maximum(n - A_PREV, 0), CHUNK)

    def scores(n, s_ref, alibi):
        q0 = _row_start(n, CHUNK)
        for kvh in range(A_KV_HEADS):
            parts = []
            for g in range(A_GROUP):
                head = kvh * A_GROUP + g
                blk = q_ref[pl.ds(q0, CHUNK), (head // 2) * LANES:(head // 2 + 1) * LANES]
                parts.append(jnp.where(first if head % 2 == 0 else ~first, blk, jnp.zeros_like(blk)))
            qs = jnp.concatenate(parts, axis=0)
            kb = kd_ref[kvh, pl.ds(band_start(n), A_BAND), :]
            s = lax.dot_general(qs, kb, (((1,), (1,)), ((), ())), preferred_element_type=F32)
            s_ref[kvh] = s + alibi[kvh]

    def softmax(s_ref, p_ref, d_ref, last_key_chunk):
        key_chunk = lax.broadcasted_iota(jnp.int32, (SOFTMAX_ROWS, A_BAND), 1) // CHUNK
        for kvh in range(A_KV_HEADS):
            for r0 in range(0, rows, SOFTMAX_ROWS):
                blk = slice(r0, r0 + SOFTMAX_ROWS)
                sink = sink_ref[layer, kvh * A_GROUP + r0 // CHUNK] * LOG2E
                s = s_ref[kvh, blk, :]
                if last_key_chunk is not None:
                    s = jnp.where(key_chunk <= last_key_chunk, s, NEG_INF)
                m = jnp.maximum(jnp.max(s, axis=1, keepdims=True), sink)
                p = jnp.exp2(s - m)
                denom = jnp.sum(p, axis=1, keepdims=True) + jnp.exp2(sink - m)
                d_ref[kvh, blk, :] = jnp.broadcast_to(denom, (SOFTMAX_ROWS, LANES))
                p_ref[kvh, blk, :] = p.astype(BF16)

    def emit(n, p_ref, d_ref):
        q0 = _row_start(n, CHUNK)
        for kvh in range(A_KV_HEADS):
            vb = vd_ref[kvh, pl.ds(band_start(n), A_BAND), :]
            r = jnp.dot(p_ref[kvh], vb, preferred_element_type=F32) / d_ref[kvh]
            for pr in range(A_GROUP // 2):
                even = r[(2 * pr) * CHUNK:(2 * pr + 1) * CHUNK]
                odd = r[(2 * pr + 1) * CHUNK:(2 * pr + 2) * CHUNK]
                col = (kvh * (A_GROUP // 2) + pr) * LANES
                o_ref[pl.ds(q0, CHUNK), col:col + LANES] = jnp.where(first, even, odd).astype(BF16)

    def alibi_for(dist):
        return tuple(-slope * jnp.abs(dist).astype(F32) for slope in slopes)

    alibi = alibi_for(A_PREV * CHUNK + qi - si)
    def trip(i, carry=0, *, first_trip=False, last_trip=False):
        cur = i & 1
        even = 2 * i
        if first_trip:
            scores(1, s1_ref, alibi_for(CHUNK + qi - si))
            softmax(s0_ref.at[cur], p0_ref, d0_ref, 0)
        else:
            scores(even + 1, s1_ref, alibi)
            emit(even - 1, p1_ref.at[1 - cur], d1_ref.at[1 - cur])
            softmax(s0_ref.at[cur], p0_ref, d0_ref, None)
        if not last_trip:
            scores(even + 2, s0_ref.at[1 - cur], alibi)
        emit(even, p0_ref, d0_ref)
        softmax(s1_ref, p1_ref.at[cur], d1_ref.at[cur], 1 if first_trip else None)
        if last_trip:
            emit(even + 1, p1_ref.at[cur], d1_ref.at[cur])
        return carry

    scores(0, s0_ref.at[0], alibi_for(qi - si))
    trip(0, first_trip=True)
    for i in range(1, N_CHUNKS // 2 - 1):
        trip(i)
    trip(N_CHUNKS // 2 - 1, last_trip=True)


def _swa_attention(proj, sinks, layer):
    return pl.pallas_call(
        functools.partial(_swa_kernel, layer=layer),
        grid=(BATCH,),
        in_specs=[
            pl.BlockSpec(memory_space=pltpu.SMEM),
            pl.BlockSpec((SEQ, A_HEADS * HEAD_DIM), lambda b: (b, QA_BLK // N_PAIRS)),
            pl.BlockSpec((SEQ, LANES), lambda b: (b, KA_BLK)),
            pl.BlockSpec((SEQ, LANES), lambda b: (b, VA_BLK)),
        ],
        out_specs=pl.BlockSpec((SEQ, BRANCH_WIDTH), lambda b: (b, 0)),
        out_shape=jax.ShapeDtypeStruct((TOKENS, BRANCH_WIDTH), BF16),
        scratch_shapes=[pltpu.VMEM((A_KV_HEADS, SEQ, LANES), BF16),
                        pltpu.VMEM((A_KV_HEADS, SEQ, LANES), BF16),
                        pltpu.VMEM((2, A_KV_HEADS, A_GROUP * CHUNK, A_BAND), F32),
                        pltpu.VMEM((A_KV_HEADS, A_GROUP * CHUNK, A_BAND), F32),
                        pltpu.VMEM((A_KV_HEADS, A_GROUP * CHUNK, A_BAND), BF16),
                        pltpu.VMEM((2, A_KV_HEADS, A_GROUP * CHUNK, A_BAND), BF16),
                        pltpu.VMEM((A_KV_HEADS, A_GROUP * CHUNK, LANES), F32),
                        pltpu.VMEM((2, A_KV_HEADS, A_GROUP * CHUNK, LANES), F32)],
        compiler_params=_params(1),
        name="swa_attention",
    )(sinks, proj, proj, proj)


def _chunked_kernel(q_ref, k_ref, v_ref, bias_ref, o_ref, kp_ref, vp_ref,
                    s0_ref, s1_ref, p0_ref, p1_ref, d0_ref, d1_ref):
    first = _lane_half((1, LANES))
    width = N_PAIRS * LANES
    kp_ref[0:C_PAD, :] = jnp.zeros((C_PAD, width), BF16)
    vp_ref[0:C_PAD, :] = jnp.zeros((C_PAD, width), BF16)
    kp_ref[C_PAD:, :] = k_ref[...]
    vp_ref[C_PAD:, :] = v_ref[...]
    si = lax.broadcasted_iota(jnp.int32, (2 * CHUNK, C_BAND), 1)

    def scores(n, s_ref):
        q0 = _row_start(n, CHUNK)
        for pair in range(N_PAIRS):
            cols = slice(pair * LANES, (pair + 1) * LANES)
            q2 = q_ref[pl.ds(q0, CHUNK), cols]
            zero = jnp.zeros_like(q2)
            qs = jnp.concatenate([jnp.where(first, q2, zero), jnp.where(first, zero, q2)], axis=0)
            kb = kp_ref[pl.ds(q0, C_BAND), cols]
            s = lax.dot_general(qs, kb, (((1,), (1,)), ((), ())), preferred_element_type=F32)
            s_ref[pair] = s + bias_ref[pair]

    def softmax(n, s_ref, p_ref, d_ref, masked):
        for pair in range(N_PAIRS):
            s = s_ref[pair]
            if masked:
                s = jnp.where(n * CHUNK + si >= C_PAD, s, NEG_INF)
            p = jnp.exp2(s - jnp.max(s, axis=1, keepdims=True))
            d_ref[pair] = jnp.sum(p, axis=1, keepdims=True)
            p_ref[pair] = p.astype(BF16)

    def emit(n, p_ref, d_ref):
        q0 = _row_start(n, CHUNK)
        for pair in range(N_PAIRS):
            cols = slice(pair * LANES, (pair + 1) * LANES)
            vb = vp_ref[pl.ds(q0, C_BAND), cols]
            r = jnp.dot(p_ref[pair], vb, preferred_element_type=F32) / d_ref[pair]
            o_ref[pl.ds(q0, CHUNK), cols] = jnp.where(first, r[:CHUNK], r[CHUNK:]).astype(BF16)

    def trip(i, carry=0, *, masked, first_trip=False, last_trip=False):
        cur = i & 1
        even = 2 * i
        scores(even + 1, s1_ref)
        if not first_trip:
            emit(even - 1, p1_ref.at[1 - cur], d1_ref.at[1 - cur])
        softmax(even, s0_ref.at[cur], p0_ref, d0_ref, masked)
        if not last_trip:
            scores(even + 2, s0_ref.at[1 - cur])
        emit(even, p0_ref, d0_ref)
        softmax(even + 1, s1_ref, p1_ref.at[cur], d1_ref.at[cur], masked)
        if last_trip:
            emit(even + 1, p1_ref.at[cur], d1_ref.at[cur])
        return carry

    masked_trips = C_PREV // 2
    scores(0, s0_ref.at[0])
    trip(0, masked=True, first_trip=True)
    lax.fori_loop(1, masked_trips, functools.partial(trip, masked=True), 0)
    lax.fori_loop(masked_trips, N_CHUNKS // 2 - 1, functools.partial(trip, masked=False), 0)
    trip(N_CHUNKS // 2 - 1, masked=False, last_trip=True)


def _chunked_attention(proj, bias, layer):
    width = N_PAIRS * LANES
    return pl.pallas_call(
        _chunked_kernel,
        grid=(BATCH,),
        in_specs=[
            pl.BlockSpec((SEQ, width), lambda b: (b, QC_BLK // N_PAIRS)),
            pl.BlockSpec((SEQ, width), lambda b: (b, KC_BLK // N_PAIRS)),
            pl.BlockSpec((SEQ, width), lambda b: (b, VC_BLK // N_PAIRS)),
            pl.BlockSpec((None, N_PAIRS, 2 * CHUNK, C_BAND), lambda b: (layer, 0, 0, 0)),
        ],
        out_specs=pl.BlockSpec((SEQ, width), lambda b: (b, 0)),
        out_shape=jax.ShapeDtypeStruct((TOKENS, BRANCH_WIDTH), BF16),
        scratch_shapes=[pltpu.VMEM((C_PAD + SEQ, width), BF16), pltpu.VMEM((C_PAD + SEQ, width), BF16),
                        pltpu.VMEM((2, N_PAIRS, 2 * CHUNK, C_BAND), F32),
                        pltpu.VMEM((N_PAIRS, 2 * CHUNK, C_BAND), F32),
                        pltpu.VMEM((N_PAIRS, 2 * CHUNK, C_BAND), BF16),
                        pltpu.VMEM((2, N_PAIRS, 2 * CHUNK, C_BAND), BF16),
                        pltpu.VMEM((N_PAIRS, 2 * CHUNK, 1), F32),
                        pltpu.VMEM((2, N_PAIRS, 2 * CHUNK, 1), F32)],
        compiler_params=_params(1),
        name="chunked_attention",
    )(proj, proj, proj, bias)


def _merge_kernel(x_ref, mod_ref, g_ref, oa_ref, ob_ref, oc_ref, wb_ref, wg_ref, wo_ref, out_ref,
                  merged_ref, *, tn):
    h = _rms_mod(x_ref[...], g_ref[...], mod_ref[0:1, :], mod_ref[1:2, :]).astype(BF16)
    branches = (oa_ref[...], ob_ref[...], oc_ref[...])
    for n in range(D_MODEL // tn):
        acc = None
        for k, o in enumerate(branches):
            y = jnp.dot(o, wb_ref[k, :, n * tn:(n + 1) * tn], preferred_element_type=F32)
            gate = jnp.dot(h, wg_ref[:, k * D_MODEL + n * tn:k * D_MODEL + (n + 1) * tn],
                           preferred_element_type=F32)
            term = jax.nn.sigmoid(gate) * y
            acc = term if acc is None else acc + term
        merged_ref[:, n * tn:(n + 1) * tn] = acc.astype(BF16)
    merged = merged_ref[...]
    for n in range(D_MODEL // tn):
        sl = slice(n * tn, (n + 1) * tn)
        out = jnp.dot(merged, wo_ref[:, sl], preferred_element_type=F32)
        out_ref[:, sl] = x_ref[:, sl] + mod_ref[2:3, sl] * out


def _merge(x, mod, gain, o_a, o_b, o_c, w_branch, w_gate, w_out, layer):
    steps_per_batch = SEQ // TM
    row = lambda i: (i, 0)
    return pl.pallas_call(
        functools.partial(_merge_kernel, tn=256),
        grid=(TOKENS // TM,),
        in_specs=[
            pl.BlockSpec((TM, D_MODEL), row),
            pl.BlockSpec((None, None, 6, D_MODEL), lambda i: (layer, i // steps_per_batch, 0, 0)),
            pl.BlockSpec((None, 1, D_MODEL), lambda i: (layer, 0, 0)),
            pl.BlockSpec((TM, BRANCH_WIDTH), row),
            pl.BlockSpec((TM, BRANCH_WIDTH), row),
            pl.BlockSpec((TM, BRANCH_WIDTH), row),
            pl.BlockSpec((None, 3, BRANCH_WIDTH, D_MODEL), lambda i: (layer, 0, 0, 0)),
            pl.BlockSpec((None, D_MODEL, 3 * D_MODEL), lambda i: (layer, 0, 0)),
            pl.BlockSpec((None, D_MODEL, D_MODEL), lambda i: (layer, 0, 0)),
        ],
        out_specs=pl.BlockSpec((TM, D_MODEL), row),
        out_shape=jax.ShapeDtypeStruct((TOKENS, D_MODEL), F32),
        scratch_shapes=[pltpu.VMEM((TM, D_MODEL), BF16)],
        compiler_params=_params(1),
        name="merge_out",
    )(x, mod, gain, o_a, o_b, o_c, w_branch, w_gate, w_out)


def _ffn_kernel(x_ref, mod_ref, g_ref, gf_ref, wi_ref, wo_ref, out_ref, act_ref, *, tf, tn, final):
    h = _rms_mod(x_ref[...], g_ref[...], mod_ref[3:4, :], mod_ref[4:5, :]).astype(BF16)
    for c in range(FFN_HIDDEN // tf):
        gate = jnp.dot(h, wi_ref[:, c * tf:(c + 1) * tf], preferred_element_type=F32)
        up = jnp.dot(h, wi_ref[:, FFN_HIDDEN + c * tf:FFN_HIDDEN + (c + 1) * tf],
                     preferred_element_type=F32)
        act_ref[:, c * tf:(c + 1) * tf] = (gate * jax.nn.sigmoid(gate) * up).astype(BF16)
    act = act_ref[...]
    for n in range(D_MODEL // tn):
        sl = slice(n * tn, (n + 1) * tn)
        out = jnp.dot(act, wo_ref[:, sl], preferred_element_type=F32)
        out_ref[:, sl] = x_ref[:, sl] + mod_ref[5:6, sl] * out
    if final:
        y = out_ref[...]
        ms = jnp.mean(y * y, axis=-1, keepdims=True)
        out_ref[...] = y * lax.rsqrt(ms + EPS) * gf_ref[...]


def _ffn(x, mod, gain, final_gain, w_ffn_in, w_ffn_out, layer, final):
    steps_per_batch = SEQ // TM
    row = lambda i: (i, 0)
    return pl.pallas_call(
        functools.partial(_ffn_kernel, tf=256, tn=256, final=final),
        grid=(TOKENS // TM,),
        in_specs=[
            pl.BlockSpec((TM, D_MODEL), row),
            pl.BlockSpec((None, None, 6, D_MODEL), lambda i: (layer, i // steps_per_batch, 0, 0)),
            pl.BlockSpec((None, 1, D_MODEL), lambda i: (layer, 0, 0)),
            pl.BlockSpec((1, D_MODEL), lambda i: (0, 0)),
            pl.BlockSpec((None, D_MODEL, 2 * FFN_HIDDEN), lambda i: (layer, 0, 0), pipeline_mode=pl.Buffered(1)),
            pl.BlockSpec((None, FFN_HIDDEN, D_MODEL), lambda i: (layer, 0, 0), pipeline_mode=pl.Buffered(1)),
        ],
        out_specs=pl.BlockSpec((TM, D_MODEL), row),
        out_shape=jax.ShapeDtypeStruct((TOKENS, D_MODEL), F32),
        scratch_shapes=[pltpu.VMEM((TM, FFN_HIDDEN), BF16)],
        compiler_params=_params(1),
        name="ffn",
    )(x, mod, gain, final_gain, w_ffn_in, w_ffn_out)


def kernel(x, c, norm_mix_g, norm_ffn_g, w_ada, b_ada, w_in, b_forget, sinks, rel_bias,
           w_branch, w_out, w_ffn_in, w_ffn_out, final_norm_g):
    scale = HEAD_DIM ** -0.5
    w_in_t = jnp.swapaxes(w_in, 1, 2)
    w_qkv = _pack_w_in(w_in_t, (IN_B0, IN_C0, IN_A0), (IN_FB0 - IN_B0, IN_GATE0 - IN_C0, IN_B0 - IN_A0),
                       (scale * LOG2E,) * 3)
    w_gate = _pack_w_in(w_in_t, (IN_GATE0,), (N_IN_COLS - IN_GATE0,), (None,))
    w_fb = jnp.pad(w_in[:, :, IN_FB0:IN_FB0 + B_HEADS], ((0, 0), (0, 0), (0, LANES - B_HEADS))).astype(BF16)
    w_branch_b = w_branch.astype(BF16)
    w_out_b = w_out.astype(BF16)
    w_ffn_in_b = w_ffn_in.astype(BF16)
    w_ffn_out_b = w_ffn_out.astype(BF16)
    b_forget_rows = jnp.pad(b_forget, ((0, 0), (0, LANES - B_HEADS))).reshape(DEPTH, 1, LANES)
    gain_mix = norm_mix_g.reshape(DEPTH, 1, D_MODEL)
    gain_ffn = norm_ffn_g.reshape(DEPTH, 1, D_MODEL)
    gain_final = final_norm_g.reshape(1, D_MODEL)

    mod = _ada_mod(c, w_ada, b_ada)
    bias_c = _rel_bias(rel_bias)

    xt = x.reshape(TOKENS, D_MODEL)
    for layer in range(DEPTH):
        proj, fb = _in_proj(xt, mod, gain_mix, w_qkv, w_fb, layer)
        ccol, crow = _forget_cumsum(fb, b_forget_rows[layer])
        o_a = _swa_attention(proj, sinks, layer)
        o_b = _fox_attention(proj, ccol, crow)
        o_c = _chunked_attention(proj, bias_c, layer)
        xt = _merge(xt, mod, gain_mix, o_a, o_b, o_c, w_branch_b, w_gate, w_out_b, layer)
        xt = _ffn(xt, mod, gain_ffn, gain_final, w_ffn_in_b, w_ffn_out_b, layer, layer == DEPTH - 1)
    return xt.reshape(BATCH, SEQ, D_MODEL)
```

```python
import functools

import jax
import jax.numpy as jnp
from jax import lax
from jax.experimental import pallas as pl
from jax.experimental.pallas import tpu as pltpu

F32 = jnp.float32
BF16 = jnp.bfloat16

D_MODEL = 1024
BATCH = 8
SEQ = 2048
TOKENS = BATCH * SEQ
DEPTH = 2
CHUNK = 64
HEAD_DIM = 64
EPS = 1e-6
NEG_INF = -1e30
LOG2E = 1.4426950408889634

A_HEADS = 8
A_KV_HEADS = 2
A_GROUP = A_HEADS // A_KV_HEADS
A_PREV = 2
A_BAND = (A_PREV + 1) * CHUNK
B_HEADS = 8
C_HEADS = 8
C_PREV = 8
C_PAD = C_PREV * CHUNK
C_BAND = (C_PREV + 1) * CHUNK
REL_CLIP = 128
N_REL = 2 * REL_CLIP + 1
BRANCH_WIDTH = 512
FFN_HIDDEN = 2816
N_CHUNKS = SEQ // CHUNK
N_IN_COLS = 6920

LANES = 128
N_PAIRS = 4

PROJ_COLS = 3840
QB_BLK, KB_BLK, VB_BLK = 0, 4, 8
QC_BLK, KC_BLK, VC_BLK = 12, 16, 20
QA_BLK, KA_BLK, VA_BLK = 24, 28, 29

TM = 1024
FOX_TQ = 512
FOX_TK = 256
FOX_VROWS = HEAD_DIM + 16
FOX_PAIRS = 2
SOFTMAX_ROWS = 32
CUM_BLK = 256

VMEM_LIMIT = 56 * 1024 * 1024


def _params(n_axes):
    return pltpu.CompilerParams(dimension_semantics=("arbitrary",) * n_axes,
                                vmem_limit_bytes=VMEM_LIMIT)


def _rms_mod(x, g, shift, scale):
    ms = jnp.mean(x * x, axis=-1, keepdims=True)
    y = x * lax.rsqrt(ms + EPS) * g
    return y * (1.0 + scale) + shift


def _row_start(index, size):
    if isinstance(index, int):
        return index * size
    return pl.multiple_of(index * size, size)


def _lane_half(shape):
    return lax.broadcasted_iota(jnp.int32, shape, len(shape) - 1) < HEAD_DIM


def _ada_kernel(c_ref, w_ref, b_ref, o_ref):
    c = c_ref[...]
    cond = c * jax.nn.sigmoid(c)
    o_ref[...] = jnp.dot(cond.astype(BF16), w_ref[...].astype(BF16),
                         preferred_element_type=F32) + b_ref[...]


def _ada_mod(c, w_ada, b_ada):
    n_blk = 6
    out = pl.pallas_call(
        _ada_kernel,
        grid=(DEPTH, n_blk),
        in_specs=[
            pl.BlockSpec((BATCH, D_MODEL), lambda l, j: (0, 0)),
            pl.BlockSpec((None, D_MODEL, D_MODEL), lambda l, j: (l, 0, j)),
            pl.BlockSpec((None, 1, D_MODEL), lambda l, j: (l, 0, j)),
        ],
        out_specs=pl.BlockSpec((None, BATCH, D_MODEL), lambda l, j: (l, 0, j)),
        out_shape=jax.ShapeDtypeStruct((DEPTH, BATCH, n_blk * D_MODEL), F32),
        compiler_params=_params(2),
        name="ada_mod",
    )(c, w_ada, b_ada.reshape(DEPTH, 1, n_blk * D_MODEL))
    return out.reshape(DEPTH, BATCH, n_blk, D_MODEL)


def _relbias_kernel(rev_ref, o_ref):
    near = C_PAD - REL_CLIP
    width = 2 * REL_CLIP
    q = lax.broadcasted_iota(jnp.int32, (CHUNK, width), 0)
    c = lax.broadcasted_iota(jnp.int32, (CHUNK, width), 1)
    for h in range(C_HEADS):
        rev = rev_ref[h:h + 1, :] * LOG2E
        far = rev[:, 0:1]
        rolled = pltpu.roll(jnp.broadcast_to(rev, (CHUNK, width)), 0, 1, stride=1, stride_axis=0)
        o_ref[h, :, 0:near] = jnp.broadcast_to(far, (CHUNK, near))
        o_ref[h, :, near:C_BAND] = jnp.where(c >= q, rolled, far)[:, :C_BAND - near]


def _rel_bias(rel_bias):
    rev = rel_bias[:, :, ::-1][:, :, :2 * REL_CLIP]
    out = pl.pallas_call(
        _relbias_kernel,
        grid=(DEPTH,),
        in_specs=[pl.BlockSpec((None, C_HEADS, 2 * REL_CLIP), lambda l: (l, 0, 0))],
        out_specs=pl.BlockSpec((None, C_HEADS, CHUNK, C_BAND), lambda l: (l, 0, 0, 0)),
        out_shape=jax.ShapeDtypeStruct((DEPTH, C_HEADS, CHUNK, C_BAND), F32),
        compiler_params=_params(1),
        name="rel_bias",
    )(rev)
    return out.reshape(DEPTH, N_PAIRS, 2 * CHUNK, C_BAND)


IN_A0, IN_B0, IN_FB0, IN_C0, IN_GATE0 = 0, 768, 2304, 2312, 3848
PACK_ROWS = 768


def _pack_kernel(wt_ref, o_ref, *, region_starts, region_steps, q_scales):
    j = pl.program_id(1)
    w = wt_ref[0]
    first_step = 0
    row = lax.broadcasted_iota(jnp.int32, (PACK_ROWS, 1), 0)
    scale = jnp.ones((PACK_ROWS, 1), F32)
    for steps, q_scale in zip(region_steps, q_scales):
        if q_scale is not None:
            in_region = (j >= first_step) & (j < first_step + steps)
            col_in_region = (j - first_step) * PACK_ROWS + row
            scale = jnp.where(in_region & (col_in_region < A_HEADS * HEAD_DIM), q_scale, scale)
        first_step += steps
    o_ref[...] = (w * scale).T.astype(BF16)


def _pack_w_in(w_in_t, region_starts, region_widths, q_scales):
    region_steps = tuple(w // PACK_ROWS for w in region_widths)

    def source_row(j):
        row = jnp.int32(0)
        first_step = 0
        for start, steps in zip(region_starts, region_steps):
            row = jnp.where(j >= first_step, start + (j - first_step) * PACK_ROWS, row)
            first_step += steps
        return pl.multiple_of(row, 8)

    return pl.pallas_call(
        functools.partial(_pack_kernel, region_starts=region_starts, region_steps=region_steps, q_scales=q_scales),
        grid=(DEPTH, sum(region_steps)),
        in_specs=[pl.BlockSpec((pl.Element(1), pl.Element(PACK_ROWS), pl.Element(D_MODEL)),
                               lambda l, j: (l, source_row(j), 0))],
        out_specs=pl.BlockSpec((None, D_MODEL, PACK_ROWS), lambda l, j: (l, 0, j)),
        out_shape=jax.ShapeDtypeStruct((DEPTH, D_MODEL, sum(region_widths)), BF16),
        compiler_params=_params(2),
        name="pack_w_in",
    )(w_in_t)


def _inproj_kernel(x_ref, mod_ref, g_ref, w_ref, wfb_ref, proj_ref, fb_ref, *, tn):
    h = _rms_mod(x_ref[...], g_ref[...], mod_ref[0:1, :], mod_ref[1:2, :]).astype(BF16)
    for j in range(PROJ_COLS // tn):
        sl = slice(j * tn, (j + 1) * tn)
        proj_ref[:, sl] = jnp.dot(h, w_ref[:, sl], preferred_element_type=F32).astype(BF16)
    fb_ref[...] = jnp.dot(h, wfb_ref[...], preferred_element_type=F32)


def _in_proj(x, mod, gain, w_qkv, w_fb, layer):
    tn = 768
    steps_per_batch = SEQ // TM
    return pl.pallas_call(
        functools.partial(_inproj_kernel, tn=tn),
        grid=(TOKENS // TM,),
        in_specs=[
            pl.BlockSpec((TM, D_MODEL), lambda i: (i, 0)),
            pl.BlockSpec((None, None, 6, D_MODEL), lambda i: (layer, i // steps_per_batch, 0, 0)),
            pl.BlockSpec((None, 1, D_MODEL), lambda i: (layer, 0, 0)),
            pl.BlockSpec((None, D_MODEL, PROJ_COLS), lambda i: (layer, 0, 0)),
            pl.BlockSpec((None, D_MODEL, LANES), lambda i: (layer, 0, 0)),
        ],
        out_specs=[
            pl.BlockSpec((TM, PROJ_COLS), lambda i: (i, 0)),
            pl.BlockSpec((TM, LANES), lambda i: (i, 0)),
        ],
        out_shape=[
            jax.ShapeDtypeStruct((TOKENS, PROJ_COLS), BF16),
            jax.ShapeDtypeStruct((TOKENS, LANES), F32),
        ],
        compiler_params=_params(1),
        name="in_proj",
    )(x, mod, gain, w_qkv, w_fb)


def _cumsum_kernel(fb_ref, bias_ref, col_ref, row_ref):
    r = lax.broadcasted_iota(jnp.int32, (CUM_BLK, CUM_BLK), 0)
    c = lax.broadcasted_iota(jnp.int32, (CUM_BLK, CUM_BLK), 1)
    tri = (r >= c).astype(BF16)
    carry = jnp.zeros((1, LANES), F32)
    for blk in range(SEQ // CUM_BLK):
        rows = slice(blk * CUM_BLK, (blk + 1) * CUM_BLK)
        z = fb_ref[0, rows, :] + bias_ref[...]
        log_f = jnp.minimum(z, 0.0) - jnp.log1p(jnp.exp(-jnp.abs(z)))
        hi = log_f.astype(BF16)
        rest = log_f - hi.astype(F32)
        mid = rest.astype(BF16)
        lo = (rest - mid.astype(F32)).astype(BF16)
        cum = (jnp.dot(tri, hi, preferred_element_type=F32) + jnp.dot(tri, mid, preferred_element_type=F32)
               + jnp.dot(tri, lo, preferred_element_type=F32)) + carry
        col_ref[0, rows, :] = cum
        row_ref[0, :, rows] = cum.T[:B_HEADS, :]
        carry = cum[CUM_BLK - 1:CUM_BLK, :]


def _forget_cumsum(fb, b_forget_row):
    return pl.pallas_call(
        _cumsum_kernel,
        grid=(BATCH,),
        in_specs=[
            pl.BlockSpec((1, SEQ, LANES), lambda b: (b, 0, 0)),
            pl.BlockSpec((1, LANES), lambda b: (0, 0)),
        ],
        out_specs=[
            pl.BlockSpec((1, SEQ, LANES), lambda b: (b, 0, 0)),
            pl.BlockSpec((1, B_HEADS, SEQ), lambda b: (b, 0, 0)),
        ],
        out_shape=[
            jax.ShapeDtypeStruct((BATCH, SEQ, LANES), F32),
            jax.ShapeDtypeStruct((BATCH, B_HEADS, SEQ), F32),
        ],
        compiler_params=_params(1),
        name="forget_cumsum",
    )(fb.reshape(BATCH, SEQ, LANES), b_forget_row)


def _fox_kernel(q_ref, k_ref, v_ref, ccol_ref, crow_ref, o_ref, vt_ref, ckb_ref,
                s0_ref, s1_ref, p0_ref, p1_ref, acc_ref, m_ref, a0_ref, a1_ref, cq_ref, qm_ref):
    tq, tk = FOX_TQ, FOX_TK
    heads = 2 * FOX_PAIRS
    grp = pl.program_id(1)
    first = _lane_half((1, LANES))

    head_lane = lax.broadcasted_iota(jnp.int32, (1, LANES), 1)
    ones_row = (lax.broadcasted_iota(jnp.int32, (FOX_VROWS - HEAD_DIM, tk), 0) == 0).astype(BF16)
    for jt in range(SEQ // tk):
        rows = slice(jt * tk, (jt + 1) * tk)
        v_t = v_ref[rows, :].astype(F32).T.astype(BF16)
        ccol = ccol_ref[0, rows, :]
        for h in range(heads):
            vt_ref[jt, h * FOX_VROWS:h * FOX_VROWS + HEAD_DIM, :] = v_t[h * HEAD_DIM:(h + 1) * HEAD_DIM]
            vt_ref[jt, h * FOX_VROWS + HEAD_DIM:(h + 1) * FOX_VROWS, :] = ones_row
            col = jnp.sum(jnp.where(head_lane == grp * heads + h, ccol, 0.0), axis=1, keepdims=True)
            ckb_ref[h, rows, :] = jnp.broadcast_to(col * LOG2E, (tk, LANES))

    for qi in range(SEQ // tq):
        _fox_query_tile(qi, grp, first, q_ref, k_ref, crow_ref, o_ref, vt_ref, ckb_ref, s0_ref, s1_ref,
                        p0_ref, p1_ref, acc_ref, m_ref, a0_ref, a1_ref, cq_ref.at[qi & 1], qm_ref.at[qi & 1])


def _fox_query_tile(qi, grp, first, q_ref, k_ref, crow_ref, o_ref, vt_ref, ckb_ref, s0_ref, s1_ref,
                    p0_ref, p1_ref, acc_ref, m_ref, a0_ref, a1_ref, cq_ref, qm_ref):
    tq, tk = FOX_TQ, FOX_TK
    heads = 2 * FOX_PAIRS
    q_rows = slice(qi * tq, (qi + 1) * tq)

    for h in range(heads):
        q2 = q_ref[q_rows, (h // 2) * LANES:(h // 2 + 1) * LANES]
        qm_ref[h] = jnp.where(first if h % 2 == 0 else ~first, q2, jnp.zeros_like(q2))
        cq_ref[h] = crow_ref[0, grp * heads + h, qi:qi + 1, :] * LOG2E

    whole, upper = slice(0, tq), slice(tq // 2, tq)

    def scores(j, s_ref, q=whole):
        row0 = _row_start(j, tk)
        for h in range(heads):
            kj = k_ref[pl.ds(row0, tk), (h // 2) * LANES:(h // 2 + 1) * LANES]
            s = lax.dot_general(kj, qm_ref[h, q, :], (((1,), (1,)), ((), ())), preferred_element_type=F32)
            ck = ckb_ref[h, pl.ds(row0, tk), :]
            s_ref[h, :, q] = s - jnp.concatenate([ck] * ((q.stop - q.start) // LANES), axis=1)

    def softmax(s_ref, p_ref, a_ref, mask, q=whole):
        for h in range(heads):
            s = s_ref[h, :, q]
            if mask is not None:
                s = jnp.where(mask, s, NEG_INF)
            m_old = m_ref[h, :, q]
            cq = cq_ref[h, :, q]
            m_new = jnp.maximum(m_old, jnp.max(s, axis=0, keepdims=True) + cq)
            p_ref[h, :, q] = jnp.exp2(s + (cq - m_new)).astype(BF16)
            m_ref[h, :, q] = m_new
            a_ref[h, :, q] = jnp.exp2(m_old - m_new)

    def accumulate(j, p_ref, a_ref, q=whole):
        tile = max(j, 0)
        for h in range(heads):
            vt = vt_ref[tile, h * FOX_VROWS:(h + 1) * FOX_VROWS, :]
            pv = jnp.dot(vt, p_ref[h, :, q], preferred_element_type=F32)
            acc_ref[h, :, q] = a_ref[h, :, q] * acc_ref[h, :, q] + pv

    def trip(i, last):
        cur = i & 1
        even = 2 * i
        odd_q = upper if last else whole
        mask_even = mask_odd = None
        if last:
            causal = lambda width: (lax.broadcasted_iota(jnp.int32, (tk, width), 0)
                                    <= lax.broadcasted_iota(jnp.int32, (tk, width), 1))
            mask_even = causal(tq)
            mask_odd = causal(tq // 2)
        scores(even + 1, s1_ref, odd_q)
        accumulate(even - 1, p1_ref.at[1 - cur], a1_ref.at[1 - cur])
        softmax(s0_ref.at[cur], p0_ref, a0_ref, mask_even)
        if not last:
            scores(even + 2, s0_ref.at[1 - cur])
        accumulate(even, p0_ref, a0_ref)
        softmax(s1_ref, p1_ref.at[cur], a1_ref.at[cur], mask_odd, odd_q)
        if last:
            accumulate(even + 1, p1_ref.at[cur], a1_ref.at[cur], odd_q)

    acc_ref[...] = jnp.zeros(acc_ref.shape, F32)
    m_ref[...] = jnp.full(m_ref.shape, NEG_INF, F32)
    p1_ref[1] = jnp.zeros(p1_ref.shape[1:], BF16)
    a1_ref[1] = jnp.ones(a1_ref.shape[1:], F32)
    scores(0, s0_ref.at[0])
    for i in range(qi):
        trip(i, last=False)
    trip(qi, last=True)
    out_t = jnp.concatenate([acc_ref[h, :HEAD_DIM, :] / acc_ref[h, HEAD_DIM:HEAD_DIM + 1, :]
                             for h in range(heads)], axis=0)
    o_ref[q_rows, :] = out_t.T.astype(BF16)


def _fox_attention(proj, ccol, crow):
    tq, tk = FOX_TQ, FOX_TK
    nq = SEQ // tq
    heads = 2 * FOX_PAIRS
    width = FOX_PAIRS * LANES
    return pl.pallas_call(
        _fox_kernel,
        grid=(BATCH, N_PAIRS // FOX_PAIRS),
        in_specs=[
            pl.BlockSpec((SEQ, width), lambda b, g: (b, QB_BLK // FOX_PAIRS + g)),
            pl.BlockSpec((SEQ, width), lambda b, g: (b, KB_BLK // FOX_PAIRS + g)),
            pl.BlockSpec((SEQ, width), lambda b, g: (b, VB_BLK // FOX_PAIRS + g)),
            pl.BlockSpec((1, SEQ, LANES), lambda b, g: (b, 0, 0)),
            pl.BlockSpec((1, B_HEADS, nq, tq), lambda b, g: (b, 0, 0, 0)),
        ],
        out_specs=pl.BlockSpec((SEQ, width), lambda b, g: (b, g)),
        out_shape=jax.ShapeDtypeStruct((TOKENS, BRANCH_WIDTH), BF16),
        scratch_shapes=[pltpu.VMEM((SEQ // tk, heads * FOX_VROWS, tk), BF16),
                        pltpu.VMEM((heads, SEQ, LANES), F32),
                        pltpu.VMEM((2, heads, tk, tq), F32), pltpu.VMEM((heads, tk, tq), F32),
                        pltpu.VMEM((heads, tk, tq), BF16), pltpu.VMEM((2, heads, tk, tq), BF16),
                        pltpu.VMEM((heads, FOX_VROWS, tq), F32),
                        pltpu.VMEM((heads, 1, tq), F32),
                        pltpu.VMEM((heads, 1, tq), F32),
                        pltpu.VMEM((2, heads, 1, tq), F32),
                        pltpu.VMEM((2, heads, 1, tq), F32),
                        pltpu.VMEM((2, heads, tq, LANES), BF16)],
        compiler_params=_params(2),
        name="fox_attention",
    )(proj, proj, proj, ccol, crow.reshape(BATCH, B_HEADS, nq, tq))


def _swa_kernel(sink_ref, q_ref, k_ref, v_ref, o_ref, kd_ref, vd_ref,
                s0_ref, s1_ref, p0_ref, p1_ref, d0_ref, d1_ref, *, layer):
    rows = A_GROUP * CHUNK
    first = _lane_half((1, LANES))
    grp = lax.broadcasted_iota(jnp.int32, (rows, 1), 0) // CHUNK
    qi = lax.broadcasted_iota(jnp.int32, (rows, A_BAND), 0) % CHUNK
    si = lax.broadcasted_iota(jnp.int32, (rows, A_BAND), 1)
    sel_r = lax.broadcasted_iota(jnp.int32, (LANES, LANES), 0)
    sel_c = lax.broadcasted_iota(jnp.int32, (LANES, LANES), 1)

    slopes = []
    for kvh in range(A_KV_HEADS):
        sel = (sel_r == kvh * HEAD_DIM + sel_c % HEAD_DIM).astype(BF16)
        kd_ref[kvh] = jnp.dot(k_ref[...], sel, preferred_element_type=F32).astype(BF16)
        vd_ref[kvh] = jnp.dot(v_ref[...], sel, preferred_element_type=F32).astype(BF16)
        slope = jnp.zeros((rows, 1), F32)
        for g in range(A_GROUP):
            slope = jnp.where(grp == g, 2.0 ** -(kvh * A_GROUP + g + 1) * LOG2E, slope)
        slopes.append(slope)

    def band_start(n):
        if isinstance(n, int):
            return max(n - A_PREV, 0) * CHUNK
        return _row_start(jnp.maximum(n - A_PREV, 0), CHUNK)

    def scores(n, s_ref, alibi):
        q0 = _row_start(n, CHUNK)
        for kvh in range(A_KV_HEADS):
            parts = []
            for g in range(A_GROUP):
                head = kvh * A_GROUP + g
                blk = q_ref[pl.ds(q0, CHUNK), (head // 2) * LANES:(head // 2 + 1) * LANES]
                parts.append(jnp.where(first if head % 2 == 0 else ~first, blk, jnp.zeros_like(blk)))
            qs = jnp.concatenate(parts, axis=0)
            kb = kd_ref[kvh, pl.ds(band_start(n), A_BAND), :]
            s = lax.dot_general(qs, kb, (((1,), (1,)), ((), ())), preferred_element_type=F32)
            s_ref[kvh] = s + alibi[kvh]

    def softmax(s_ref, p_ref, d_ref, last_key_chunk):
        key_chunk = lax.broadcasted_iota(jnp.int32, (SOFTMAX_ROWS, A_BAND), 1) // CHUNK
        for kvh in range(A_KV_HEADS):
            for r0 in range(0, rows, SOFTMAX_ROWS):
                blk = slice(r0, r0 + SOFTMAX_ROWS)
                sink = sink_ref[layer, kvh * A_GROUP + r0 // CHUNK] * LOG2E
                s = s_ref[kvh, blk, :]
                if last_key_chunk is not None:
                    s = jnp.where(key_chunk <= last_key_chunk, s, NEG_INF)
                m = jnp.maximum(jnp.max(s, axis=1, keepdims=True), sink)
                p = jnp.exp2(s - m)
                denom = jnp.sum(p, axis=1, keepdims=True) + jnp.exp2(sink - m)
                d_ref[kvh, blk, :] = jnp.broadcast_to(denom, (SOFTMAX_ROWS, LANES))
                p_ref[kvh, blk, :] = p.astype(BF16)

    def emit(n, p_ref, d_ref):
        q0 = _row_start(n, CHUNK)
        for kvh in range(A_KV_HEADS):
            vb = vd_ref[kvh, pl.ds(band_start(n), A_BAND), :]
            r = jnp.dot(p_ref[kvh], vb, preferred_element_type=F32) / d_ref[kvh]
            for pr in range(A_GROUP // 2):
                even = r[(2 * pr) * CHUNK:(2 * pr + 1) * CHUNK]
                odd = r[(2 * pr + 1) * CHUNK:(2 * pr + 2) * CHUNK]
                col = (kvh * (A_GROUP // 2) + pr) * LANES
                o_ref[pl.ds(q0, CHUNK), col:col + LANES] = jnp.where(first, even, odd).astype(BF16)

    def alibi_for(dist):
        return tuple(-slope * jnp.abs(dist).astype(F32) for slope in slopes)

    alibi = alibi_for(A_PREV * CHUNK + qi - si)
    def trip(i, carry=0, *, first_trip=False, last_trip=False):
        cur = i & 1
        even = 2 * i
        if first_trip:
            scores(1, s1_ref, alibi_for(CHUNK + qi - si))
            softmax(s0_ref.at[cur], p0_ref, d0_ref, 0)
        else:
            scores(even + 1, s1_ref, alibi)
            emit(even - 1, p1_ref.at[1 - cur], d1_ref.at[1 - cur])
            softmax(s0_ref.at[cur], p0_ref, d0_ref, None)
        if not last_trip:
            scores(even + 2, s0_ref.at[1 - cur], alibi)
        emit(even, p0_ref, d0_ref)
        softmax(s1_ref, p1_ref.at[cur], d1_ref.at[cur], 1 if first_trip else None)
        if last_trip:
            emit(even + 1, p1_ref.at[cur], d1_ref.at[cur])
        return carry

    scores(0, s0_ref.at[0], alibi_for(qi - si))
    trip(0, first_trip=True)
    for i in range(1, N_CHUNKS // 2 - 1):
        trip(i)
    trip(N_CHUNKS // 2 - 1, last_trip=True)


def _swa_attention(proj, sinks, layer):
    return pl.pallas_call(
        functools.partial(_swa_kernel, layer=layer),
        grid=(BATCH,),
        in_specs=[
            pl.BlockSpec(memory_space=pltpu.SMEM),
            pl.BlockSpec((SEQ, A_HEADS * HEAD_DIM), lambda b: (b, QA_BLK // N_PAIRS)),
            pl.BlockSpec((SEQ, LANES), lambda b: (b, KA_BLK)),
            pl.BlockSpec((SEQ, LANES), lambda b: (b, VA_BLK)),
        ],
        out_specs=pl.BlockSpec((SEQ, BRANCH_WIDTH), lambda b: (b, 0)),
        out_shape=jax.ShapeDtypeStruct((TOKENS, BRANCH_WIDTH), BF16),
        scratch_shapes=[pltpu.VMEM((A_KV_HEADS, SEQ, LANES), BF16),
                        pltpu.VMEM((A_KV_HEADS, SEQ, LANES), BF16),
                        pltpu.VMEM((2, A_KV_HEADS, A_GROUP * CHUNK, A_BAND), F32),
                        pltpu.VMEM((A_KV_HEADS, A_GROUP * CHUNK, A_BAND), F32),
                        pltpu.VMEM((A_KV_HEADS, A_GROUP * CHUNK, A_BAND), BF16),
                        pltpu.VMEM((2, A_KV_HEADS, A_GROUP * CHUNK, A_BAND), BF16),
                        pltpu.VMEM((A_KV_HEADS, A_GROUP * CHUNK, LANES), F32),
                        pltpu.VMEM((2, A_KV_HEADS, A_GROUP * CHUNK, LANES), F32)],
        compiler_params=_params(1),
        name="swa_attention",
    )(sinks, proj, proj, proj)


def _chunked_kernel(q_ref, k_ref, v_ref, bias_ref, o_ref, kp_ref, vp_ref,
                    s0_ref, s1_ref, p0_ref, p1_ref, d0_ref, d1_ref):
    first = _lane_half((1, LANES))
    width = N_PAIRS * LANES
    kp_ref[0:C_PAD, :] = jnp.zeros((C_PAD, width), BF16)
    vp_ref[0:C_PAD, :] = jnp.zeros((C_PAD, width), BF16)
    kp_ref[C_PAD:, :] = k_ref[...]
    vp_ref[C_PAD:, :] = v_ref[...]
    si = lax.broadcasted_iota(jnp.int32, (2 * CHUNK, C_BAND), 1)

    def scores(n, s_ref):
        q0 = _row_start(n, CHUNK)
        for pair in range(N_PAIRS):
            cols = slice(pair * LANES, (pair + 1) * LANES)
            q2 = q_ref[pl.ds(q0, CHUNK), cols]
            zero = jnp.zeros_like(q2)
            qs = jnp.concatenate([jnp.where(first, q2, zero), jnp.where(first, zero, q2)], axis=0)
            kb = kp_ref[pl.ds(q0, C_BAND), cols]
            s = lax.dot_general(qs, kb, (((1,), (1,)), ((), ())), preferred_element_type=F32)
            s_ref[pair] = s + bias_ref[pair]

    def softmax(n, s_ref, p_ref, d_ref, masked):
        for pair in range(N_PAIRS):
            s = s_ref[pair]
            if masked:
                s = jnp.where(n * CHUNK + si >= C_PAD, s, NEG_INF)
            p = jnp.exp2(s - jnp.max(s, axis=1, keepdims=True))
            d_ref[pair] = jnp.sum(p, axis=1, keepdims=True)
            p_ref[pair] = p.astype(BF16)

    def emit(n, p_ref, d_ref):
        q0 = _row_start(n, CHUNK)
        for pair in range(N_PAIRS):
            cols = slice(pair * LANES, (pair + 1) * LANES)
            vb = vp_ref[pl.ds(q0, C_BAND), cols]
            r = jnp.dot(p_ref[pair], vb, preferred_element_type=F32) / d_ref[pair]
            o_ref[pl.ds(q0, CHUNK), cols] = jnp.where(first, r[:CHUNK], r[CHUNK:]).astype(BF16)

    def trip(i, carry=0, *, masked, first_trip=False, last_trip=False):
        cur = i & 1
        even = 2 * i
        scores(even + 1, s1_ref)
        if not first_trip:
            emit(even - 1, p1_ref.at[1 - cur], d1_ref.at[1 - cur])
        softmax(even, s0_ref.at[cur], p0_ref, d0_ref, masked)
        if not last_trip:
            scores(even + 2, s0_ref.at[1 - cur])
        emit(even, p0_ref, d0_ref)
        softmax(even + 1, s1_ref, p1_ref.at[cur], d1_ref.at[cur], masked)
        if last_trip:
            emit(even + 1, p1_ref.at[cur], d1_ref.at[cur])
        return carry

    masked_trips = C_PREV // 2
    scores(0, s0_ref.at[0])
    trip(0, masked=True, first_trip=True)
    lax.fori_loop(1, masked_trips, functools.partial(trip, masked=True), 0)
    lax.fori_loop(masked_trips, N_CHUNKS // 2 - 1, functools.partial(trip, masked=False), 0)
    trip(N_CHUNKS // 2 - 1, masked=False, last_trip=True)


def _chunked_attention(proj, bias, layer):
    width = N_PAIRS * LANES
    return pl.pallas_call(
        _chunked_kernel,
        grid=(BATCH,),
        in_specs=[
            pl.BlockSpec((SEQ, width), lambda b: (b, QC_BLK // N_PAIRS)),
            pl.BlockSpec((SEQ, width), lambda b: (b, KC_BLK // N_PAIRS)),
            pl.BlockSpec((SEQ, width), lambda b: (b, VC_BLK // N_PAIRS)),
            pl.BlockSpec((None, N_PAIRS, 2 * CHUNK, C_BAND), lambda b: (layer, 0, 0, 0)),
        ],
        out_specs=pl.BlockSpec((SEQ, width), lambda b: (b, 0)),
        out_shape=jax.ShapeDtypeStruct((TOKENS, BRANCH_WIDTH), BF16),
        scratch_shapes=[pltpu.VMEM((C_PAD + SEQ, width), BF16), pltpu.VMEM((C_PAD + SEQ, width), BF16),
                        pltpu.VMEM((2, N_PAIRS, 2 * CHUNK, C_BAND), F32),
                        pltpu.VMEM((N_PAIRS, 2 * CHUNK, C_BAND), F32),
                        pltpu.VMEM((N_PAIRS, 2 * CHUNK, C_BAND), BF16),
                        pltpu.VMEM((2, N_PAIRS, 2 * CHUNK, C_BAND), BF16),
                        pltpu.VMEM((N_PAIRS, 2 * CHUNK, 1), F32),
                        pltpu.VMEM((2, N_PAIRS, 2 * CHUNK, 1), F32)],
        compiler_params=_params(1),
        name="chunked_attention",
    )(proj, proj, proj, bias)


def _merge_kernel(x_ref, mod_ref, g_ref, oa_ref, ob_ref, oc_ref, wb_ref, wg_ref, wo_ref, out_ref,
                  merged_ref, *, tn):
    h = _rms_mod(x_ref[...], g_ref[...], mod_ref[0:1, :], mod_ref[1:2, :]).astype(BF16)
    branches = (oa_ref[...], ob_ref[...], oc_ref[...])
    for n in range(D_MODEL // tn):
        acc = None
        for k, o in enumerate(branches):
            y = jnp.dot(o, wb_ref[k, :, n * tn:(n + 1) * tn], preferred_element_type=F32)
            gate = jnp.dot(h, wg_ref[:, k * D_MODEL + n * tn:k * D_MODEL + (n + 1) * tn],
                           preferred_element_type=F32)
            term = jax.nn.sigmoid(gate) * y
            acc = term if acc is None else acc + term
        merged_ref[:, n * tn:(n + 1) * tn] = acc.astype(BF16)
    merged = merged_ref[...]
    for n in range(D_MODEL // tn):
        sl = slice(n * tn, (n + 1) * tn)
        out = jnp.dot(merged, wo_ref[:, sl], preferred_element_type=F32)
        out_ref[:, sl] = x_ref[:, sl] + mod_ref[2:3, sl] * out


def _merge(x, mod, gain, o_a, o_b, o_c, w_branch, w_gate, w_out, layer):
    steps_per_batch = SEQ // TM
    row = lambda i: (i, 0)
    return pl.pallas_call(
        functools.partial(_merge_kernel, tn=256),
        grid=(TOKENS // TM,),
        in_specs=[
            pl.BlockSpec((TM, D_MODEL), row),
            pl.BlockSpec((None, None, 6, D_MODEL), lambda i: (layer, i // steps_per_batch, 0, 0)),
            pl.BlockSpec((None, 1, D_MODEL), lambda i: (layer, 0, 0)),
            pl.BlockSpec((TM, BRANCH_WIDTH), row),
            pl.BlockSpec((TM, BRANCH_WIDTH), row),
            pl.BlockSpec((TM, BRANCH_WIDTH), row),
            pl.BlockSpec((None, 3, BRANCH_WIDTH, D_MODEL), lambda i: (layer, 0, 0, 0)),
            pl.BlockSpec((None, D_MODEL, 3 * D_MODEL), lambda i: (layer, 0, 0)),
            pl.BlockSpec((None, D_MODEL, D_MODEL), lambda i: (layer, 0, 0)),
        ],
        out_specs=pl.BlockSpec((TM, D_MODEL), row),
        out_shape=jax.ShapeDtypeStruct((TOKENS, D_MODEL), F32),
        scratch_shapes=[pltpu.VMEM((TM, D_MODEL), BF16)],
        compiler_params=_params(1),
        name="merge_out",
    )(x, mod, gain, o_a, o_b, o_c, w_branch, w_gate, w_out)


def _ffn_kernel(x_ref, mod_ref, g_ref, gf_ref, wi_ref, wo_ref, out_ref, act_ref, *, tf, tn, final):
    h = _rms_mod(x_ref[...], g_ref[...], mod_ref[3:4, :], mod_ref[4:5, :]).astype(BF16)
    for c in range(FFN_HIDDEN // tf):
        gate = jnp.dot(h, wi_ref[:, c * tf:(c + 1) * tf], preferred_element_type=F32)
        up = jnp.dot(h, wi_ref[:, FFN_HIDDEN + c * tf:FFN_HIDDEN + (c + 1) * tf],
                     preferred_element_type=F32)
        act_ref[:, c * tf:(c + 1) * tf] = (gate * jax.nn.sigmoid(gate) * up).astype(BF16)
    act = act_ref[...]
    for n in range(D_MODEL // tn):
        sl = slice(n * tn, (n + 1) * tn)
        out = jnp.dot(act, wo_ref[:, sl], preferred_element_type=F32)
        out_ref[:, sl] = x_ref[:, sl] + mod_ref[5:6, sl] * out
    if final:
        y = out_ref[...]
        ms = jnp.mean(y * y, axis=-1, keepdims=True)
        out_ref[...] = y * lax.rsqrt(ms + EPS) * gf_ref[...]


def _ffn(x, mod, gain, final_gain, w_ffn_in, w_ffn_out, layer, final):
    steps_per_batch = SEQ // TM
    row = lambda i: (i, 0)
    return pl.pallas_call(
        functools.partial(_ffn_kernel, tf=256, tn=256, final=final),
        grid=(TOKENS // TM,),
        in_specs=[
            pl.BlockSpec((TM, D_MODEL), row),
            pl.BlockSpec((None, None, 6, D_MODEL), lambda i: (layer, i // steps_per_batch, 0, 0)),
            pl.BlockSpec((None, 1, D_MODEL), lambda i: (layer, 0, 0)),
            pl.BlockSpec((1, D_MODEL), lambda i: (0, 0)),
            pl.BlockSpec((None, D_MODEL, 2 * FFN_HIDDEN), lambda i: (layer, 0, 0), pipeline_mode=pl.Buffered(1)),
            pl.BlockSpec((None, FFN_HIDDEN, D_MODEL), lambda i: (layer, 0, 0), pipeline_mode=pl.Buffered(1)),
        ],
        out_specs=pl.BlockSpec((TM, D_MODEL), row),
        out_shape=jax.ShapeDtypeStruct((TOKENS, D_MODEL), F32),
        scratch_shapes=[pltpu.VMEM((TM, FFN_HIDDEN), BF16)],
        compiler_params=_params(1),
        name="ffn",
    )(x, mod, gain, final_gain, w_ffn_in, w_ffn_out)


def kernel(x, c, norm_mix_g, norm_ffn_g, w_ada, b_ada, w_in, b_forget, sinks, rel_bias,
           w_branch, w_out, w_ffn_in, w_ffn_out, final_norm_g):
    scale = HEAD_DIM ** -0.5
    w_in_t = jnp.swapaxes(w_in, 1, 2)
    w_qkv = _pack_w_in(w_in_t, (IN_B0, IN_C0, IN_A0), (IN_FB0 - IN_B0, IN_GATE0 - IN_C0, IN_B0 - IN_A0),
                       (scale * LOG2E,) * 3)
    w_gate = _pack_w_in(w_in_t, (IN_GATE0,), (N_IN_COLS - IN_GATE0,), (None,))
    w_fb = jnp.pad(w_in[:, :, IN_FB0:IN_FB0 + B_HEADS], ((0, 0), (0, 0), (0, LANES - B_HEADS))).astype(BF16)
    w_branch_b = w_branch.astype(BF16)
    w_out_b = w_out.astype(BF16)
    w_ffn_in_b = w_ffn_in.astype(BF16)
    w_ffn_out_b = w_ffn_out.astype(BF16)
    b_forget_rows = jnp.pad(b_forget, ((0, 0), (0, LANES - B_HEADS))).reshape(DEPTH, 1, LANES)
    gain_mix = norm_mix_g.reshape(DEPTH, 1, D_MODEL)
    gain_ffn = norm_ffn_g.reshape(DEPTH, 1, D_MODEL)
    gain_final = final_norm_g.reshape(1, D_MODEL)

    mod = _ada_mod(c, w_ada, b_ada)
    bias_c = _rel_bias(rel_bias)

    xt = x.reshape(TOKENS, D_MODEL)
    for layer in range(DEPTH):
        proj, fb = _in_proj(xt, mod, gain_mix, w_qkv, w_fb, layer)
        ccol, crow = _forget_cumsum(fb, b_forget_rows[layer])
        o_a = _swa_attention(proj, sinks, layer)
        o_b = _fox_attention(proj, ccol, crow)
        o_c = _chunked_attention(proj, bias_c, layer)
        xt = _merge(xt, mod, gain_mix, o_a, o_b, o_c, w_branch_b, w_gate, w_out_b, layer)
        xt = _ffn(xt, mod, gain_ffn, gain_final, w_ffn_in_b, w_ffn_out_b, layer, layer == DEPTH - 1)
    return xt.reshape(BATCH, SEQ, D_MODEL)
```
